```python
import jax, jax.numpy as jnp
from jax import lax
import numpy as np

D_MODEL = 1024
BATCH = 16
SEQ = 256
DEPTH = 2
DEC_BATCH = 4
DEC_SEQ = 4096
PAST_LEN = 256

GRID_W = 64
Q_BLOCK = 128
ROPE_THETA = 10000.0
RMS_EPS = 1e-6
NEG_INF = -1e30
MLA_HEADS = 8
MLA_Q_LORA = 256
MLA_KV_LORA = 256
MLA_NOPE_DIM = 64
MLA_ROPE_DIM = 32
MLA_V_DIM = 64
MLA_QK_DIM = MLA_NOPE_DIM + MLA_ROPE_DIM
MLA_SCALE = MLA_QK_DIM ** -0.5
NA_HEADS = 8
NA_HEAD_DIM = 64
NA_WIN_H = 8
NA_WIN_W = 16
NA_SCALE = NA_HEAD_DIM ** -0.5
W_IN_A = MLA_Q_LORA + MLA_KV_LORA + MLA_ROPE_DIM + 3 * NA_HEADS * NA_HEAD_DIM
W_OUT_A = MLA_HEADS * MLA_V_DIM + NA_HEADS * NA_HEAD_DIM
GQA_HEADS = 8
GQA_KV_HEADS = 2
GQA_HEAD_DIM = 128
GQA_SCALE = GQA_HEAD_DIM ** -0.5
W_IN_C = (GQA_HEADS + 2 * GQA_KV_HEADS) * GQA_HEAD_DIM
W_OUT_C = GQA_HEADS * GQA_HEAD_DIM
D_FF = -(-8 * D_MODEL // (3 * 256)) * 256
N_EVEN = (DEPTH + 1) // 2
N_ODD = DEPTH // 2

kernel_name = "hybrid_mla_natten_gqa_prefix_diffusion_step"


def rms_norm(x, g):
    xf = x.astype(jnp.float32)
    y = xf * lax.rsqrt(jnp.mean(xf * xf, axis=-1, keepdims=True) + RMS_EPS)
    return (y * g.astype(jnp.float32)).astype(x.dtype)


def ada_modulation(cond, w_mod, b_mod):
    m = jax.nn.silu(cond) @ w_mod + b_mod
    return jnp.split(m, 6, axis=-1)


def modulate(h, shift, scale):
    return h * (1.0 + scale) + shift


def axial_rope_tables(n_tokens, rot_dim):
    t = jnp.arange(n_tokens)
    row = (t // GRID_W).astype(jnp.float32)
    col = (t % GRID_W).astype(jnp.float32)
    axis_dim = rot_dim // 2
    inv_freq = ROPE_THETA ** (-jnp.arange(0, axis_dim, 2, dtype=jnp.float32) / axis_dim)
    ang = jnp.concatenate([row[:, None] * inv_freq, col[:, None] * inv_freq], axis=-1)
    return jnp.cos(ang), jnp.sin(ang)


def apply_rope(x, cos, sin):
    half = x.shape[-1] // 2
    xf = x.astype(jnp.float32)
    x1, x2 = xf[..., :half], xf[..., half:]
    c = cos[None, :, None, :]
    s = sin[None, :, None, :]
    return jnp.concatenate([x1 * c - x2 * s, x1 * s + x2 * c], axis=-1).astype(x.dtype)


def attend_blocked(q, k, v, scale):
    b, t, h, dq = q.shape
    hk, dv = k.shape[2], v.shape[-1]
    g = h // hk
    nb = t // Q_BLOCK
    qb = q.reshape(b, nb, Q_BLOCK, hk, g, dq).transpose(1, 0, 2, 3, 4, 5)

    def block(qi):
        s = jnp.einsum('bqkgd,bskd->bkgqs', qi, k).astype(jnp.float32) * scale
        p = jax.nn.softmax(s, axis=-1).astype(v.dtype)
        return jnp.einsum('bkgqs,bskd->bqkgd', p, v)

    o = lax.map(block, qb)
    return o.transpose(1, 0, 2, 3, 4, 5).reshape(b, t, h, dv)


def neighbourhood_attention(q, k, v, k_ctx, v_ctx, rpb):
    b, t, h, d = q.shape
    n_rows = t // GRID_W
    wh = min(NA_WIN_H, n_rows)
    qg = q.reshape(b, n_rows, GRID_W, h, d).transpose(1, 0, 2, 3, 4)
    kg = k.reshape(b, n_rows, GRID_W, h, d)
    vg = v.reshape(b, n_rows, GRID_W, h, d)
    rows = jnp.arange(n_rows)
    row_start = jnp.clip(rows - wh // 2, 0, n_rows - wh)
    cols = jnp.arange(GRID_W)
    col_start = jnp.clip(cols - NA_WIN_W // 2, 0, GRID_W - NA_WIN_W)
    col_mask = (cols[None, :] >= col_start[:, None]) & (cols[None, :] < col_start[:, None] + NA_WIN_W)
    dc_idx = jnp.clip(cols[None, :] - cols[:, None] + NA_WIN_W - 1, 0, 2 * NA_WIN_W - 2)

    def one_row(args):
        q_r, r, rs = args
        k_band = lax.dynamic_slice_in_dim(kg, rs, wh, axis=1)
        v_band = lax.dynamic_slice_in_dim(vg, rs, wh, axis=1)
        dr_idx = rs + jnp.arange(wh) - r + NA_WIN_H - 1
        bias = rpb[:, dr_idx][:, :, dc_idx]
        s_band = jnp.einsum('bqhd,bikhd->bhqik', q_r, k_band).astype(jnp.float32) * NA_SCALE
        s_band = s_band + bias.transpose(0, 2, 1, 3)[None].astype(jnp.float32)
        s_band = jnp.where(col_mask[:, None, :], s_band, NEG_INF)
        s_band = s_band.reshape(b, h, GRID_W, wh * GRID_W)
        s_ctx = jnp.einsum('bqhd,bchd->bhqc', q_r, k_ctx).astype(jnp.float32) * NA_SCALE
        p = jax.nn.softmax(jnp.concatenate([s_band, s_ctx], axis=-1), axis=-1).astype(v.dtype)
        p_band = p[..., :wh * GRID_W].reshape(b, h, GRID_W, wh, GRID_W)
        p_ctx = p[..., wh * GRID_W:]
        return (jnp.einsum('bhqik,bikhd->bqhd', p_band, v_band)
                + jnp.einsum('bhqc,bchd->bqhd', p_ctx, v_ctx))

    o = lax.map(one_row, (qg, rows, row_start))
    return o.transpose(1, 0, 2, 3, 4).reshape(b, t, h, d)


def even_project(h, w_in, q_norm, w_uq, kv_norm, w_ukv):
    b, L, _ = h.shape
    p = h @ w_in
    i0 = MLA_Q_LORA
    i1 = i0 + MLA_KV_LORA
    i2 = i1 + MLA_ROPE_DIM
    na_w = NA_HEADS * NA_HEAD_DIM
    cq, ckv, krope, nq, nk, nv = jnp.split(p, [i0, i1, i2, i2 + na_w, i2 + 2 * na_w], axis=-1)
    q = (rms_norm(cq, q_norm) @ w_uq).reshape(b, L, MLA_HEADS, MLA_QK_DIM)
    ckv = rms_norm(ckv, kv_norm)
    shp = (b, L, NA_HEADS, NA_HEAD_DIM)
    return q, ckv, krope, nq.reshape(shp), nk.reshape(shp), nv.reshape(shp)


def mla_expand(ckv, w_ukv, k_rope):
    b, L, _ = ckv.shape
    kv = (ckv @ w_ukv).reshape(b, L, MLA_HEADS, MLA_NOPE_DIM + MLA_V_DIM)
    k_nope, v = kv[..., :MLA_NOPE_DIM], kv[..., MLA_NOPE_DIM:]
    k_r = jnp.broadcast_to(k_rope[:, :, None, :], (b, L, MLA_HEADS, MLA_ROPE_DIM))
    return jnp.concatenate([k_nope, k_r], axis=-1), v


def even_mixer_context(h, w_in, q_norm, w_uq, kv_norm, w_ukv, w_out):
    b, L, _ = h.shape
    q, ckv, krope, nq, nk, nv = even_project(h, w_in, q_norm, w_uq, kv_norm, w_ukv)
    k, v = mla_expand(ckv, w_ukv, krope)
    a_mla = attend_blocked(q, k, v, MLA_SCALE)
    a_na = attend_blocked(nq, nk, nv, NA_SCALE)
    out = jnp.concatenate([a_mla.reshape(b, L, -1), a_na.reshape(b, L, -1)], axis=-1) @ w_out
    return out, ckv, krope, nk, nv


def even_mixer_latent(h, c_ckv, c_krope, c_nk, c_nv, w_in, q_norm, w_uq, kv_norm, w_ukv, rpb, w_out,
                      cos, sin):
    b, L, _ = h.shape
    q, ckv, krope, nq, nk, nv = even_project(h, w_in, q_norm, w_uq, kv_norm, w_ukv)
    q = jnp.concatenate([q[..., :MLA_NOPE_DIM], apply_rope(q[..., MLA_NOPE_DIM:], cos, sin)], axis=-1)
    krope = apply_rope(krope[:, :, None, :], cos, sin)[:, :, 0, :]
    k_lat, v_lat = mla_expand(ckv, w_ukv, krope)
    k_ctx, v_ctx = mla_expand(c_ckv, w_ukv, c_krope)
    a_mla = attend_blocked(q, jnp.concatenate([k_lat, k_ctx], axis=1),
                           jnp.concatenate([v_lat, v_ctx], axis=1), MLA_SCALE)
    a_na = neighbourhood_attention(nq, nk, nv, c_nk, c_nv, rpb)
    return jnp.concatenate([a_mla.reshape(b, L, -1), a_na.reshape(b, L, -1)], axis=-1) @ w_out


def odd_project(h, w_in, q_norm, k_norm):
    b, L, _ = h.shape
    p = h @ w_in
    q, k, v = jnp.split(p, [GQA_HEADS * GQA_HEAD_DIM, (GQA_HEADS + GQA_KV_HEADS) * GQA_HEAD_DIM], axis=-1)
    q = rms_norm(q.reshape(b, L, GQA_HEADS, GQA_HEAD_DIM), q_norm)
    k = rms_norm(k.reshape(b, L, GQA_KV_HEADS, GQA_HEAD_DIM), k_norm)
    v = v.reshape(b, L, GQA_KV_HEADS, GQA_HEAD_DIM)
    return q, k, v


def odd_mixer_context(h, w_in, q_norm, k_norm, w_out):
    b, L, _ = h.shape
    q, k, v = odd_project(h, w_in, q_norm, k_norm)
    out = attend_blocked(q, k, v, GQA_SCALE).reshape(b, L, -1) @ w_out
    return out, k, v


def odd_mixer_latent(h, c_k, c_v, w_in, q_norm, k_norm, w_out, cos, sin):
    b, L, _ = h.shape
    q, k, v = odd_project(h, w_in, q_norm, k_norm)
    q = apply_rope(q, cos, sin)
    k = apply_rope(k, cos, sin)
    o = attend_blocked(q, jnp.concatenate([k, c_k], axis=1), jnp.concatenate([v, c_v], axis=1), GQA_SCALE)
    return o.reshape(b, L, -1) @ w_out


def swiglu(h, w_in, w_out):
    gate, up = jnp.split(h @ w_in, 2, axis=-1)
    return (jax.nn.silu(gate) * up) @ w_out


def setup_inputs(seed: int = 0) -> dict:
    key = jax.random.key(seed)
    ks = iter(jax.random.split(key, 40))

    def nrm(shape, scale=1.0):
        return jax.random.normal(next(ks), shape, jnp.float32) * scale

    def gain(shape):
        return 1.0 + nrm(shape, 0.05)

    d = D_MODEL
    return {
        "x_prompt": nrm((BATCH, SEQ, d)),
        "x_sample": nrm((DEC_BATCH, DEC_SEQ, d)),
        "cache_mla_ckv": nrm((DEC_BATCH, N_EVEN, PAST_LEN, MLA_KV_LORA)),
        "cache_mla_krope": nrm((DEC_BATCH, N_EVEN, PAST_LEN, MLA_ROPE_DIM)),
        "cache_na_k": nrm((DEC_BATCH, N_EVEN, PAST_LEN, NA_HEADS, NA_HEAD_DIM)),
        "cache_na_v": nrm((DEC_BATCH, N_EVEN, PAST_LEN, NA_HEADS, NA_HEAD_DIM)),
        "cache_gqa_k": nrm((DEC_BATCH, N_ODD, PAST_LEN, GQA_KV_HEADS, GQA_HEAD_DIM)),
        "cache_gqa_v": nrm((DEC_BATCH, N_ODD, PAST_LEN, GQA_KV_HEADS, GQA_HEAD_DIM)),
        "c": nrm((DEC_BATCH, d)),
        "c_ctx": nrm((d,)),
        "w_mod": nrm((DEPTH, d, 6 * d), 0.5 * d ** -0.5),
        "b_mod": nrm((DEPTH, 6 * d), 0.02),
        "norm_mix": gain((DEPTH, d)),
        "norm_ffn": gain((DEPTH, d)),
        "norm_final": gain((d,)),
        "w_in_a": nrm((N_EVEN, d, W_IN_A), d ** -0.5),
        "mla_q_norm": gain((N_EVEN, MLA_Q_LORA)),
        "mla_w_uq": nrm((N_EVEN, MLA_Q_LORA, MLA_HEADS * MLA_QK_DIM), MLA_Q_LORA ** -0.5),
        "mla_kv_norm": gain((N_EVEN, MLA_KV_LORA)),
        "mla_w_ukv": nrm((N_EVEN, MLA_KV_LORA, MLA_HEADS * (MLA_NOPE_DIM + MLA_V_DIM)), MLA_KV_LORA ** -0.5),
        "na_rpb": nrm((N_EVEN, NA_HEADS, 2 * NA_WIN_H - 1, 2 * NA_WIN_W - 1), 0.5),
        "w_out_a": nrm((N_EVEN, W_OUT_A, d), W_OUT_A ** -0.5),
        "w_in_c": nrm((N_ODD, d, W_IN_C), d ** -0.5),
        "gqa_q_norm": gain((N_ODD, GQA_HEAD_DIM)),
        "gqa_k_norm": gain((N_ODD, GQA_HEAD_DIM)),
        "w_out_c": nrm((N_ODD, W_OUT_C, d), W_OUT_C ** -0.5),
        "w_ffn_in": nrm((DEPTH, d, 2 * D_FF), d ** -0.5),
        "w_ffn_out": nrm((DEPTH, D_FF, d), D_FF ** -0.5),
    }


def reference(x_prompt, x_sample, cache_mla_ckv, cache_mla_krope, cache_na_k, cache_na_v, cache_gqa_k,
              cache_gqa_v, c, c_ctx, w_mod, b_mod, norm_mix, norm_ffn, norm_final, w_in_a, mla_q_norm,
              mla_w_uq, mla_kv_norm, mla_w_ukv, na_rpb, w_out_a, w_in_c, gqa_q_norm, gqa_k_norm, w_out_c,
              w_ffn_in, w_ffn_out):
    xp = x_prompt
    xs = x_sample
    n_lat = x_sample.shape[1]
    cos_m, sin_m = axial_rope_tables(n_lat, MLA_ROPE_DIM)
    cos_g, sin_g = axial_rope_tables(n_lat, GQA_HEAD_DIM)
    cond_ctx = c_ctx[None, None, :]
    cond_lat = c[:, None, :]
    st_ckv, st_krope, st_nk, st_nv, st_gk, st_gv = [], [], [], [], [], []

    for l in range(DEPTH):
        sh1_p, sc1_p, g1_p, sh2_p, sc2_p, g2_p = ada_modulation(cond_ctx, w_mod[l], b_mod[l])
        sh1_s, sc1_s, g1_s, sh2_s, sc2_s, g2_s = ada_modulation(cond_lat, w_mod[l], b_mod[l])
        hp = modulate(rms_norm(xp, norm_mix[l]), sh1_p, sc1_p)
        hs = modulate(rms_norm(xs, norm_mix[l]), sh1_s, sc1_s)
        if l % 2 == 0:
            e = l // 2
            out_p, ckv, krope, nk, nv = even_mixer_context(
                hp, w_in_a[e], mla_q_norm[e], mla_w_uq[e], mla_kv_norm[e], mla_w_ukv[e], w_out_a[e])
            st_ckv.append(ckv)
            st_krope.append(krope)
            st_nk.append(nk)
            st_nv.append(nv)
            out_s = even_mixer_latent(
                hs, cache_mla_ckv[:, e], cache_mla_krope[:, e], cache_na_k[:, e], cache_na_v[:, e],
                w_in_a[e], mla_q_norm[e], mla_w_uq[e], mla_kv_norm[e], mla_w_ukv[e], na_rpb[e], w_out_a[e],
                cos_m, sin_m)
        else:
            o = l // 2
            out_p, gk, gv = odd_mixer_context(hp, w_in_c[o], gqa_q_norm[o], gqa_k_norm[o], w_out_c[o])
            st_gk.append(gk)
            st_gv.append(gv)
            out_s = odd_mixer_latent(hs, cache_gqa_k[:, o], cache_gqa_v[:, o], w_in_c[o], gqa_q_norm[o],
                                     gqa_k_norm[o], w_out_c[o], cos_g, sin_g)
        xp = xp + g1_p * out_p
        xs = xs + g1_s * out_s
        hp = modulate(rms_norm(xp, norm_ffn[l]), sh2_p, sc2_p)
        hs = modulate(rms_norm(xs, norm_ffn[l]), sh2_s, sc2_s)
        xp = xp + g2_p * swiglu(hp, w_ffn_in[l], w_ffn_out[l])
        xs = xs + g2_s * swiglu(hs, w_ffn_in[l], w_ffn_out[l])

    y_prompt = rms_norm(xp, norm_final)
    y_sample = rms_norm(xs, norm_final)
    state_mla_ckv = jnp.stack(st_ckv, axis=1)
    state_mla_krope = jnp.stack(st_krope, axis=1)
    state_na_k = jnp.stack(st_nk, axis=1)
    state_na_v = jnp.stack(st_nv, axis=1)
    state_gqa_k = jnp.stack(st_gk, axis=1)
    state_gqa_v = jnp.stack(st_gv, axis=1)
    return (y_prompt, y_sample, state_mla_ckv, state_mla_krope, state_na_k, state_na_v, state_gqa_k, state_gqa_v)
```

```python
import functools
import math

import numpy as np
import jax
import jax.numpy as jnp
from jax import lax
from jax.experimental import pallas as pl
from jax.experimental.pallas import tpu as pltpu

LANES = 128
V7X_VMEM_BYTES = 64 * 1024 * 1024

D_MODEL = 1024
GRID_W = 64
ROPE_THETA = 10000.0
RMS_EPS = 1e-6
NEG_INF = -1e30
MLA_HEADS = 8
MLA_Q_LORA = 256
MLA_KV_LORA = 256
MLA_NOPE_DIM = 64
MLA_ROPE_DIM = 32
MLA_V_DIM = 64
MLA_QK_DIM = MLA_NOPE_DIM + MLA_ROPE_DIM
MLA_SCALE = MLA_QK_DIM ** -0.5
NA_HEADS = 8
NA_HEAD_DIM = 64
NA_WIN_H = 8
NA_WIN_W = 16
NA_SCALE = NA_HEAD_DIM ** -0.5
NA_W = NA_HEADS * NA_HEAD_DIM
GQA_HEADS = 8
GQA_KV_HEADS = 2
GQA_HEAD_DIM = 128
GQA_SCALE = GQA_HEAD_DIM ** -0.5
GQA_GROUP = GQA_HEADS // GQA_KV_HEADS

TOKEN_TILE = 512
ATTN_Q_TILE = 256
ATTN_K_CHUNK = 512
NA_Q_ROWS = 4
NA_BAND_ROWS = 12

BF16 = jnp.bfloat16
F32 = jnp.float32


def _vmem_limit(nbytes):
    return int(min(V7X_VMEM_BYTES - (4 << 20), max(nbytes, 16 << 20)))


def _params(nbytes, ndims):
    return pltpu.CompilerParams(dimension_semantics=("arbitrary",) * ndims,
                                vmem_limit_bytes=_vmem_limit(nbytes))


def _rms(x, gain):
    return x * lax.rsqrt(jnp.mean(x * x, axis=-1, keepdims=True) + RMS_EPS) * gain


def _dot(a, b):
    return jnp.dot(a, b, preferred_element_type=F32)


def _dot_nt(a, b):
    return lax.dot_general(a, b, (((1,), (1,)), ((), ())), preferred_element_type=F32)


def _const_spec(shape):
    nd = len(shape)
    return pl.BlockSpec(shape, lambda *_: (0,) * nd, pipeline_mode=pl.Buffered(1))


def _mod_kernel(cond_ref, w_ref, b_ref, o_ref):
    c = cond_ref[...]
    s = (c * jax.nn.sigmoid(c)).astype(BF16)
    o_ref[0] = _dot(s, w_ref[0].astype(BF16)) + b_ref[0]


def _modulation(cond, w_mod, b_mod):
    depth, d, n = w_mod.shape
    rows = cond.shape[0]
    bn = 1024
    return pl.pallas_call(
        _mod_kernel,
        out_shape=jax.ShapeDtypeStruct((depth, rows, n), F32),
        grid=(depth, n // bn),
        in_specs=[pl.BlockSpec((rows, d), lambda l, j: (0, 0)),
                  pl.BlockSpec((1, d, bn), lambda l, j: (l, 0, j)),
                  pl.BlockSpec((1, 1, bn), lambda l, j: (l, 0, j))],
        out_specs=pl.BlockSpec((1, rows, bn), lambda l, j: (l, 0, j)),
        compiler_params=_params(3 * d * bn * 4, 2),
        name="ada_modulation",
    )(cond, w_mod, b_mod.reshape(depth, 1, n))


def _even_in_kernel(x_ref, mod_ref, gmix_ref, w_in_ref, qn_ref, kvn_ref, w_uq_ref, w_ukv_ref,
                    cos_ref, sin_ref, *out_refs, with_state):
    q_ref, k_ref, v_ref, nq_ref, nk_ref, nv_ref = out_refs[:6]
    x = x_ref[...]
    mod = mod_ref[0]
    h = _rms(x, gmix_ref[...]) * (1.0 + mod[1:2]) + mod[0:1]
    p = _dot(h.astype(BF16), w_in_ref[...])
    cq = p[:, 0:256]
    ckv = _rms(p[:, 256:512], kvn_ref[...])
    nq = p[:, 512:1024]
    nk = p[:, 1024:1536]
    nv = p[:, 1536:2048]
    kr = p[:, 2048:2176]
    kr_sw = p[:, 2176:2304]
    cos = cos_ref[...]
    sin = sin_ref[...]
    qq = _dot(_rms(cq, qn_ref[...]).astype(BF16), w_uq_ref[...])
    kv = _dot(ckv.astype(BF16), w_ukv_ref[...])
    kr_rot = kr * cos + kr_sw * sin
    for hd in range(MLA_HEADS):
        lo = hd * LANES
        qh = qq[:, lo:lo + LANES] * cos + qq[:, 1024 + lo:1024 + lo + LANES] * sin
        q_ref[:, lo:lo + LANES] = (qh * MLA_SCALE).astype(BF16)
        k_ref[:, lo:lo + LANES] = (kv[:, lo:lo + LANES] + kr_rot).astype(BF16)
    v_ref[...] = kv[:, 1024:1536].astype(BF16)
    nq_ref[...] = (nq * NA_SCALE).astype(BF16)
    nk_ref[...] = nk.astype(BF16)
    nv_ref[...] = nv.astype(BF16)
    if with_state:
        s_ckv_ref, s_kr_ref, s_nk_ref, s_nv_ref = out_refs[6:]
        s_ckv_ref[...] = ckv
        s_kr_ref[...] = kr
        s_nk_ref[...] = nk
        s_nv_ref[...] = nv


def _even_in(x, mod, tokens_per_group, gmix, w_in, qn, kvn, w_uq, w_ukv, cos, sin, with_state):
    n = x.shape[0]
    tm = TOKEN_TILE
    tiles_per_group = tokens_per_group // tm
    rope_tiles = cos.shape[0] // tm
    row = lambda i: (i, 0)
    outs = [jax.ShapeDtypeStruct((n, 1024), BF16), jax.ShapeDtypeStruct((n, 1024), BF16),
            jax.ShapeDtypeStruct((n, 512), BF16), jax.ShapeDtypeStruct((n, 512), BF16),
            jax.ShapeDtypeStruct((n, 512), BF16), jax.ShapeDtypeStruct((n, 512), BF16)]
    if with_state:
        outs += [jax.ShapeDtypeStruct((n, 256), F32), jax.ShapeDtypeStruct((n, 128), F32),
                 jax.ShapeDtypeStruct((n, 512), F32), jax.ShapeDtypeStruct((n, 512), F32)]
    return pl.pallas_call(
        functools.partial(_even_in_kernel, with_state=with_state),
        out_shape=outs,
        grid=(n // tm,),
        in_specs=[pl.BlockSpec((tm, D_MODEL), row),
                  pl.BlockSpec((1, 6, D_MODEL), lambda i: (i // tiles_per_group, 0, 0)),
                  _const_spec(gmix.shape), _const_spec(w_in.shape), _const_spec(qn.shape),
                  _const_spec(kvn.shape), _const_spec(w_uq.shape), _const_spec(w_ukv.shape),
                  pl.BlockSpec((tm, LANES), lambda i: (i % rope_tiles, 0)),
                  pl.BlockSpec((tm, LANES), lambda i: (i % rope_tiles, 0))],
        out_specs=[pl.BlockSpec((tm, o.shape[1]), row) for o in outs],
        compiler_params=_params(40 << 20, 1),
        name="even_in",
    )(x, mod, gmix, w_in, qn, kvn, w_uq, w_ukv, cos, sin)


def _cache_expand_kernel(ckv_ref, kr_ref, w_ukv_ref, k_ref, v_ref):
    kv = _dot(ckv_ref[0].astype(BF16), w_ukv_ref[...])
    kr = kr_ref[0]
    for hd in range(MLA_HEADS):
        lo = hd * LANES
        k_ref[0, :, lo:lo + LANES] = (kv[:, lo:lo + LANES] + kr).astype(BF16)
    v_ref[0] = kv[:, 1024:1536].astype(BF16)


def _cache_expand(ckv, kr128, w_ukv):
    b, s, _ = ckv.shape
    return pl.pallas_call(
        _cache_expand_kernel,
        out_shape=[jax.ShapeDtypeStruct((b, s, 1024), BF16), jax.ShapeDtypeStruct((b, s, 512), BF16)],
        grid=(b,),
        in_specs=[pl.BlockSpec((1, s, MLA_KV_LORA), lambda i: (i, 0, 0)),
                  pl.BlockSpec((1, s, LANES), lambda i: (i, 0, 0)),
                  _const_spec(w_ukv.shape)],
        out_specs=[pl.BlockSpec((1, s, 1024), lambda i: (i, 0, 0)),
                   pl.BlockSpec((1, s, 512), lambda i: (i, 0, 0))],
        compiler_params=_params(16 << 20, 1),
        name="mla_cache_expand",
    )(ckv, kr128, w_ukv)


def _odd_in_kernel(x_ref, mod_ref, gmix_ref, w_in_ref, qn_ref, kn_ref, cos_ref, sin_ref, *out_refs,
                   with_state):
    q_ref, k_ref, v_ref = out_refs[:3]
    x = x_ref[...]
    mod = mod_ref[0]
    h = _rms(x, gmix_ref[...]) * (1.0 + mod[1:2]) + mod[0:1]
    p = _dot(h.astype(BF16), w_in_ref[...])
    cos = cos_ref[...]
    sin = sin_ref[...]
    half = GQA_HEAD_DIM // 2

    def rope(t):
        return t * cos + pltpu.roll(t, half, 1) * sin

    for hd in range(GQA_HEADS):
        lo = hd * LANES
        qh = rope(_rms(p[:, lo:lo + LANES], qn_ref[...]))
        q_ref[:, lo:lo + LANES] = (qh * GQA_SCALE).astype(BF16)
    for hd in range(GQA_KV_HEADS):
        lo = hd * LANES
        kh = _rms(p[:, 1024 + lo:1024 + lo + LANES], kn_ref[...])
        vh = p[:, 1280 + lo:1280 + lo + LANES]
        k_ref[:, lo:lo + LANES] = rope(kh).astype(BF16)
        v_ref[:, lo:lo + LANES] = vh.astype(BF16)
        if with_state:
            out_refs[3][:, lo:lo + LANES] = kh
            out_refs[4][:, lo:lo + LANES] = vh


def _odd_in(x, mod, tokens_per_group, gmix, w_in, qn, kn, cos, sin, with_state):
    n = x.shape[0]
    tm = TOKEN_TILE
    tiles_per_group = tokens_per_group // tm
    rope_tiles = cos.shape[0] // tm
    row = lambda i: (i, 0)
    outs = [jax.ShapeDtypeStruct((n, 1024), BF16), jax.ShapeDtypeStruct((n, 256), BF16),
            jax.ShapeDtypeStruct((n, 256), BF16)]
    if with_state:
        outs += [jax.ShapeDtypeStruct((n, 256), F32), jax.ShapeDtypeStruct((n, 256), F32)]
    return pl.pallas_call(
        functools.partial(_odd_in_kernel, with_state=with_state),
        out_shape=outs,
        grid=(n // tm,),
        in_specs=[pl.BlockSpec((tm, D_MODEL), row),
                  pl.BlockSpec((1, 6, D_MODEL), lambda i: (i // tiles_per_group, 0, 0)),
                  _const_spec(gmix.shape), _const_spec(w_in.shape), _const_spec(qn.shape),
                  _const_spec(kn.shape),
                  pl.BlockSpec((tm, LANES), lambda i: (i % rope_tiles, 0)),
                  pl.BlockSpec((tm, LANES), lambda i: (i % rope_tiles, 0))],
        out_specs=[pl.BlockSpec((tm, o.shape[1]), row) for o in outs],
        compiler_params=_params(32 << 20, 1),
        name="odd_in",
    )(x, mod, gmix, w_in, qn, kn, cos, sin)


def _attn_kernel(*refs, n_src, heads, q_tile, src_len, k_stride, q_half_mask, pair_out):
    q_ref = refs[0]
    kv_refs = refs[1:1 + 2 * n_src]
    o_ref = refs[1 + 2 * n_src]
    s_ref = refs[2 + 2 * n_src]
    chunks = []
    for src in range(n_src):
        ck = min(ATTN_K_CHUNK, src_len[src])
        for c in range(src_len[src] // ck):
            chunks.append((src, c * ck, ck))
    lane = lax.broadcasted_iota(jnp.int32, (q_tile, LANES), 1)
    outs = []
    for j in range(heads):
        if q_half_mask:
            qb = q_ref[0]
            q = jnp.where((lane >= 64) == (j == 1), qb, jnp.zeros_like(qb))
        else:
            q = q_ref[0, :, j * LANES:(j + 1) * LANES]
        m_part = jnp.full((q_tile, LANES), -jnp.inf, F32)
        off = 0
        for src, k0, ck in chunks:
            k_ref = kv_refs[2 * src]
            kc = k_ref[0, k0:k0 + ck, j * k_stride:j * k_stride + LANES]
            s = _dot_nt(q, kc)
            s_ref[:, off:off + ck] = s
            for t in range(ck // LANES):
                m_part = jnp.maximum(m_part, s[:, t * LANES:(t + 1) * LANES])
            off += ck
        m = jnp.max(m_part, axis=-1, keepdims=True)
        l_part = jnp.zeros((q_tile, LANES), F32)
        acc = jnp.zeros((q_tile, LANES), F32)
        off = 0
        for src, k0, ck in chunks:
            v_ref = kv_refs[2 * src + 1]
            p = jnp.exp(s_ref[:, off:off + ck] - m)
            for t in range(ck // LANES):
                l_part = l_part + p[:, t * LANES:(t + 1) * LANES]
            acc = acc + _dot(p.astype(BF16), v_ref[0, k0:k0 + ck, :])
            off += ck
        outs.append(acc / jnp.sum(l_part, axis=-1, keepdims=True))
    if pair_out:
        o_ref[0] = jnp.where(lane < 64, outs[0], outs[1]).astype(o_ref.dtype)
    else:
        for j in range(heads):
            o_ref[0, :, j * LANES:(j + 1) * LANES] = outs[j].astype(o_ref.dtype)


def _attention(q, sources, *, groups, heads, k_stride, q_half_mask, pair_out, name):
    b, t, _ = q.shape
    q_tile = min(ATTN_Q_TILE, t)
    q_block = LANES if q_half_mask else heads * LANES
    k_block = LANES if k_stride == 0 else heads * LANES
    out_block = LANES if pair_out else heads * LANES
    src_len = tuple(k.shape[1] for k, _ in sources)
    in_specs = [pl.BlockSpec((1, q_tile, q_block), lambda bi, g, qi: (bi, qi, g))]
    args = [q]
    for k, v in sources:
        s = k.shape[1]
        in_specs.append(pl.BlockSpec((1, s, k_block), lambda bi, g, qi: (bi, 0, g)))
        in_specs.append(pl.BlockSpec((1, s, LANES), lambda bi, g, qi: (bi, 0, g)))
        args += [k, v]
    total = sum(src_len)
    return pl.pallas_call(
        functools.partial(_attn_kernel, n_src=len(sources), heads=heads, q_tile=q_tile,
                          src_len=src_len, k_stride=k_stride, q_half_mask=q_half_mask,
                          pair_out=pair_out),
        out_shape=jax.ShapeDtypeStruct((b, t, groups * out_block), BF16),
        grid=(b, groups, t // q_tile),
        in_specs=in_specs,
        out_specs=pl.BlockSpec((1, q_tile, out_block), lambda bi, g, qi: (bi, qi, g)),
        scratch_shapes=[pltpu.VMEM((q_tile, total), F32)],
        compiler_params=_params(48 << 20, 3),
        name=name,
    )(*args)


def _na_bias_tables(rpb):
    n_rows = GRID_W
    out = []
    for blk in (0, 1, n_rows // NA_Q_ROWS - 1):
        b0 = int(np.clip(NA_Q_ROWS * blk - NA_WIN_H // 2, 0, n_rows - NA_BAND_ROWS))
        r = (NA_Q_ROWS * blk + np.arange(NA_Q_ROWS))[:, None, None, None]
        c = np.arange(GRID_W)[None, :, None, None]
        kr = (b0 + np.arange(NA_BAND_ROWS))[None, None, :, None]
        kc = np.arange(GRID_W)[None, None, None, :]
        rs = np.clip(r - NA_WIN_H // 2, 0, n_rows - NA_WIN_H)
        cs = np.clip(c - NA_WIN_W // 2, 0, GRID_W - NA_WIN_W)
        valid = (kr >= rs) & (kr < rs + NA_WIN_H) & (kc >= cs) & (kc < cs + NA_WIN_W)
        dr = np.clip(kr - r + NA_WIN_H - 1, 0, 2 * NA_WIN_H - 2)
        dc = np.clip(kc - c + NA_WIN_W - 1, 0, 2 * NA_WIN_W - 2)
        shape = (NA_Q_ROWS, GRID_W, NA_BAND_ROWS, GRID_W)
        dr = np.broadcast_to(dr, shape).reshape(NA_Q_ROWS * GRID_W, NA_BAND_ROWS * GRID_W)
        dc = np.broadcast_to(dc, shape).reshape(NA_Q_ROWS * GRID_W, NA_BAND_ROWS * GRID_W)
        valid = np.broadcast_to(valid, shape).reshape(NA_Q_ROWS * GRID_W, NA_BAND_ROWS * GRID_W)
        out.append(jnp.where(valid[None], rpb[:, dr, dc], NEG_INF))
    return jnp.stack(out)


def _na_kernel(q_ref, k_ref, v_ref, kc_ref, vc_ref, bias_ref, o_ref):
    i = pl.program_id(2)
    nq = NA_Q_ROWS * GRID_W
    nb = NA_BAND_ROWS * GRID_W
    t = k_ref.shape[1]
    start = pl.multiple_of(jnp.clip(nq * i - (NA_WIN_H // 2) * GRID_W, 0, t - nb), nq)
    kb = k_ref[0, pl.ds(start, nb), :]
    vb = v_ref[0, pl.ds(start, nb), :]
    kc = kc_ref[0]
    vc = vc_ref[0]
    qb = q_ref[0]
    lane = lax.broadcasted_iota(jnp.int32, (nq, LANES), 1)
    outs = []
    for j in range(2):
        q = jnp.where((lane >= 64) == (j == 1), qb, jnp.zeros_like(qb))
        s_band = _dot_nt(q, kb) + bias_ref[0, j]
        s_ctx = _dot_nt(q, kc)
        m = jnp.maximum(jnp.max(s_band, axis=-1, keepdims=True), jnp.max(s_ctx, axis=-1, keepdims=True))
        p_band = jnp.exp(s_band - m)
        p_ctx = jnp.exp(s_ctx - m)
        l = jnp.sum(p_band, axis=-1, keepdims=True) + jnp.sum(p_ctx, axis=-1, keepdims=True)
        acc = _dot(p_band.astype(BF16), vb) + _dot(p_ctx.astype(BF16), vc)
        outs.append(acc / l)
    o_ref[0] = jnp.where(lane < 64, outs[0], outs[1]).astype(o_ref.dtype)


def _neighbourhood_attention(q, k, v, kc, vc, bias):
    b, t, w = q.shape
    pairs = w // LANES
    nq = NA_Q_ROWS * GRID_W
    nb = NA_BAND_ROWS * GRID_W
    nblk = t // nq
    c = kc.shape[1]
    cls = lambda i: jnp.minimum(i, 1) + jnp.maximum(i - (nblk - 2), 0)
    return pl.pallas_call(
        _na_kernel,
        out_shape=jax.ShapeDtypeStruct((b, t, w), BF16),
        grid=(b, pairs, nblk),
        in_specs=[pl.BlockSpec((1, nq, LANES), lambda bi, g, i: (bi, i, g)),
                  pl.BlockSpec((1, t, LANES), lambda bi, g, i: (bi, 0, g)),
                  pl.BlockSpec((1, t, LANES), lambda bi, g, i: (bi, 0, g)),
                  pl.BlockSpec((1, c, LANES), lambda bi, g, i: (bi, 0, g)),
                  pl.BlockSpec((1, c, LANES), lambda bi, g, i: (bi, 0, g)),
                  pl.BlockSpec((1, 2, nq, nb), lambda bi, g, i: (cls(i), g, 0, 0))],
        out_specs=pl.BlockSpec((1, nq, LANES), lambda bi, g, i: (bi, i, g)),
        compiler_params=_params(32 << 20, 3),
        name="neighbourhood_attention",
    )(q, k, v, kc, vc, bias)


def _out_ffn_kernel(*refs, n_attn, ff_chunk, final_norm):
    x_ref, mod_ref = refs[0], refs[1]
    a_refs = refs[2:2 + n_attn]
    w_refs = refs[2 + n_attn:2 + 2 * n_attn]
    gffn_ref, w_in_ref, w_out_ref, gfin_ref, o_ref = refs[2 + 2 * n_attn:]
    mod = mod_ref[0]
    mix = _dot(a_refs[0][...], w_refs[0][...])
    for a_ref, w_ref in zip(a_refs[1:], w_refs[1:]):
        mix = mix + _dot(a_ref[...], w_ref[...])
    x1 = x_ref[...] + mod[2:3] * mix
    h = (_rms(x1, gffn_ref[...]) * (1.0 + mod[4:5]) + mod[3:4]).astype(BF16)
    d_ff = w_out_ref.shape[0]
    acc = None
    for c in range(d_ff // ff_chunk):
        lo = c * ff_chunk
        gate = _dot(h, w_in_ref[:, lo:lo + ff_chunk])
        up = _dot(h, w_in_ref[:, d_ff + lo:d_ff + lo + ff_chunk])
        act = (gate * jax.nn.sigmoid(gate) * up).astype(BF16)
        part = _dot(act, w_out_ref[lo:lo + ff_chunk, :])
        acc = part if acc is None else acc + part
    x2 = x1 + mod[5:6] * acc
    if final_norm:
        x2 = _rms(x2, gfin_ref[...])
    o_ref[...] = x2


def _out_ffn(x, mod, tokens_per_group, attn, w_outs, gffn, w_ffn_in, w_ffn_out, gfin, final_norm):
    n = x.shape[0]
    tm = TOKEN_TILE
    tiles_per_group = tokens_per_group // tm
    row = lambda i: (i, 0)
    in_specs = [pl.BlockSpec((tm, D_MODEL), row),
                pl.BlockSpec((1, 6, D_MODEL), lambda i: (i // tiles_per_group, 0, 0))]
    in_specs += [pl.BlockSpec((tm, a.shape[1]), row) for a in attn]
    in_specs += [_const_spec(w.shape) for w in w_outs]
    in_specs += [_const_spec(gffn.shape), _const_spec(w_ffn_in.shape), _const_spec(w_ffn_out.shape),
                 _const_spec(gfin.shape)]
    return pl.pallas_call(
        functools.partial(_out_ffn_kernel, n_attn=len(attn), ff_chunk=256, final_norm=final_norm),
        out_shape=jax.ShapeDtypeStruct((n, D_MODEL), F32),
        grid=(n // tm,),
        in_specs=in_specs,
        out_specs=pl.BlockSpec((tm, D_MODEL), row),
        compiler_params=_params(56 << 20, 1),
        name="out_ffn",
    )(x, mod, *attn, *w_outs, gffn, w_ffn_in, w_ffn_out, gfin)


def _rope_tables(n_tokens, rot_dim):
    t = jnp.arange(n_tokens)
    row = (t // GRID_W).astype(F32)
    col = (t % GRID_W).astype(F32)
    axis_dim = rot_dim // 2
    inv_freq = ROPE_THETA ** (-jnp.arange(0, axis_dim, 2, dtype=F32) / axis_dim)
    ang = jnp.concatenate([row[:, None] * inv_freq, col[:, None] * inv_freq], axis=-1)
    return jnp.cos(ang), jnp.sin(ang)


def _mla_rope_lanes(n_tokens):
    cos, sin = _rope_tables(n_tokens, MLA_ROPE_DIM)
    one = jnp.ones((n_tokens, MLA_NOPE_DIM), F32)
    zero = jnp.zeros((n_tokens, MLA_NOPE_DIM), F32)
    pad1 = jnp.ones((n_tokens, LANES - MLA_QK_DIM), F32)
    pad0 = jnp.zeros((n_tokens, LANES - MLA_QK_DIM), F32)
    return (jnp.concatenate([one, cos, cos, pad1], axis=-1),
            jnp.concatenate([zero, -sin, sin, pad0], axis=-1))


def _gqa_rope_lanes(n_tokens):
    cos, sin = _rope_tables(n_tokens, GQA_HEAD_DIM)
    return jnp.concatenate([cos, cos], axis=-1), jnp.concatenate([-sin, sin], axis=-1)


def _swap_halves(w):
    half = w.shape[-1] // 2
    return jnp.concatenate([w[..., half:], w[..., :half]], axis=-1)


def _even_weights(w_in, w_uq, w_ukv):
    d = w_in.shape[0]
    i0 = MLA_Q_LORA
    i1 = i0 + MLA_KV_LORA
    i2 = i1 + MLA_ROPE_DIM
    w_kr = w_in[:, i1:i2]
    zl = jnp.zeros((d, MLA_NOPE_DIM), F32)
    zr = jnp.zeros((d, LANES - MLA_QK_DIM), F32)
    w_in_k = jnp.concatenate([w_in[:, :i1], w_in[:, i2:], zl, w_kr, zr, zl, _swap_halves(w_kr), zr],
                             axis=-1).astype(BF16)
    r = w_uq.shape[0]
    uq = w_uq.reshape(r, MLA_HEADS, MLA_QK_DIM)
    zpad = jnp.zeros((r, MLA_HEADS, LANES - MLA_QK_DIM), F32)
    znope = jnp.zeros((r, MLA_HEADS, MLA_NOPE_DIM), F32)
    q_plain = jnp.concatenate([uq, zpad], axis=-1).reshape(r, MLA_HEADS * LANES)
    q_swap = jnp.concatenate([znope, _swap_halves(uq[..., MLA_NOPE_DIM:]), zpad], axis=-1)
    w_uq_k = jnp.concatenate([q_plain, q_swap.reshape(r, MLA_HEADS * LANES)], axis=-1).astype(BF16)
    r = w_ukv.shape[0]
    ukv = w_ukv.reshape(r, MLA_HEADS, MLA_NOPE_DIM + MLA_V_DIM)
    k_pad = jnp.concatenate([ukv[..., :MLA_NOPE_DIM], jnp.zeros((r, MLA_HEADS, LANES - MLA_NOPE_DIM), F32)],
                            axis=-1).reshape(r, MLA_HEADS * LANES)
    v_cat = ukv[..., MLA_NOPE_DIM:].reshape(r, MLA_HEADS * MLA_V_DIM)
    w_ukv_k = jnp.concatenate([k_pad, v_cat], axis=-1).astype(BF16)
    return w_in_k, w_uq_k, w_ukv_k


def kernel(x_prompt, x_sample, cache_mla_ckv, cache_mla_krope, cache_na_k, cache_na_v, cache_gqa_k, cache_gqa_v, c, c_ctx, w_mod, b_mod, norm_mix, norm_ffn, norm_final, w_in_a, mla_q_norm, mla_w_uq, mla_kv_norm, mla_w_ukv, na_rpb, w_out_a, w_in_c, gqa_q_norm, gqa_k_norm, w_out_c, w_ffn_in, w_ffn_out):
    batch, seq, d = x_prompt.shape
    dec_batch, dec_seq, _ = x_sample.shape
    depth = w_mod.shape[0]
    past = cache_mla_ckv.shape[2]
    n_ctx = batch * seq
    n_lat = dec_batch * dec_seq

    cond = jnp.concatenate([c_ctx[None], c, jnp.zeros((8 - 1 - dec_batch, d), F32)], axis=0)
    mod = _modulation(cond, w_mod, b_mod).reshape(depth, 8, 6, d)

    xp = x_prompt.reshape(n_ctx, d)
    xs = x_sample.reshape(n_lat, d)
    cos_m, sin_m = _mla_rope_lanes(dec_seq)
    cos_g, sin_g = _gqa_rope_lanes(dec_seq)
    ident_cos = jnp.ones((TOKEN_TILE, LANES), F32)
    ident_sin = jnp.zeros((TOKEN_TILE, LANES), F32)
    gfin = norm_final.reshape(1, d)
    states = {k: [] for k in ("ckv", "krope", "nk", "nv", "gk", "gv")}

    for l in range(depth):
        mod_p = mod[l, 0:1]
        mod_s = mod[l, 1:1 + dec_batch]
        gmix = norm_mix[l].reshape(1, d)
        gffn = norm_ffn[l].reshape(1, d)
        if l % 2 == 0:
            e = l // 2
            w_in_k, w_uq_k, w_ukv_k = _even_weights(w_in_a[e], mla_w_uq[e], mla_w_ukv[e])
            qn = mla_q_norm[e].reshape(1, -1)
            kvn = mla_kv_norm[e].reshape(1, -1)
            (qp, kp, vp, nqp, nkp, nvp, s_ckv, s_kr, s_nk, s_nv) = _even_in(
                xp, mod_p, n_ctx, gmix, w_in_k, qn, kvn, w_uq_k, w_ukv_k, ident_cos, ident_sin, True)
            states["ckv"].append(s_ckv.reshape(batch, seq, MLA_KV_LORA))
            states["krope"].append(s_kr[:, MLA_NOPE_DIM:MLA_QK_DIM].reshape(batch, seq, MLA_ROPE_DIM))
            states["nk"].append(s_nk.reshape(batch, seq, NA_HEADS, NA_HEAD_DIM))
            states["nv"].append(s_nv.reshape(batch, seq, NA_HEADS, NA_HEAD_DIM))
            qs, ks, vs, nqs, nks, nvs = _even_in(
                xs, mod_s, dec_seq, gmix, w_in_k, qn, kvn, w_uq_k, w_ukv_k, cos_m, sin_m, False)
            kr_cache = jnp.pad(cache_mla_krope[:, e],
                               ((0, 0), (0, 0), (MLA_NOPE_DIM, LANES - MLA_QK_DIM)))
            kc, vc = _cache_expand(cache_mla_ckv[:, e], kr_cache, w_ukv_k)

            r3 = lambda a, b_: a.reshape(b_, a.shape[0] // b_, a.shape[1])
            mla_kw = dict(groups=MLA_HEADS // 2, heads=2, k_stride=LANES, q_half_mask=False, pair_out=True)
            na_kw = dict(groups=NA_HEADS // 2, heads=2, k_stride=0, q_half_mask=True, pair_out=True)
            a_mla_p = _attention(r3(qp, batch), [(r3(kp, batch), r3(vp, batch))], name="mla_ctx", **mla_kw)
            a_na_p = _attention(r3(nqp, batch), [(r3(nkp, batch), r3(nvp, batch))], name="na_ctx", **na_kw)
            a_mla_s = _attention(r3(qs, dec_batch), [(r3(ks, dec_batch), r3(vs, dec_batch)), (kc, vc)],
                                 name="mla_lat", **mla_kw)
            bias = _na_bias_tables(na_rpb[e])
            a_na_s = _neighbourhood_attention(
                r3(nqs, dec_batch), r3(nks, dec_batch), r3(nvs, dec_batch),
                cache_na_k[:, e].reshape(dec_batch, past, NA_W).astype(BF16),
                cache_na_v[:, e].reshape(dec_batch, past, NA_W).astype(BF16), bias)
            attn_p = [a_mla_p.reshape(n_ctx, -1), a_na_p.reshape(n_ctx, -1)]
            attn_s = [a_mla_s.reshape(n_lat, -1), a_na_s.reshape(n_lat, -1)]
            wo = w_out_a[e].astype(BF16)
            half = MLA_HEADS * MLA_V_DIM
            w_outs = [wo[:half], wo[half:]]
        else:
            o = l // 2
            w_in_k = w_in_c[o].astype(BF16)
            qn = gqa_q_norm[o].reshape(1, -1)
            kn = gqa_k_norm[o].reshape(1, -1)
            qp, kp, vp, s_gk, s_gv = _odd_in(xp, mod_p, n_ctx, gmix, w_in_k, qn, kn, ident_cos, ident_sin, True)
            states["gk"].append(s_gk.reshape(batch, seq, GQA_KV_HEADS, GQA_HEAD_DIM))
            states["gv"].append(s_gv.reshape(batch, seq, GQA_KV_HEADS, GQA_HEAD_DIM))
            qs, ks, vs = _odd_in(xs, mod_s, dec_seq, gmix, w_in_k, qn, kn, cos_g, sin_g, False)
            r3 = lambda a, b_: a.reshape(b_, a.shape[0] // b_, a.shape[1])
            gqa_kw = dict(groups=GQA_KV_HEADS, heads=GQA_GROUP, k_stride=0, q_half_mask=False, pair_out=False)
            a_p = _attention(r3(qp, batch), [(r3(kp, batch), r3(vp, batch))], name="gqa_ctx", **gqa_kw)
            kcache = cache_gqa_k[:, o].reshape(dec_batch, past, -1).astype(BF16)
            vcache = cache_gqa_v[:, o].reshape(dec_batch, past, -1).astype(BF16)
            a_s = _attention(r3(qs, dec_batch), [(r3(ks, dec_batch), r3(vs, dec_batch)), (kcache, vcache)],
                             name="gqa_lat", **gqa_kw)
            attn_p = [a_p.reshape(n_ctx, -1)]
            attn_s = [a_s.reshape(n_lat, -1)]
            w_outs = [w_out_c[o].astype(BF16)]
        last = l == depth - 1
        wfi = w_ffn_in[l].astype(BF16)
        wfo = w_ffn_out[l].astype(BF16)
        xp = _out_ffn(xp, mod_p, n_ctx, attn_p, w_outs, gffn, wfi, wfo, gfin, last)
        xs = _out_ffn(xs, mod_s, dec_seq, attn_s, w_outs, gffn, wfi, wfo, gfin, last)

    y_prompt = xp.reshape(batch, seq, d)
    y_sample = xs.reshape(dec_batch, dec_seq, d)
    return (y_prompt, y_sample,
            jnp.stack(states["ckv"], axis=1), jnp.stack(states["krope"], axis=1),
            jnp.stack(states["nk"], axis=1), jnp.stack(states["nv"], axis=1),
            jnp.stack(states["gk"], axis=1), jnp.stack(states["gv"], axis=1))
```

```python
import functools
import math

import numpy as np
import jax
import jax.numpy as jnp
from jax import lax
from jax.experimental import pallas as pl
from jax.experimental.pallas import tpu as pltpu

LANES = 128
V7X_VMEM_BYTES = 64 * 1024 * 1024

D_MODEL = 1024
GRID_W = 64
ROPE_THETA = 10000.0
RMS_EPS = 1e-6
NEG_INF = -1e30
MLA_HEADS = 8
MLA_Q_LORA = 256
MLA_KV_LORA = 256
MLA_NOPE_DIM = 64
MLA_ROPE_DIM = 32
MLA_V_DIM = 64
MLA_QK_DIM = MLA_NOPE_DIM + MLA_ROPE_DIM
MLA_SCALE = MLA_QK_DIM ** -0.5
NA_HEADS = 8
NA_HEAD_DIM = 64
NA_WIN_H = 8
NA_WIN_W = 16
NA_SCALE = NA_HEAD_DIM ** -0.5
NA_W = NA_HEADS * NA_HEAD_DIM
GQA_HEADS = 8
GQA_KV_HEADS = 2
GQA_HEAD_DIM = 128
GQA_SCALE = GQA_HEAD_DIM ** -0.5
GQA_GROUP = GQA_HEADS // GQA_KV_HEADS

TOKEN_TILE = 512
ATTN_Q_TILE = 256
ATTN_K_CHUNK = 512
NA_Q_ROWS = 4
NA_BAND_ROWS = 12

BF16 = jnp.bfloat16
F32 = jnp.float32


def _vmem_limit(nbytes):
    return int(min(V7X_VMEM_BYTES - (4 << 20), max(nbytes, 16 << 20)))


def _params(nbytes, ndims):
    return pltpu.CompilerParams(dimension_semantics=("arbitrary",) * ndims,
                                vmem_limit_bytes=_vmem_limit(nbytes))


def _rms(x, gain):
    return x * lax.rsqrt(jnp.mean(x * x, axis=-1, keepdims=True) + RMS_EPS) * gain


def _dot(a, b):
    return jnp.dot(a, b, preferred_element_type=F32)


def _dot_nt(a, b):
    return lax.dot_general(a, b, (((1,), (1,)), ((), ())), preferred_element_type=F32)


def _const_spec(shape):
    nd = len(shape)
    return pl.BlockSpec(shape, lambda *_: (0,) * nd, pipeline_mode=pl.Buffered(1))


def _mod_kernel(cond_ref, w_ref, b_ref, o_ref):
    c = cond_ref[...]
    s = (c * jax.nn.sigmoid(c)).astype(BF16)
    o_ref[0] = _dot(s, w_ref[0].astype(BF16)) + b_ref[0]


def _modulation(cond, w_mod, b_mod):
    depth, d, n = w_mod.shape
    rows = cond.shape[0]
    bn = 1024
    return pl.pallas_call(
        _mod_kernel,
        out_shape=jax.ShapeDtypeStruct((depth, rows, n), F32),
        grid=(depth, n // bn),
        in_specs=[pl.BlockSpec((rows, d), lambda l, j: (0, 0)),
                  pl.BlockSpec((1, d, bn), lambda l, j: (l, 0, j)),
                  pl.BlockSpec((1, 1, bn), lambda l, j: (l, 0, j))],
        out_specs=pl.BlockSpec((1, rows, bn), lambda l, j: (l, 0, j)),
        compiler_params=_params(3 * d * bn * 4, 2),
        name="ada_modulation",
    )(cond, w_mod, b_mod.reshape(depth, 1, n))


def _even_in_kernel(x_ref, mod_ref, gmix_ref, w_in_ref, qn_ref, kvn_ref, w_uq_ref, w_ukv_ref,
                    cos_ref, sin_ref, *out_refs, with_state):
    q_ref, k_ref, v_ref, nq_ref, nk_ref, nv_ref = out_refs[:6]
    x = x_ref[...]
    mod = mod_ref[0]
    h = _rms(x, gmix_ref[...]) * (1.0 + mod[1:2]) + mod[0:1]
    p = _dot(h.astype(BF16), w_in_ref[...])
    cq = p[:, 0:256]
    ckv = _rms(p[:, 256:512], kvn_ref[...])
    nq = p[:, 512:1024]
    nk = p[:, 1024:1536]
    nv = p[:, 1536:2048]
    kr = p[:, 2048:2176]
    kr_sw = p[:, 2176:2304]
    cos = cos_ref[...]
    sin = sin_ref[...]
    qq = _dot(_rms(cq, qn_ref[...]).astype(BF16), w_uq_ref[...])
    kv = _dot(ckv.astype(BF16), w_ukv_ref[...])
    kr_rot = kr * cos + kr_sw * sin
    for hd in range(MLA_HEADS):
        lo = hd * LANES
        qh = qq[:, lo:lo + LANES] * cos + qq[:, 1024 + lo:1024 + lo + LANES] * sin
        q_ref[:, lo:lo + LANES] = (qh * MLA_SCALE).astype(BF16)
        k_ref[:, lo:lo + LANES] = (kv[:, lo:lo + LANES] + kr_rot).astype(BF16)
    v_ref[...] = kv[:, 1024:1536].astype(BF16)
    nq_ref[...] = (nq * NA_SCALE).astype(BF16)
    nk_ref[...] = nk.astype(BF16)
    nv_ref[...] = nv.astype(BF16)
    if with_state:
        s_ckv_ref, s_kr_ref, s_nk_ref, s_nv_ref = out_refs[6:]
        s_ckv_ref[...] = ckv
        s_kr_ref[...] = kr
        s_nk_ref[...] = nk
        s_nv_ref[...] = nv


def _even_in(x, mod, tokens_per_group, gmix, w_in, qn, kvn, w_uq, w_ukv, cos, sin, with_state):
    n = x.shape[0]
    tm = TOKEN_TILE
    tiles_per_group = tokens_per_group // tm
    rope_tiles = cos.shape[0] // tm
    row = lambda i: (i, 0)
    outs = [jax.ShapeDtypeStruct((n, 1024), BF16), jax.ShapeDtypeStruct((n, 1024), BF16),
            jax.ShapeDtypeStruct((n, 512), BF16), jax.ShapeDtypeStruct((n, 512), BF16),
            jax.ShapeDtypeStruct((n, 512), BF16), jax.ShapeDtypeStruct((n, 512), BF16)]
    if with_state:
        outs += [jax.ShapeDtypeStruct((n, 256), F32), jax.ShapeDtypeStruct((n, 128), F32),
                 jax.ShapeDtypeStruct((n, 512), F32), jax.ShapeDtypeStruct((n, 512), F32)]
    return pl.pallas_call(
        functools.partial(_even_in_kernel, with_state=with_state),
        out_shape=outs,
        grid=(n // tm,),
        in_specs=[pl.BlockSpec((tm, D_MODEL), row),
                  pl.BlockSpec((1, 6, D_MODEL), lambda i: (i // tiles_per_group, 0, 0)),
                  _const_spec(gmix.shape), _const_spec(w_in.shape), _const_spec(qn.shape),
                  _const_spec(kvn.shape), _const_spec(w_uq.shape), _const_spec(w_ukv.shape),
                  pl.BlockSpec((tm, LANES), lambda i: (i % rope_tiles, 0)),
                  pl.BlockSpec((tm, LANES), lambda i: (i % rope_tiles, 0))],
        out_specs=[pl.BlockSpec((tm, o.shape[1]), row) for o in outs],
        compiler_params=_params(40 << 20, 1),
        name="even_in",
    )(x, mod, gmix, w_in, qn, kvn, w_uq, w_ukv, cos, sin)


def _cache_expand_kernel(ckv_ref, kr_ref, w_ukv_ref, k_ref, v_ref):
    kv = _dot(ckv_ref[0].astype(BF16), w_ukv_ref[...])
    kr = kr_ref[0]
    for hd in range(MLA_HEADS):
        lo = hd * LANES
        k_ref[0, :, lo:lo + LANES] = (kv[:, lo:lo + LANES] + kr).astype(BF16)
    v_ref[0] = kv[:, 1024:1536].astype(BF16)


def _cache_expand(ckv, kr128, w_ukv):
    b, s, _ = ckv.shape
    return pl.pallas_call(
        _cache_expand_kernel,
        out_shape=[jax.ShapeDtypeStruct((b, s, 1024), BF16), jax.ShapeDtypeStruct((b, s, 512), BF16)],
        grid=(b,),
        in_specs=[pl.BlockSpec((1, s, MLA_KV_LORA), lambda i: (i, 0, 0)),
                  pl.BlockSpec((1, s, LANES), lambda i: (i, 0, 0)),
                  _const_spec(w_ukv.shape)],
        out_specs=[pl.BlockSpec((1, s, 1024), lambda i: (i, 0, 0)),
                   pl.BlockSpec((1, s, 512), lambda i: (i, 0, 0))],
        compiler_params=_params(16 << 20, 1),
        name="mla_cache_expand",
    )(ckv, kr128, w_ukv)


def _odd_in_kernel(x_ref, mod_ref, gmix_ref, w_in_ref, qn_ref, kn_ref, cos_ref, sin_ref, *out_refs,
                   with_state):
    q_ref, k_ref, v_ref = out_refs[:3]
    x = x_ref[...]
    mod = mod_ref[0]
    h = _rms(x, gmix_ref[...]) * (1.0 + mod[1:2]) + mod[0:1]
    p = _dot(h.astype(BF16), w_in_ref[...])
    cos = cos_ref[...]
    sin = sin_ref[...]
    half = GQA_HEAD_DIM // 2

    def rope(t):
        return t * cos + pltpu.roll(t, half, 1) * sin

    for hd in range(GQA_HEADS):
        lo = hd * LANES
        qh = rope(_rms(p[:, lo:lo + LANES], qn_ref[...]))
        q_ref[:, lo:lo + LANES] = (qh * GQA_SCALE).astype(BF16)
    for hd in range(GQA_KV_HEADS):
        lo = hd * LANES
        kh = _rms(p[:, 1024 + lo:1024 + lo + LANES], kn_ref[...])
        vh = p[:, 1280 + lo:1280 + lo + LANES]
        k_ref[:, lo:lo + LANES] = rope(kh).astype(BF16)
        v_ref[:, lo:lo + LANES] = vh.astype(BF16)
        if with_state:
            out_refs[3][:, lo:lo + LANES] = kh
            out_refs[4][:, lo:lo + LANES] = vh


def _odd_in(x, mod, tokens_per_group, gmix, w_in, qn, kn, cos, sin, with_state):
    n = x.shape[0]
    tm = TOKEN_TILE
    tiles_per_group = tokens_per_group // tm
    rope_tiles = cos.shape[0] // tm
    row = lambda i: (i, 0)
    outs = [jax.ShapeDtypeStruct((n, 1024), BF16), jax.ShapeDtypeStruct((n, 256), BF16),
            jax.ShapeDtypeStruct((n, 256), BF16)]
    if with_state:
        outs += [jax.ShapeDtypeStruct((n, 256), F32), jax.ShapeDtypeStruct((n, 256), F32)]
    return pl.pallas_call(
        functools.partial(_odd_in_kernel, with_state=with_state),
        out_shape=outs,
        grid=(n // tm,),
        in_specs=[pl.BlockSpec((tm, D_MODEL), row),
                  pl.BlockSpec((1, 6, D_MODEL), lambda i: (i // tiles_per_group, 0, 0)),
                  _const_spec(gmix.shape), _const_spec(w_in.shape), _const_spec(qn.shape),
                  _const_spec(kn.shape),
                  pl.BlockSpec((tm, LANES), lambda i: (i % rope_tiles, 0)),
                  pl.BlockSpec((tm, LANES), lambda i: (i % rope_tiles, 0))],
        out_specs=[pl.BlockSpec((tm, o.shape[1]), row) for o in outs],
        compiler_params=_params(32 << 20, 1),
        name="odd_in",
    )(x, mod, gmix, w_in, qn, kn, cos, sin)


def _attn_kernel(*refs, n_src, heads, q_tile, src_len, k_stride, q_half_mask, pair_out):
    q_ref = refs[0]
    kv_refs = refs[1:1 + 2 * n_src]
    o_ref = refs[1 + 2 * n_src]
    s_ref = refs[2 + 2 * n_src]
    chunks = []
    for src in range(n_src):
        ck = min(ATTN_K_CHUNK, src_len[src])
        for c in range(src_len[src] // ck):
            chunks.append((src, c * ck, ck))
    lane = lax.broadcasted_iota(jnp.int32, (q_tile, LANES), 1)
    outs = []
    for j in range(heads):
        if q_half_mask:
            qb = q_ref[0]
            q = jnp.where((lane >= 64) == (j == 1), qb, jnp.zeros_like(qb))
        else:
            q = q_ref[0, :, j * LANES:(j + 1) * LANES]
        m_part = jnp.full((q_tile, LANES), -jnp.inf, F32)
        off = 0
        for src, k0, ck in chunks:
            k_ref = kv_refs[2 * src]
            kc = k_ref[0, k0:k0 + ck, j * k_stride:j * k_stride + LANES]
            s = _dot_nt(q, kc)
            s_ref[:, off:off + ck] = s
            for t in range(ck // LANES):
                m_part = jnp.maximum(m_part, s[:, t * LANES:(t + 1) * LANES])
            off += ck
        m = jnp.max(m_part, axis=-1, keepdims=True)
        l_part = jnp.zeros((q_tile, LANES), F32)
        acc = jnp.zeros((q_tile, LANES), F32)
        off = 0
        for src, k0, ck in chunks:
            v_ref = kv_refs[2 * src + 1]
            p = jnp.exp(s_ref[:, off:off + ck] - m)
            for t in range(ck // LANES):
                l_part = l_part + p[:, t * LANES:(t + 1) * LANES]
            acc = acc + _dot(p.astype(BF16), v_ref[0, k0:k0 + ck, :])
            off += ck
        outs.append(acc / jnp.sum(l_part, axis=-1, keepdims=True))
    if pair_out:
        o_ref[0] = jnp.where(lane < 64, outs[0], outs[1]).astype(o_ref.dtype)
    else:
        for j in range(heads):
            o_ref[0, :, j * LANES:(j + 1) * LANES] = outs[j].astype(o_ref.dtype)


def _attention(q, sources, *, groups, heads, k_stride, q_half_mask, pair_out, name):
    b, t, _ = q.shape
    q_tile = min(ATTN_Q_TILE, t)
    q_block = LANES if q_half_mask else heads * LANES
    k_block = LANES if k_stride == 0 else heads * LANES
    out_block = LANES if pair_out else heads * LANES
    src_len = tuple(k.shape[1] for k, _ in sources)
    in_specs = [pl.BlockSpec((1, q_tile, q_block), lambda bi, g, qi: (bi, qi, g))]
    args = [q]
    for k, v in sources:
        s = k.shape[1]
        in_specs.append(pl.BlockSpec((1, s, k_block), lambda bi, g, qi: (bi, 0, g)))
        in_specs.append(pl.BlockSpec((1, s, LANES), lambda bi, g, qi: (bi, 0, g)))
        args += [k, v]
    total = sum(src_len)
    return pl.pallas_call(
        functools.partial(_attn_kernel, n_src=len(sources), heads=heads, q_tile=q_tile,
                          src_len=src_len, k_stride=k_stride, q_half_mask=q_half_mask,
                          pair_out=pair_out),
        out_shape=jax.ShapeDtypeStruct((b, t, groups * out_block), BF16),
        grid=(b, groups, t // q_tile),
        in_specs=in_specs,
        out_specs=pl.BlockSpec((1, q_tile, out_block), lambda bi, g, qi: (bi, qi, g)),
        scratch_shapes=[pltpu.VMEM((q_tile, total), F32)],
        compiler_params=_params(48 << 20, 3),
        name=name,
    )(*args)


def _na_bias_tables(rpb):
    n_rows = GRID_W
    h, n_dr, n_dc = rpb.shape
    edge = n_dc - 1 - (NA_WIN_W - 1)
    w = jnp.concatenate([rpb[..., NA_WIN_W - 1:],
                         jnp.broadcast_to(rpb[..., n_dc - 1:], (h, n_dr, GRID_W - 1 - edge)),
                         jnp.broadcast_to(rpb[..., :1], (h, n_dr, GRID_W - (NA_WIN_W - 1) + 1)),
                         rpb[..., 1:NA_WIN_W - 1]], axis=-1)
    toe = jnp.tile(w, (1, 1, GRID_W))[..., :GRID_W * (2 * GRID_W - 1)]
    toe = toe.reshape(h, n_dr, GRID_W, 2 * GRID_W - 1)[..., :GRID_W]
    cols = np.arange(GRID_W)
    cs = np.clip(cols - NA_WIN_W // 2, 0, GRID_W - NA_WIN_W)
    col_ok = (cols[None, :] >= cs[:, None]) & (cols[None, :] < cs[:, None] + NA_WIN_W)
    toe = jnp.where(col_ok, toe, NEG_INF)
    masked = jnp.full((h, GRID_W, GRID_W), NEG_INF, F32)
    out = []
    for blk in (0, 1, n_rows // NA_Q_ROWS - 1):
        b0 = int(np.clip(NA_Q_ROWS * blk - NA_WIN_H // 2, 0, n_rows - NA_BAND_ROWS))
        q_rows = []
        for qr in range(NA_Q_ROWS):
            r = NA_Q_ROWS * blk + qr
            rs = int(np.clip(r - NA_WIN_H // 2, 0, n_rows - NA_WIN_H))
            parts = []
            for j in range(NA_BAND_ROWS):
                kr = b0 + j
                parts.append(toe[:, kr - r + NA_WIN_H - 1] if rs <= kr < rs + NA_WIN_H else masked)
            q_rows.append(jnp.concatenate(parts, axis=-1))
        out.append(jnp.concatenate(q_rows, axis=-2))
    return jnp.stack(out)


def _na_kernel(q_ref, k_ref, v_ref, kc_ref, vc_ref, bias_ref, o_ref):
    i = pl.program_id(2)
    nq = NA_Q_ROWS * GRID_W
    nb = NA_BAND_ROWS * GRID_W
    t = k_ref.shape[1]
    start = pl.multiple_of(jnp.clip(nq * i - (NA_WIN_H // 2) * GRID_W, 0, t - nb), nq)
    kb = k_ref[0, pl.ds(start, nb), :]
    vb = v_ref[0, pl.ds(start, nb), :]
    kc = kc_ref[0]
    vc = vc_ref[0]
    qb = q_ref[0]
    lane = lax.broadcasted_iota(jnp.int32, (nq, LANES), 1)
    outs = []
    for j in range(2):
        q = jnp.where((lane >= 64) == (j == 1), qb, jnp.zeros_like(qb))
        s_band = _dot_nt(q, kb) + bias_ref[0, j]
        s_ctx = _dot_nt(q, kc)
        m = jnp.maximum(jnp.max(s_band, axis=-1, keepdims=True), jnp.max(s_ctx, axis=-1, keepdims=True))
        p_band = jnp.exp(s_band - m)
        p_ctx = jnp.exp(s_ctx - m)
        l = jnp.sum(p_band, axis=-1, keepdims=True) + jnp.sum(p_ctx, axis=-1, keepdims=True)
        acc = _dot(p_band.astype(BF16), vb) + _dot(p_ctx.astype(BF16), vc)
        outs.append(acc / l)
    o_ref[0] = jnp.where(lane < 64, outs[0], outs[1]).astype(o_ref.dtype)


def _neighbourhood_attention(q, k, v, kc, vc, bias):
    b, t, w = q.shape
    pairs = w // LANES
    nq = NA_Q_ROWS * GRID_W
    nb = NA_BAND_ROWS * GRID_W
    nblk = t // nq
    c = kc.shape[1]
    cls = lambda i: jnp.minimum(i, 1) + jnp.maximum(i - (nblk - 2), 0)
    return pl.pallas_call(
        _na_kernel,
        out_shape=jax.ShapeDtypeStruct((b, t, w), BF16),
        grid=(b, pairs, nblk),
        in_specs=[pl.BlockSpec((1, nq, LANES), lambda bi, g, i: (bi, i, g)),
                  pl.BlockSpec((1, t, LANES), lambda bi, g, i: (bi, 0, g)),
                  pl.BlockSpec((1, t, LANES), lambda bi, g, i: (bi, 0, g)),
                  pl.BlockSpec((1, c, LANES), lambda bi, g, i: (bi, 0, g)),
                  pl.BlockSpec((1, c, LANES), lambda bi, g, i: (bi, 0, g)),
                  pl.BlockSpec((1, 2, nq, nb), lambda bi, g, i: (cls(i), g, 0, 0))],
        out_specs=pl.BlockSpec((1, nq, LANES), lambda bi, g, i: (bi, i, g)),
        compiler_params=_params(32 << 20, 3),
        name="neighbourhood_attention",
    )(q, k, v, kc, vc, bias)


def _out_ffn_kernel(*refs, n_attn, ff_chunk, final_norm):
    x_ref, mod_ref = refs[0], refs[1]
    a_refs = refs[2:2 + n_attn]
    w_refs = refs[2 + n_attn:2 + 2 * n_attn]
    gffn_ref, w_in_ref, w_out_ref, gfin_ref, o_ref = refs[2 + 2 * n_attn:]
    mod = mod_ref[0]
    mix = _dot(a_refs[0][...], w_refs[0][...])
    for a_ref, w_ref in zip(a_refs[1:], w_refs[1:]):
        mix = mix + _dot(a_ref[...], w_ref[...])
    x1 = x_ref[...] + mod[2:3] * mix
    h = (_rms(x1, gffn_ref[...]) * (1.0 + mod[4:5]) + mod[3:4]).astype(BF16)
    d_ff = w_out_ref.shape[0]
    acc = None
    for c in range(d_ff // ff_chunk):
        lo = c * ff_chunk
        gate = _dot(h, w_in_ref[:, lo:lo + ff_chunk])
        up = _dot(h, w_in_ref[:, d_ff + lo:d_ff + lo + ff_chunk])
        act = (gate * jax.nn.sigmoid(gate) * up).astype(BF16)
        part = _dot(act, w_out_ref[lo:lo + ff_chunk, :])
        acc = part if acc is None else acc + part
    x2 = x1 + mod[5:6] * acc
    if final_norm:
        x2 = _rms(x2, gfin_ref[...])
    o_ref[...] = x2


def _out_ffn(x, mod, tokens_per_group, attn, w_outs, gffn, w_ffn_in, w_ffn_out, gfin, final_norm):
    n = x.shape[0]
    tm = TOKEN_TILE
    tiles_per_group = tokens_per_group // tm
    row = lambda i: (i, 0)
    in_specs = [pl.BlockSpec((tm, D_MODEL), row),
                pl.BlockSpec((1, 6, D_MODEL), lambda i: (i // tiles_per_group, 0, 0))]
    in_specs += [pl.BlockSpec((tm, a.shape[1]), row) for a in attn]
    in_specs += [_const_spec(w.shape) for w in w_outs]
    in_specs += [_const_spec(gffn.shape), _const_spec(w_ffn_in.shape), _const_spec(w_ffn_out.shape),
                 _const_spec(gfin.shape)]
    return pl.pallas_call(
        functools.partial(_out_ffn_kernel, n_attn=len(attn), ff_chunk=256, final_norm=final_norm),
        out_shape=jax.ShapeDtypeStruct((n, D_MODEL), F32),
        grid=(n // tm,),
        in_specs=in_specs,
        out_specs=pl.BlockSpec((tm, D_MODEL), row),
        compiler_params=_params(56 << 20, 1),
        name="out_ffn",
    )(x, mod, *attn, *w_outs, gffn, w_ffn_in, w_ffn_out, gfin)


def _rope_tables(n_tokens, rot_dim):
    t = jnp.arange(n_tokens)
    row = (t // GRID_W).astype(F32)
    col = (t % GRID_W).astype(F32)
    axis_dim = rot_dim // 2
    inv_freq = ROPE_THETA ** (-jnp.arange(0, axis_dim, 2, dtype=F32) / axis_dim)
    ang = jnp.concatenate([row[:, None] * inv_freq, col[:, None] * inv_freq], axis=-1)
    return jnp.cos(ang), jnp.sin(ang)


def _mla_rope_lanes(n_tokens):
    cos, sin = _rope_tables(n_tokens, MLA_ROPE_DIM)
    one = jnp.ones((n_tokens, MLA_NOPE_DIM), F32)
    zero = jnp.zeros((n_tokens, MLA_NOPE_DIM), F32)
    pad1 = jnp.ones((n_tokens, LANES - MLA_QK_DIM), F32)
    pad0 = jnp.zeros((n_tokens, LANES - MLA_QK_DIM), F32)
    return (jnp.concatenate([one, cos, cos, pad1], axis=-1),
            jnp.concatenate([zero, -sin, sin, pad0], axis=-1))


def _gqa_rope_lanes(n_tokens):
    cos, sin = _rope_tables(n_tokens, GQA_HEAD_DIM)
    return jnp.concatenate([cos, cos], axis=-1), jnp.concatenate([-sin, sin], axis=-1)


def _swap_halves(w):
    half = w.shape[-1] // 2
    return jnp.concatenate([w[..., half:], w[..., :half]], axis=-1)


def _even_weights(w_in, w_uq, w_ukv):
    d = w_in.shape[0]
    i0 = MLA_Q_LORA
    i1 = i0 + MLA_KV_LORA
    i2 = i1 + MLA_ROPE_DIM
    w_kr = w_in[:, i1:i2]
    zl = jnp.zeros((d, MLA_NOPE_DIM), F32)
    zr = jnp.zeros((d, LANES - MLA_QK_DIM), F32)
    w_in_k = jnp.concatenate([w_in[:, :i1], w_in[:, i2:], zl, w_kr, zr, zl, _swap_halves(w_kr), zr],
                             axis=-1).astype(BF16)
    r = w_uq.shape[0]
    uq = w_uq.reshape(r, MLA_HEADS, MLA_QK_DIM)
    zpad = jnp.zeros((r, MLA_HEADS, LANES - MLA_QK_DIM), F32)
    znope = jnp.zeros((r, MLA_HEADS, MLA_NOPE_DIM), F32)
    q_plain = jnp.concatenate([uq, zpad], axis=-1).reshape(r, MLA_HEADS * LANES)
    q_swap = jnp.concatenate([znope, _swap_halves(uq[..., MLA_NOPE_DIM:]), zpad], axis=-1)
    w_uq_k = jnp.concatenate([q_plain, q_swap.reshape(r, MLA_HEADS * LANES)], axis=-1).astype(BF16)
    r = w_ukv.shape[0]
    ukv = w_ukv.reshape(r, MLA_HEADS, MLA_NOPE_DIM + MLA_V_DIM)
    k_pad = jnp.concatenate([ukv[..., :MLA_NOPE_DIM], jnp.zeros((r, MLA_HEADS, LANES - MLA_NOPE_DIM), F32)],
                            axis=-1).reshape(r, MLA_HEADS * LANES)
    v_cat = ukv[..., MLA_NOPE_DIM:].reshape(r, MLA_HEADS * MLA_V_DIM)
    w_ukv_k = jnp.concatenate([k_pad, v_cat], axis=-1).astype(BF16)
    return w_in_k, w_uq_k, w_ukv_k


def kernel(x_prompt, x_sample, cache_mla_ckv, cache_mla_krope, cache_na_k, cache_na_v, cache_gqa_k, cache_gqa_v, c, c_ctx, w_mod, b_mod, norm_mix, norm_ffn, norm_final, w_in_a, mla_q_norm, mla_w_uq, mla_kv_norm, mla_w_ukv, na_rpb, w_out_a, w_in_c, gqa_q_norm, gqa_k_norm, w_out_c, w_ffn_in, w_ffn_out):
    batch, seq, d = x_prompt.shape
    dec_batch, dec_seq, _ = x_sample.shape
    depth = w_mod.shape[0]
    past = cache_mla_ckv.shape[2]
    n_ctx = batch * seq
    n_lat = dec_batch * dec_seq

    cond = jnp.concatenate([c_ctx[None], c, jnp.zeros((8 - 1 - dec_batch, d), F32)], axis=0)
    mod = _modulation(cond, w_mod, b_mod).reshape(depth, 8, 6, d)

    xp = x_prompt.reshape(n_ctx, d)
    xs = x_sample.reshape(n_lat, d)
    cos_m, sin_m = _mla_rope_lanes(dec_seq)
    cos_g, sin_g = _gqa_rope_lanes(dec_seq)
    ident_cos = jnp.ones((TOKEN_TILE, LANES), F32)
    ident_sin = jnp.zeros((TOKEN_TILE, LANES), F32)
    gfin = norm_final.reshape(1, d)
    states = {k: [] for k in ("ckv", "krope", "nk", "nv", "gk", "gv")}

    for l in range(depth):
        mod_p = mod[l, 0:1]
        mod_s = mod[l, 1:1 + dec_batch]
        gmix = norm_mix[l].reshape(1, d)
        gffn = norm_ffn[l].reshape(1, d)
        if l % 2 == 0:
            e = l // 2
            w_in_k, w_uq_k, w_ukv_k = _even_weights(w_in_a[e], mla_w_uq[e], mla_w_ukv[e])
            qn = mla_q_norm[e].reshape(1, -1)
            kvn = mla_kv_norm[e].reshape(1, -1)
            (qp, kp, vp, nqp, nkp, nvp, s_ckv, s_kr, s_nk, s_nv) = _even_in(
                xp, mod_p, n_ctx, gmix, w_in_k, qn, kvn, w_uq_k, w_ukv_k, ident_cos, ident_sin, True)
            states["ckv"].append(s_ckv.reshape(batch, seq, MLA_KV_LORA))
            states["krope"].append(s_kr[:, MLA_NOPE_DIM:MLA_QK_DIM].reshape(batch, seq, MLA_ROPE_DIM))
            states["nk"].append(s_nk.reshape(batch, seq, NA_HEADS, NA_HEAD_DIM))
            states["nv"].append(s_nv.reshape(batch, seq, NA_HEADS, NA_HEAD_DIM))
            qs, ks, vs, nqs, nks, nvs = _even_in(
                xs, mod_s, dec_seq, gmix, w_in_k, qn, kvn, w_uq_k, w_ukv_k, cos_m, sin_m, False)
            kr_cache = jnp.pad(cache_mla_krope[:, e],
                               ((0, 0), (0, 0), (MLA_NOPE_DIM, LANES - MLA_QK_DIM)))
            kc, vc = _cache_expand(cache_mla_ckv[:, e], kr_cache, w_ukv_k)

            r3 = lambda a, b_: a.reshape(b_, a.shape[0] // b_, a.shape[1])
            mla_kw = dict(groups=MLA_HEADS // 2, heads=2, k_stride=LANES, q_half_mask=False, pair_out=True)
            na_kw = dict(groups=NA_HEADS // 2, heads=2, k_stride=0, q_half_mask=True, pair_out=True)
            a_mla_p = _attention(r3(qp, batch), [(r3(kp, batch), r3(vp, batch))], name="mla_ctx", **mla_kw)
            a_na_p = _attention(r3(nqp, batch), [(r3(nkp, batch), r3(nvp, batch))], name="na_ctx", **na_kw)
            a_mla_s = _attention(r3(qs, dec_batch), [(r3(ks, dec_batch), r3(vs, dec_batch)), (kc, vc)],
                                 name="mla_lat", **mla_kw)
            bias = _na_bias_tables(na_rpb[e])
            a_na_s = _neighbourhood_attention(
                r3(nqs, dec_batch), r3(nks, dec_batch), r3(nvs, dec_batch),
                cache_na_k[:, e].reshape(dec_batch, past, NA_W).astype(BF16),
                cache_na_v[:, e].reshape(dec_batch, past, NA_W).astype(BF16), bias)
            attn_p = [a_mla_p.reshape(n_ctx, -1), a_na_p.reshape(n_ctx, -1)]
            attn_s = [a_mla_s.reshape(n_lat, -1), a_na_s.reshape(n_lat, -1)]
            wo = w_out_a[e].astype(BF16)
            half = MLA_HEADS * MLA_V_DIM
            w_outs = [wo[:half], wo[half:]]
        else:
            o = l // 2
            w_in_k = w_in_c[o].astype(BF16)
            qn = gqa_q_norm[o].reshape(1, -1)
            kn = gqa_k_norm[o].reshape(1, -1)
            qp, kp, vp, s_gk, s_gv = _odd_in(xp, mod_p, n_ctx, gmix, w_in_k, qn, kn, ident_cos, ident_sin, True)
            states["gk"].append(s_gk.reshape(batch, seq, GQA_KV_HEADS, GQA_HEAD_DIM))
            states["gv"].append(s_gv.reshape(batch, seq, GQA_KV_HEADS, GQA_HEAD_DIM))
            qs, ks, vs = _odd_in(xs, mod_s, dec_seq, gmix, w_in_k, qn, kn, cos_g, sin_g, False)
            r3 = lambda a, b_: a.reshape(b_, a.shape[0] // b_, a.shape[1])
            gqa_kw = dict(groups=GQA_KV_HEADS, heads=GQA_GROUP, k_stride=0, q_half_mask=False, pair_out=False)
            a_p = _attention(r3(qp, batch), [(r3(kp, batch), r3(vp, batch))], name="gqa_ctx", **gqa_kw)
            kcache = cache_gqa_k[:, o].reshape(dec_batch, past, -1).astype(BF16)
            vcache = cache_gqa_v[:, o].reshape(dec_batch, past, -1).astype(BF16)
            a_s = _attention(r3(qs, dec_batch), [(r3(ks, dec_batch), r3(vs, dec_batch)), (kcache, vcache)],
                             name="gqa_lat", **gqa_kw)
            attn_p = [a_p.reshape(n_ctx, -1)]
            attn_s = [a_s.reshape(n_lat, -1)]
            w_outs = [w_out_c[o].astype(BF16)]
        last = l == depth - 1
        wfi = w_ffn_in[l].astype(BF16)
        wfo = w_ffn_out[l].astype(BF16)
        xp = _out_ffn(xp, mod_p, n_ctx, attn_p, w_outs, gffn, wfi, wfo, gfin, last)
        xs = _out_ffn(xs, mod_s, dec_seq, attn_s, w_outs, gffn, wfi, wfo, gfin, last)

    y_prompt = xp.reshape(batch, seq, d)
    y_sample = xs.reshape(dec_batch, dec_seq, d)
    return (y_prompt, y_sample,
            jnp.stack(states["ckv"], axis=1), jnp.stack(states["krope"], axis=1),
            jnp.stack(states["nk"], axis=1), jnp.stack(states["nv"], axis=1),
            jnp.stack(states["gk"], axis=1), jnp.stack(states["gv"], axis=1))
```

```python
import functools
import math

import numpy as np
import jax
import jax.numpy as jnp
from jax import lax
from jax.experimental import pallas as pl
from jax.experimental.pallas import tpu as pltpu

LANES = 128
V7X_VMEM_BYTES = 64 * 1024 * 1024

D_MODEL = 1024
GRID_W = 64
ROPE_THETA = 10000.0
RMS_EPS = 1e-6
NEG_INF = -1e30
MLA_HEADS = 8
MLA_Q_LORA = 256
MLA_KV_LORA = 256
MLA_NOPE_DIM = 64
MLA_ROPE_DIM = 32
MLA_V_DIM = 64
MLA_QK_DIM = MLA_NOPE_DIM + MLA_ROPE_DIM
LOG2E = math.log2(math.e)
MLA_SCALE = MLA_QK_DIM ** -0.5 * LOG2E
NA_HEADS = 8
NA_HEAD_DIM = 64
NA_WIN_H = 8
NA_WIN_W = 16
NA_SCALE = NA_HEAD_DIM ** -0.5 * LOG2E
NA_W = NA_HEADS * NA_HEAD_DIM
GQA_HEADS = 8
GQA_KV_HEADS = 2
GQA_HEAD_DIM = 128
GQA_SCALE = GQA_HEAD_DIM ** -0.5 * LOG2E
GQA_GROUP = GQA_HEADS // GQA_KV_HEADS

TOKEN_TILE = 512
ATTN_Q_TILE = 512
ATTN_K_CHUNK = 512
NA_Q_ROWS = 4
NA_BAND_ROWS = 12

BF16 = jnp.bfloat16
F32 = jnp.float32


def _vmem_limit(nbytes):
    return int(min(V7X_VMEM_BYTES - (4 << 20), max(nbytes, 16 << 20)))


def _params(nbytes, ndims):
    return pltpu.CompilerParams(dimension_semantics=("arbitrary",) * ndims,
                                vmem_limit_bytes=_vmem_limit(nbytes))


def _rms(x, gain):
    return x * lax.rsqrt(jnp.mean(x * x, axis=-1, keepdims=True) + RMS_EPS) * gain


def _dot(a, b):
    return jnp.dot(a, b, preferred_element_type=F32)


def _dot_nt(a, b):
    return lax.dot_general(a, b, (((1,), (1,)), ((), ())), preferred_element_type=F32)


def _const_spec(shape):
    nd = len(shape)
    return pl.BlockSpec(shape, lambda *_: (0,) * nd, pipeline_mode=pl.Buffered(1))


def _mod_kernel(cond_ref, w_ref, b_ref, o_ref):
    c = cond_ref[...]
    s = (c * jax.nn.sigmoid(c)).astype(BF16)
    o_ref[0] = _dot(s, w_ref[0].astype(BF16)) + b_ref[0]


def _modulation(cond, w_mod, b_mod):
    depth, d, n = w_mod.shape
    rows = cond.shape[0]
    bn = 1024
    return pl.pallas_call(
        _mod_kernel,
        out_shape=jax.ShapeDtypeStruct((depth, rows, n), F32),
        grid=(depth, n // bn),
        in_specs=[pl.BlockSpec((rows, d), lambda l, j: (0, 0)),
                  pl.BlockSpec((1, d, bn), lambda l, j: (l, 0, j)),
                  pl.BlockSpec((1, 1, bn), lambda l, j: (l, 0, j))],
        out_specs=pl.BlockSpec((1, rows, bn), lambda l, j: (l, 0, j)),
        compiler_params=_params(3 * d * bn * 4, 2),
        name="ada_modulation",
    )(cond, w_mod, b_mod.reshape(depth, 1, n))


def _even_in_kernel(x_ref, mod_ref, gmix_ref, w_in_ref, qn_ref, kvn_ref, w_uq_ref, w_ukv_ref,
                    cos_ref, sin_ref, *out_refs, with_state):
    q_ref, k_ref, v_ref, nq_ref, nk_ref, nv_ref = out_refs[:6]
    x = x_ref[...]
    mod = mod_ref[0]
    h = _rms(x, gmix_ref[...]) * (1.0 + mod[1:2]) + mod[0:1]
    p = _dot(h.astype(BF16), w_in_ref[...])
    cq = p[:, 0:256]
    ckv = _rms(p[:, 256:512], kvn_ref[...])
    nq = p[:, 512:1024]
    nk = p[:, 1024:1536]
    nv = p[:, 1536:2048]
    kr = p[:, 2048:2176]
    kr_sw = p[:, 2176:2304]
    cos = cos_ref[...]
    sin = sin_ref[...]
    qq = _dot(_rms(cq, qn_ref[...]).astype(BF16), w_uq_ref[...])
    kv = _dot(ckv.astype(BF16), w_ukv_ref[...])
    kr_rot = kr * cos + kr_sw * sin
    for hd in range(MLA_HEADS):
        lo = hd * LANES
        qh = qq[:, lo:lo + LANES] * cos + qq[:, 1024 + lo:1024 + lo + LANES] * sin
        q_ref[:, lo:lo + LANES] = (qh * MLA_SCALE).astype(BF16)
        k_ref[:, lo:lo + LANES] = (kv[:, lo:lo + LANES] + kr_rot).astype(BF16)
    v_ref[...] = kv[:, 1024:1536].astype(BF16)
    nq_ref[...] = (nq * NA_SCALE).astype(BF16)
    nk_ref[...] = nk.astype(BF16)
    nv_ref[...] = nv.astype(BF16)
    if with_state:
        s_ckv_ref, s_kr_ref, s_nk_ref, s_nv_ref = out_refs[6:]
        s_ckv_ref[...] = ckv
        s_kr_ref[...] = kr
        s_nk_ref[...] = nk
        s_nv_ref[...] = nv


def _even_in(x, mod, tokens_per_group, gmix, w_in, qn, kvn, w_uq, w_ukv, cos, sin, with_state):
    n = x.shape[0]
    tm = TOKEN_TILE
    tiles_per_group = tokens_per_group // tm
    rope_tiles = cos.shape[0] // tm
    row = lambda i: (i, 0)
    outs = [jax.ShapeDtypeStruct((n, 1024), BF16), jax.ShapeDtypeStruct((n, 1024), BF16),
            jax.ShapeDtypeStruct((n, 512), BF16), jax.ShapeDtypeStruct((n, 512), BF16),
            jax.ShapeDtypeStruct((n, 512), BF16), jax.ShapeDtypeStruct((n, 512), BF16)]
    if with_state:
        outs += [jax.ShapeDtypeStruct((n, 256), F32), jax.ShapeDtypeStruct((n, 128), F32),
                 jax.ShapeDtypeStruct((n, 512), F32), jax.ShapeDtypeStruct((n, 512), F32)]
    return pl.pallas_call(
        functools.partial(_even_in_kernel, with_state=with_state),
        out_shape=outs,
        grid=(n // tm,),
        in_specs=[pl.BlockSpec((tm, D_MODEL), row),
                  pl.BlockSpec((1, 6, D_MODEL), lambda i: (i // tiles_per_group, 0, 0)),
                  _const_spec(gmix.shape), _const_spec(w_in.shape), _const_spec(qn.shape),
                  _const_spec(kvn.shape), _const_spec(w_uq.shape), _const_spec(w_ukv.shape),
                  pl.BlockSpec((tm, LANES), lambda i: (i % rope_tiles, 0)),
                  pl.BlockSpec((tm, LANES), lambda i: (i % rope_tiles, 0))],
        out_specs=[pl.BlockSpec((tm, o.shape[1]), row) for o in outs],
        compiler_params=_params(40 << 20, 1),
        name="even_in",
    )(x, mod, gmix, w_in, qn, kvn, w_uq, w_ukv, cos, sin)


def _cache_expand_kernel(ckv_ref, kr_ref, w_ukv_ref, k_ref, v_ref):
    kv = _dot(ckv_ref[0].astype(BF16), w_ukv_ref[...])
    kr = kr_ref[0]
    for hd in range(MLA_HEADS):
        lo = hd * LANES
        k_ref[0, :, lo:lo + LANES] = (kv[:, lo:lo + LANES] + kr).astype(BF16)
    v_ref[0] = kv[:, 1024:1536].astype(BF16)


def _cache_expand(ckv, kr128, w_ukv):
    b, s, _ = ckv.shape
    return pl.pallas_call(
        _cache_expand_kernel,
        out_shape=[jax.ShapeDtypeStruct((b, s, 1024), BF16), jax.ShapeDtypeStruct((b, s, 512), BF16)],
        grid=(b,),
        in_specs=[pl.BlockSpec((1, s, MLA_KV_LORA), lambda i: (i, 0, 0)),
                  pl.BlockSpec((1, s, LANES), lambda i: (i, 0, 0)),
                  _const_spec(w_ukv.shape)],
        out_specs=[pl.BlockSpec((1, s, 1024), lambda i: (i, 0, 0)),
                   pl.BlockSpec((1, s, 512), lambda i: (i, 0, 0))],
        compiler_params=_params(16 << 20, 1),
        name="mla_cache_expand",
    )(ckv, kr128, w_ukv)


def _odd_in_kernel(x_ref, mod_ref, gmix_ref, w_in_ref, qn_ref, kn_ref, cos_ref, sin_ref, *out_refs,
                   with_state):
    q_ref, k_ref, v_ref = out_refs[:3]
    x = x_ref[...]
    mod = mod_ref[0]
    h = _rms(x, gmix_ref[...]) * (1.0 + mod[1:2]) + mod[0:1]
    p = _dot(h.astype(BF16), w_in_ref[...])
    cos = cos_ref[...]
    sin = sin_ref[...]
    half = GQA_HEAD_DIM // 2

    def rope(t):
        return t * cos + pltpu.roll(t, half, 1) * sin

    for hd in range(GQA_HEADS):
        lo = hd * LANES
        qh = rope(_rms(p[:, lo:lo + LANES], qn_ref[...]))
        q_ref[:, lo:lo + LANES] = (qh * GQA_SCALE).astype(BF16)
    for hd in range(GQA_KV_HEADS):
        lo = hd * LANES
        kh = _rms(p[:, 1024 + lo:1024 + lo + LANES], kn_ref[...])
        vh = p[:, 1280 + lo:1280 + lo + LANES]
        k_ref[:, lo:lo + LANES] = rope(kh).astype(BF16)
        v_ref[:, lo:lo + LANES] = vh.astype(BF16)
        if with_state:
            out_refs[3][:, lo:lo + LANES] = kh
            out_refs[4][:, lo:lo + LANES] = vh


def _odd_in(x, mod, tokens_per_group, gmix, w_in, qn, kn, cos, sin, with_state):
    n = x.shape[0]
    tm = TOKEN_TILE
    tiles_per_group = tokens_per_group // tm
    rope_tiles = cos.shape[0] // tm
    row = lambda i: (i, 0)
    outs = [jax.ShapeDtypeStruct((n, 1024), BF16), jax.ShapeDtypeStruct((n, 256), BF16),
            jax.ShapeDtypeStruct((n, 256), BF16)]
    if with_state:
        outs += [jax.ShapeDtypeStruct((n, 256), F32), jax.ShapeDtypeStruct((n, 256), F32)]
    return pl.pallas_call(
        functools.partial(_odd_in_kernel, with_state=with_state),
        out_shape=outs,
        grid=(n // tm,),
        in_specs=[pl.BlockSpec((tm, D_MODEL), row),
                  pl.BlockSpec((1, 6, D_MODEL), lambda i: (i // tiles_per_group, 0, 0)),
                  _const_spec(gmix.shape), _const_spec(w_in.shape), _const_spec(qn.shape),
                  _const_spec(kn.shape),
                  pl.BlockSpec((tm, LANES), lambda i: (i % rope_tiles, 0)),
                  pl.BlockSpec((tm, LANES), lambda i: (i % rope_tiles, 0))],
        out_specs=[pl.BlockSpec((tm, o.shape[1]), row) for o in outs],
        compiler_params=_params(32 << 20, 1),
        name="odd_in",
    )(x, mod, gmix, w_in, qn, kn, cos, sin)


def _attn_kernel(*refs, n_src, heads, q_tile, src_len, k_stride, q_half_mask, pair_out):
    q_ref = refs[0]
    kv_refs = refs[1:1 + 2 * n_src]
    o_ref = refs[1 + 2 * n_src]
    s_ref = refs[2 + 2 * n_src]
    n_tiles = q_ref.shape[1] // q_tile
    chunks = []
    for src in range(n_src):
        ck = min(ATTN_K_CHUNK, src_len[src])
        for c in range(src_len[src] // ck):
            chunks.append((src, c * ck, ck))
    lane = lax.broadcasted_iota(jnp.int32, (q_tile, LANES), 1)

    def rows(t):
        if isinstance(t, int):
            return slice(t * q_tile, (t + 1) * q_tile)
        return pl.ds(pl.multiple_of(t * q_tile, q_tile), q_tile)

    def load_q(t, j):
        if q_half_mask:
            qb = q_ref[0, rows(t), :]
            return jnp.where((lane >= 64) == (j == 1), qb, jnp.zeros_like(qb))
        return q_ref[0, rows(t), j * LANES:(j + 1) * LANES]

    def slot(j, q_next, m_prev):
        if q_next is not None:
            m_part = jnp.full((q_tile, LANES), -jnp.inf, F32)
        if m_prev is not None:
            l_part = jnp.zeros((q_tile, LANES), F32)
            acc = jnp.zeros((q_tile, LANES), F32)
        off = 0
        for src, k0, ck in chunks:
            if m_prev is not None:
                p = jnp.exp2(s_ref[:, off:off + ck] - m_prev)
                for i in range(ck // LANES):
                    l_part = l_part + p[:, i * LANES:(i + 1) * LANES]
                acc = acc + _dot(p.astype(BF16), kv_refs[2 * src + 1][0, k0:k0 + ck, :])
            if q_next is not None:
                kc = kv_refs[2 * src][0, j * k_stride:j * k_stride + LANES, k0:k0 + ck]
                s = _dot(q_next, kc)
                s_ref[:, off:off + ck] = s
                for i in range(ck // LANES):
                    m_part = jnp.maximum(m_part, s[:, i * LANES:(i + 1) * LANES])
            off += ck
        m_next = None if q_next is None else jnp.max(m_part, axis=-1, keepdims=True)
        o_prev = None if m_prev is None else acc / jnp.sum(l_part, axis=-1, keepdims=True)
        return m_next, o_prev

    def write_out(t, j, o):
        o = o.astype(o_ref.dtype)
        if not pair_out:
            o_ref[0, rows(t), j * LANES:(j + 1) * LANES] = o
        elif j == 0:
            o_ref[0, rows(t), :] = o
        else:
            o_ref[0, rows(t), :] = jnp.where(lane < 64, o_ref[0, rows(t), :], o)

    def tile(j, t, m, last):
        m_next, o = slot(j, None if last else load_q(t + 1, j), m)
        write_out(t, j, o)
        return m_next

    for j in range(heads):
        m, _ = slot(j, load_q(0, j), None)
        if n_tiles > 1:
            m = lax.fori_loop(0, n_tiles - 1, lambda t, m, j=j: tile(j, t, m, False), m)
        tile(j, n_tiles - 1, m, True)


def _attention(q, sources, *, groups, heads, k_stride, q_half_mask, pair_out, name):
    b, t, _ = q.shape
    q_tile = min(ATTN_Q_TILE, t)
    q_block = LANES if q_half_mask else heads * LANES
    k_block = LANES if k_stride == 0 else heads * LANES
    out_block = LANES if pair_out else heads * LANES
    src_len = tuple(k.shape[1] for k, _ in sources)
    in_specs = [pl.BlockSpec((1, t, q_block), lambda bi, g: (bi, 0, g))]
    args = [q]
    for k, v in sources:
        s = k.shape[1]
        in_specs.append(pl.BlockSpec((1, k_block, s), lambda bi, g: (bi, g, 0)))
        in_specs.append(pl.BlockSpec((1, s, LANES), lambda bi, g: (bi, 0, g)))
        args += [jnp.swapaxes(k, 1, 2), v]
    total = sum(src_len)
    return pl.pallas_call(
        functools.partial(_attn_kernel, n_src=len(sources), heads=heads, q_tile=q_tile,
                          src_len=src_len, k_stride=k_stride, q_half_mask=q_half_mask,
                          pair_out=pair_out),
        out_shape=jax.ShapeDtypeStruct((b, t, groups * out_block), BF16),
        grid=(b, groups),
        in_specs=in_specs,
        out_specs=pl.BlockSpec((1, t, out_block), lambda bi, g: (bi, 0, g)),
        scratch_shapes=[pltpu.VMEM((q_tile, total), F32)],
        compiler_params=_params(58 << 20, 2),
        name=name,
    )(*args)


def _na_bias_tables(rpb):
    n_rows = GRID_W
    h, n_dr, n_dc = rpb.shape
    edge = n_dc - 1 - (NA_WIN_W - 1)
    w = jnp.concatenate([rpb[..., NA_WIN_W - 1:],
                         jnp.broadcast_to(rpb[..., n_dc - 1:], (h, n_dr, GRID_W - 1 - edge)),
                         jnp.broadcast_to(rpb[..., :1], (h, n_dr, GRID_W - (NA_WIN_W - 1) + 1)),
                         rpb[..., 1:NA_WIN_W - 1]], axis=-1)
    toe = jnp.tile(w, (1, 1, GRID_W))[..., :GRID_W * (2 * GRID_W - 1)]
    toe = toe.reshape(h, n_dr, GRID_W, 2 * GRID_W - 1)[..., :GRID_W]
    cols = np.arange(GRID_W)
    cs = np.clip(cols - NA_WIN_W // 2, 0, GRID_W - NA_WIN_W)
    col_ok = (cols[None, :] >= cs[:, None]) & (cols[None, :] < cs[:, None] + NA_WIN_W)
    toe = jnp.where(col_ok, toe * LOG2E, NEG_INF)
    masked = jnp.full((h, GRID_W, GRID_W), NEG_INF, F32)
    out = []
    for blk in (0, 1, n_rows // NA_Q_ROWS - 1):
        b0 = int(np.clip(NA_Q_ROWS * blk - NA_WIN_H // 2, 0, n_rows - NA_BAND_ROWS))
        q_rows = []
        for qr in range(NA_Q_ROWS):
            r = NA_Q_ROWS * blk + qr
            rs = int(np.clip(r - NA_WIN_H // 2, 0, n_rows - NA_WIN_H))
            parts = []
            for j in range(NA_BAND_ROWS):
                kr = b0 + j
                parts.append(toe[:, kr - r + NA_WIN_H - 1] if rs <= kr < rs + NA_WIN_H else masked)
            q_rows.append(jnp.concatenate(parts, axis=-1))
        out.append(jnp.concatenate(q_rows, axis=-2))
    return jnp.stack(out)


def _na_kernel(q_ref, k_ref, v_ref, kc_ref, vc_ref, bias_ref, o_ref):
    i = pl.program_id(2)
    nq = NA_Q_ROWS * GRID_W
    nb = NA_BAND_ROWS * GRID_W
    t = k_ref.shape[1]
    start = pl.multiple_of(jnp.clip(nq * i - (NA_WIN_H // 2) * GRID_W, 0, t - nb), nq)
    kb = k_ref[0, pl.ds(start, nb), :]
    vb = v_ref[0, pl.ds(start, nb), :]
    kc = kc_ref[0]
    vc = vc_ref[0]
    qb = q_ref[0]
    lane = lax.broadcasted_iota(jnp.int32, (nq, LANES), 1)
    outs = []
    for j in range(2):
        q = jnp.where((lane >= 64) == (j == 1), qb, jnp.zeros_like(qb))
        s_band = _dot_nt(q, kb) + bias_ref[0, j]
        s_ctx = _dot_nt(q, kc)
        m = jnp.maximum(jnp.max(s_band, axis=-1, keepdims=True), jnp.max(s_ctx, axis=-1, keepdims=True))
        p_band = jnp.exp2(s_band - m)
        p_ctx = jnp.exp2(s_ctx - m)
        l = jnp.sum(p_band, axis=-1, keepdims=True) + jnp.sum(p_ctx, axis=-1, keepdims=True)
        acc = _dot(p_band.astype(BF16), vb) + _dot(p_ctx.astype(BF16), vc)
        outs.append(acc / l)
    o_ref[0] = jnp.where(lane < 64, outs[0], outs[1]).astype(o_ref.dtype)


def _neighbourhood_attention(q, k, v, kc, vc, bias):
    b, t, w = q.shape
    pairs = w // LANES
    nq = NA_Q_ROWS * GRID_W
    nb = NA_BAND_ROWS * GRID_W
    nblk = t // nq
    c = kc.shape[1]
    cls = lambda i: jnp.minimum(i, 1) + jnp.maximum(i - (nblk - 2), 0)
    return pl.pallas_call(
        _na_kernel,
        out_shape=jax.ShapeDtypeStruct((b, t, w), BF16),
        grid=(b, pairs, nblk),
        in_specs=[pl.BlockSpec((1, nq, LANES), lambda bi, g, i: (bi, i, g)),
                  pl.BlockSpec((1, t, LANES), lambda bi, g, i: (bi, 0, g)),
                  pl.BlockSpec((1, t, LANES), lambda bi, g, i: (bi, 0, g)),
                  pl.BlockSpec((1, c, LANES), lambda bi, g, i: (bi, 0, g)),
                  pl.BlockSpec((1, c, LANES), lambda bi, g, i: (bi, 0, g)),
                  pl.BlockSpec((1, 2, nq, nb), lambda bi, g, i: (cls(i), g, 0, 0))],
        out_specs=pl.BlockSpec((1, nq, LANES), lambda bi, g, i: (bi, i, g)),
        compiler_params=_params(32 << 20, 3),
        name="neighbourhood_attention",
    )(q, k, v, kc, vc, bias)


def _out_ffn_kernel(*refs, n_attn, ff_chunk, final_norm):
    x_ref, mod_ref = refs[0], refs[1]
    a_refs = refs[2:2 + n_attn]
    w_refs = refs[2 + n_attn:2 + 2 * n_attn]
    gffn_ref, w_in_ref, w_out_ref, gfin_ref, o_ref = refs[2 + 2 * n_attn:]
    mod = mod_ref[0]
    mix = _dot(a_refs[0][...], w_refs[0][...])
    for a_ref, w_ref in zip(a_refs[1:], w_refs[1:]):
        mix = mix + _dot(a_ref[...], w_ref[...])
    x1 = x_ref[...] + mod[2:3] * mix
    h = (_rms(x1, gffn_ref[...]) * (1.0 + mod[4:5]) + mod[3:4]).astype(BF16)
    d_ff = w_out_ref.shape[0]
    acc = None
    for c in range(d_ff // ff_chunk):
        lo = c * ff_chunk
        gate = _dot(h, w_in_ref[:, lo:lo + ff_chunk])
        up = _dot(h, w_in_ref[:, d_ff + lo:d_ff + lo + ff_chunk])
        act = (gate * jax.nn.sigmoid(gate) * up).astype(BF16)
        part = _dot(act, w_out_ref[lo:lo + ff_chunk, :])
        acc = part if acc is None else acc + part
    x2 = x1 + mod[5:6] * acc
    if final_norm:
        x2 = _rms(x2, gfin_ref[...])
    o_ref[...] = x2


def _out_ffn(x, mod, tokens_per_group, attn, w_outs, gffn, w_ffn_in, w_ffn_out, gfin, final_norm):
    n = x.shape[0]
    tm = TOKEN_TILE
    tiles_per_group = tokens_per_group // tm
    row = lambda i: (i, 0)
    in_specs = [pl.BlockSpec((tm, D_MODEL), row),
                pl.BlockSpec((1, 6, D_MODEL), lambda i: (i // tiles_per_group, 0, 0))]
    in_specs += [pl.BlockSpec((tm, a.shape[1]), row) for a in attn]
    in_specs += [_const_spec(w.shape) for w in w_outs]
    in_specs += [_const_spec(gffn.shape), _const_spec(w_ffn_in.shape), _const_spec(w_ffn_out.shape),
                 _const_spec(gfin.shape)]
    return pl.pallas_call(
        functools.partial(_out_ffn_kernel, n_attn=len(attn), ff_chunk=256, final_norm=final_norm),
        out_shape=jax.ShapeDtypeStruct((n, D_MODEL), F32),
        grid=(n // tm,),
        in_specs=in_specs,
        out_specs=pl.BlockSpec((tm, D_MODEL), row),
        compiler_params=_params(56 << 20, 1),
        name="out_ffn",
    )(x, mod, *attn, *w_outs, gffn, w_ffn_in, w_ffn_out, gfin)


def _rope_tables(n_tokens, rot_dim):
    t = jnp.arange(n_tokens)
    row = (t // GRID_W).astype(F32)
    col = (t % GRID_W).astype(F32)
    axis_dim = rot_dim // 2
    inv_freq = ROPE_THETA ** (-jnp.arange(0, axis_dim, 2, dtype=F32) / axis_dim)
    ang = jnp.concatenate([row[:, None] * inv_freq, col[:, None] * inv_freq], axis=-1)
    return jnp.cos(ang), jnp.sin(ang)


def _mla_rope_lanes(n_tokens):
    cos, sin = _rope_tables(n_tokens, MLA_ROPE_DIM)
    one = jnp.ones((n_tokens, MLA_NOPE_DIM), F32)
    zero = jnp.zeros((n_tokens, MLA_NOPE_DIM), F32)
    pad1 = jnp.ones((n_tokens, LANES - MLA_QK_DIM), F32)
    pad0 = jnp.zeros((n_tokens, LANES - MLA_QK_DIM), F32)
    return (jnp.concatenate([one, cos, cos, pad1], axis=-1),
            jnp.concatenate([zero, -sin, sin, pad0], axis=-1))


def _gqa_rope_lanes(n_tokens):
    cos, sin = _rope_tables(n_tokens, GQA_HEAD_DIM)
    return jnp.concatenate([cos, cos], axis=-1), jnp.concatenate([-sin, sin], axis=-1)


def _swap_halves(w):
    half = w.shape[-1] // 2
    return jnp.concatenate([w[..., half:], w[..., :half]], axis=-1)


def _even_weights(w_in, w_uq, w_ukv):
    d = w_in.shape[0]
    i0 = MLA_Q_LORA
    i1 = i0 + MLA_KV_LORA
    i2 = i1 + MLA_ROPE_DIM
    w_kr = w_in[:, i1:i2]
    zl = jnp.zeros((d, MLA_NOPE_DIM), F32)
    zr = jnp.zeros((d, LANES - MLA_QK_DIM), F32)
    w_in_k = jnp.concatenate([w_in[:, :i1], w_in[:, i2:], zl, w_kr, zr, zl, _swap_halves(w_kr), zr],
                             axis=-1).astype(BF16)
    r = w_uq.shape[0]
    uq = w_uq.reshape(r, MLA_HEADS, MLA_QK_DIM)
    zpad = jnp.zeros((r, MLA_HEADS, LANES - MLA_QK_DIM), F32)
    znope = jnp.zeros((r, MLA_HEADS, MLA_NOPE_DIM), F32)
    q_plain = jnp.concatenate([uq, zpad], axis=-1).reshape(r, MLA_HEADS * LANES)
    q_swap = jnp.concatenate([znope, _swap_halves(uq[..., MLA_NOPE_DIM:]), zpad], axis=-1)
    w_uq_k = jnp.concatenate([q_plain, q_swap.reshape(r, MLA_HEADS * LANES)], axis=-1).astype(BF16)
    r = w_ukv.shape[0]
    ukv = w_ukv.reshape(r, MLA_HEADS, MLA_NOPE_DIM + MLA_V_DIM)
    k_pad = jnp.concatenate([ukv[..., :MLA_NOPE_DIM], jnp.zeros((r, MLA_HEADS, LANES - MLA_NOPE_DIM), F32)],
                            axis=-1).reshape(r, MLA_HEADS * LANES)
    v_cat = ukv[..., MLA_NOPE_DIM:].reshape(r, MLA_HEADS * MLA_V_DIM)
    w_ukv_k = jnp.concatenate([k_pad, v_cat], axis=-1).astype(BF16)
    return w_in_k, w_uq_k, w_ukv_k


def kernel(x_prompt, x_sample, cache_mla_ckv, cache_mla_krope, cache_na_k, cache_na_v, cache_gqa_k, cache_gqa_v, c, c_ctx, w_mod, b_mod, norm_mix, norm_ffn, norm_final, w_in_a, mla_q_norm, mla_w_uq, mla_kv_norm, mla_w_ukv, na_rpb, w_out_a, w_in_c, gqa_q_norm, gqa_k_norm, w_out_c, w_ffn_in, w_ffn_out):
    batch, seq, d = x_prompt.shape
    dec_batch, dec_seq, _ = x_sample.shape
    depth = w_mod.shape[0]
    past = cache_mla_ckv.shape[2]
    n_ctx = batch * seq
    n_lat = dec_batch * dec_seq

    cond = jnp.concatenate([c_ctx[None], c, jnp.zeros((8 - 1 - dec_batch, d), F32)], axis=0)
    mod = _modulation(cond, w_mod, b_mod).reshape(depth, 8, 6, d)

    xp = x_prompt.reshape(n_ctx, d)
    xs = x_sample.reshape(n_lat, d)
    cos_m, sin_m = _mla_rope_lanes(dec_seq)
    cos_g, sin_g = _gqa_rope_lanes(dec_seq)
    ident_cos = jnp.ones((TOKEN_TILE, LANES), F32)
    ident_sin = jnp.zeros((TOKEN_TILE, LANES), F32)
    gfin = norm_final.reshape(1, d)
    states = {k: [] for k in ("ckv", "krope", "nk", "nv", "gk", "gv")}

    for l in range(depth):
        mod_p = mod[l, 0:1]
        mod_s = mod[l, 1:1 + dec_batch]
        gmix = norm_mix[l].reshape(1, d)
        gffn = norm_ffn[l].reshape(1, d)
        if l % 2 == 0:
            e = l // 2
            w_in_k, w_uq_k, w_ukv_k = _even_weights(w_in_a[e], mla_w_uq[e], mla_w_ukv[e])
            qn = mla_q_norm[e].reshape(1, -1)
            kvn = mla_kv_norm[e].reshape(1, -1)
            (qp, kp, vp, nqp, nkp, nvp, s_ckv, s_kr, s_nk, s_nv) = _even_in(
                xp, mod_p, n_ctx, gmix, w_in_k, qn, kvn, w_uq_k, w_ukv_k, ident_cos, ident_sin, True)
            states["ckv"].append(s_ckv.reshape(batch, seq, MLA_KV_LORA))
            states["krope"].append(s_kr[:, MLA_NOPE_DIM:MLA_QK_DIM].reshape(batch, seq, MLA_ROPE_DIM))
            states["nk"].append(s_nk.reshape(batch, seq, NA_HEADS, NA_HEAD_DIM))
            states["nv"].append(s_nv.reshape(batch, seq, NA_HEADS, NA_HEAD_DIM))
            qs, ks, vs, nqs, nks, nvs = _even_in(
                xs, mod_s, dec_seq, gmix, w_in_k, qn, kvn, w_uq_k, w_ukv_k, cos_m, sin_m, False)
            kr_cache = jnp.pad(cache_mla_krope[:, e],
                               ((0, 0), (0, 0), (MLA_NOPE_DIM, LANES - MLA_QK_DIM)))
            kc, vc = _cache_expand(cache_mla_ckv[:, e], kr_cache, w_ukv_k)

            r3 = lambda a, b_: a.reshape(b_, a.shape[0] // b_, a.shape[1])
            mla_kw = dict(groups=MLA_HEADS // 2, heads=2, k_stride=LANES, q_half_mask=False, pair_out=True)
            na_kw = dict(groups=NA_HEADS // 2, heads=2, k_stride=0, q_half_mask=True, pair_out=True)
            a_mla_p = _attention(r3(qp, batch), [(r3(kp, batch), r3(vp, batch))], name="mla_ctx", **mla_kw)
            a_na_p = _attention(r3(nqp, batch), [(r3(nkp, batch), r3(nvp, batch))], name="na_ctx", **na_kw)
            a_mla_s = _attention(r3(qs, dec_batch), [(r3(ks, dec_batch), r3(vs, dec_batch)), (kc, vc)],
                                 name="mla_lat", **mla_kw)
            bias = _na_bias_tables(na_rpb[e])
            a_na_s = _neighbourhood_attention(
                r3(nqs, dec_batch), r3(nks, dec_batch), r3(nvs, dec_batch),
                cache_na_k[:, e].reshape(dec_batch, past, NA_W).astype(BF16),
                cache_na_v[:, e].reshape(dec_batch, past, NA_W).astype(BF16), bias)
            attn_p = [a_mla_p.reshape(n_ctx, -1), a_na_p.reshape(n_ctx, -1)]
            attn_s = [a_mla_s.reshape(n_lat, -1), a_na_s.reshape(n_lat, -1)]
            wo = w_out_a[e].astype(BF16)
            half = MLA_HEADS * MLA_V_DIM
            w_outs = [wo[:half], wo[half:]]
        else:
            o = l // 2
            w_in_k = w_in_c[o].astype(BF16)
            qn = gqa_q_norm[o].reshape(1, -1)
            kn = gqa_k_norm[o].reshape(1, -1)
            qp, kp, vp, s_gk, s_gv = _odd_in(xp, mod_p, n_ctx, gmix, w_in_k, qn, kn, ident_cos, ident_sin, True)
            states["gk"].append(s_gk.reshape(batch, seq, GQA_KV_HEADS, GQA_HEAD_DIM))
            states["gv"].append(s_gv.reshape(batch, seq, GQA_KV_HEADS, GQA_HEAD_DIM))
            qs, ks, vs = _odd_in(xs, mod_s, dec_seq, gmix, w_in_k, qn, kn, cos_g, sin_g, False)
            r3 = lambda a, b_: a.reshape(b_, a.shape[0] // b_, a.shape[1])
            gqa_kw = dict(groups=GQA_KV_HEADS, heads=GQA_GROUP, k_stride=0, q_half_mask=False, pair_out=False)
            a_p = _attention(r3(qp, batch), [(r3(kp, batch), r3(vp, batch))], name="gqa_ctx", **gqa_kw)
            kcache = cache_gqa_k[:, o].reshape(dec_batch, past, -1).astype(BF16)
            vcache = cache_gqa_v[:, o].reshape(dec_batch, past, -1).astype(BF16)
            a_s = _attention(r3(qs, dec_batch), [(r3(ks, dec_batch), r3(vs, dec_batch)), (kcache, vcache)],
                             name="gqa_lat", **gqa_kw)
            attn_p = [a_p.reshape(n_ctx, -1)]
            attn_s = [a_s.reshape(n_lat, -1)]
            w_outs = [w_out_c[o].astype(BF16)]
        last = l == depth - 1
        wfi = w_ffn_in[l].astype(BF16)
        wfo = w_ffn_out[l].astype(BF16)
        xp = _out_ffn(xp, mod_p, n_ctx, attn_p, w_outs, gffn, wfi, wfo, gfin, last)
        xs = _out_ffn(xs, mod_s, dec_seq, attn_s, w_outs, gffn, wfi, wfo, gfin, last)

    y_prompt = xp.reshape(batch, seq, d)
    y_sample = xs.reshape(dec_batch, dec_seq, d)
    return (y_prompt, y_sample,
            jnp.stack(states["ckv"], axis=1), jnp.stack(states["krope"], axis=1),
            jnp.stack(states["nk"], axis=1), jnp.stack(states["nv"], axis=1),
            jnp.stack(states["gk"], axis=1), jnp.stack(states["gv"], axis=1))
```

```python
import functools
import math

import numpy as np
import jax
import jax.numpy as jnp
from jax import lax
from jax.experimental import pallas as pl
from jax.experimental.pallas import tpu as pltpu

LANES = 128
V7X_VMEM_BYTES = 64 * 1024 * 1024

D_MODEL = 1024
GRID_W = 64
ROPE_THETA = 10000.0
RMS_EPS = 1e-6
NEG_INF = -1e30
MLA_HEADS = 8
MLA_Q_LORA = 256
MLA_KV_LORA = 256
MLA_NOPE_DIM = 64
MLA_ROPE_DIM = 32
MLA_V_DIM = 64
MLA_QK_DIM = MLA_NOPE_DIM + MLA_ROPE_DIM
LOG2E = math.log2(math.e)
MLA_SCALE = MLA_QK_DIM ** -0.5 * LOG2E
NA_HEADS = 8
NA_HEAD_DIM = 64
NA_WIN_H = 8
NA_WIN_W = 16
NA_SCALE = NA_HEAD_DIM ** -0.5 * LOG2E
NA_W = NA_HEADS * NA_HEAD_DIM
GQA_HEADS = 8
GQA_KV_HEADS = 2
GQA_HEAD_DIM = 128
GQA_SCALE = GQA_HEAD_DIM ** -0.5 * LOG2E
GQA_GROUP = GQA_HEADS // GQA_KV_HEADS

TOKEN_TILE = 512
ATTN_Q_TILE = 512
ATTN_K_CHUNK = 512
NA_Q_ROWS = 4
NA_BAND_ROWS = 12
NA_BLOCKS_PER_STEP = 2

BF16 = jnp.bfloat16
F32 = jnp.float32


def _vmem_limit(nbytes):
    return int(min(V7X_VMEM_BYTES - (4 << 20), max(nbytes, 16 << 20)))


def _params(nbytes, ndims):
    return pltpu.CompilerParams(dimension_semantics=("arbitrary",) * ndims,
                                vmem_limit_bytes=_vmem_limit(nbytes))


def _rms(x, gain):
    return x * lax.rsqrt(jnp.mean(x * x, axis=-1, keepdims=True) + RMS_EPS) * gain


def _dot(a, b):
    return jnp.dot(a, b, preferred_element_type=F32)


def _dot_nt(a, b):
    return lax.dot_general(a, b, (((1,), (1,)), ((), ())), preferred_element_type=F32)


def _const_spec(shape):
    nd = len(shape)
    return pl.BlockSpec(shape, lambda *_: (0,) * nd, pipeline_mode=pl.Buffered(1))


def _mod_kernel(cond_ref, w_ref, b_ref, o_ref):
    c = cond_ref[...]
    s = (c * jax.nn.sigmoid(c)).astype(BF16)
    o_ref[0] = _dot(s, w_ref[0].astype(BF16)) + b_ref[0]


def _modulation(cond, w_mod, b_mod):
    depth, d, n = w_mod.shape
    rows = cond.shape[0]
    bn = 1024
    return pl.pallas_call(
        _mod_kernel,
        out_shape=jax.ShapeDtypeStruct((depth, rows, n), F32),
        grid=(depth, n // bn),
        in_specs=[pl.BlockSpec((rows, d), lambda l, j: (0, 0)),
                  pl.BlockSpec((1, d, bn), lambda l, j: (l, 0, j)),
                  pl.BlockSpec((1, 1, bn), lambda l, j: (l, 0, j))],
        out_specs=pl.BlockSpec((1, rows, bn), lambda l, j: (l, 0, j)),
        compiler_params=_params(3 * d * bn * 4, 2),
        name="ada_modulation",
    )(cond, w_mod, b_mod.reshape(depth, 1, n))


def _even_in_kernel(x_ref, mod_ref, gmix_ref, w_in_ref, qn_ref, kvn_ref, w_uq_ref, w_ukv_ref,
                    cos_ref, sin_ref, *out_refs, with_state):
    q_ref, k_ref, v_ref, nq_ref, nk_ref, nv_ref = out_refs[:6]
    x = x_ref[...]
    mod = mod_ref[0]
    h = _rms(x, gmix_ref[...]) * (1.0 + mod[1:2]) + mod[0:1]
    p = _dot(h.astype(BF16), w_in_ref[...])
    cq = p[:, 0:256]
    ckv = _rms(p[:, 256:512], kvn_ref[...])
    nq = p[:, 512:1024]
    nk = p[:, 1024:1536]
    nv = p[:, 1536:2048]
    kr = p[:, 2048:2176]
    kr_sw = p[:, 2176:2304]
    cos = cos_ref[...]
    sin = sin_ref[...]
    qq = _dot(_rms(cq, qn_ref[...]).astype(BF16), w_uq_ref[...])
    kv = _dot(ckv.astype(BF16), w_ukv_ref[...])
    kr_rot = kr * cos + kr_sw * sin
    for hd in range(MLA_HEADS):
        lo = hd * LANES
        qh = qq[:, lo:lo + LANES] * cos + qq[:, 1024 + lo:1024 + lo + LANES] * sin
        q_ref[:, lo:lo + LANES] = qh.astype(BF16)
        k_ref[:, lo:lo + LANES] = (kv[:, lo:lo + LANES] + kr_rot).astype(BF16)
    v_ref[...] = kv[:, 1024:1536].astype(BF16)
    nq_ref[...] = (nq * NA_SCALE).astype(BF16)
    nk_ref[...] = nk.astype(BF16)
    nv_ref[...] = nv.astype(BF16)
    if with_state:
        s_ckv_ref, s_kr_ref, s_nk_ref, s_nv_ref = out_refs[6:]
        s_ckv_ref[...] = ckv
        s_kr_ref[...] = kr
        s_nk_ref[...] = nk
        s_nv_ref[...] = nv


def _even_in(x, mod, tokens_per_group, gmix, w_in, qn, kvn, w_uq, w_ukv, cos, sin, with_state):
    n = x.shape[0]
    tm = TOKEN_TILE
    tiles_per_group = tokens_per_group // tm
    rope_tiles = cos.shape[0] // tm
    row = lambda i: (i, 0)
    outs = [jax.ShapeDtypeStruct((n, 1024), BF16), jax.ShapeDtypeStruct((n, 1024), BF16),
            jax.ShapeDtypeStruct((n, 512), BF16), jax.ShapeDtypeStruct((n, 512), BF16),
            jax.ShapeDtypeStruct((n, 512), BF16), jax.ShapeDtypeStruct((n, 512), BF16)]
    if with_state:
        outs += [jax.ShapeDtypeStruct((n, 256), F32), jax.ShapeDtypeStruct((n, 128), F32),
                 jax.ShapeDtypeStruct((n, 512), F32), jax.ShapeDtypeStruct((n, 512), F32)]
    return pl.pallas_call(
        functools.partial(_even_in_kernel, with_state=with_state),
        out_shape=outs,
        grid=(n // tm,),
        in_specs=[pl.BlockSpec((tm, D_MODEL), row),
                  pl.BlockSpec((1, 6, D_MODEL), lambda i: (i // tiles_per_group, 0, 0)),
                  _const_spec(gmix.shape), _const_spec(w_in.shape), _const_spec(qn.shape),
                  _const_spec(kvn.shape), _const_spec(w_uq.shape), _const_spec(w_ukv.shape),
                  pl.BlockSpec((tm, LANES), lambda i: (i % rope_tiles, 0)),
                  pl.BlockSpec((tm, LANES), lambda i: (i % rope_tiles, 0))],
        out_specs=[pl.BlockSpec((tm, o.shape[1]), row) for o in outs],
        compiler_params=_params(40 << 20, 1),
        name="even_in",
    )(x, mod, gmix, w_in, qn, kvn, w_uq, w_ukv, cos, sin)


def _cache_expand_kernel(ckv_ref, kr_ref, w_ukv_ref, k_ref, v_ref):
    kv = _dot(ckv_ref[0].astype(BF16), w_ukv_ref[...])
    kr = kr_ref[0]
    for hd in range(MLA_HEADS):
        lo = hd * LANES
        k_ref[0, :, lo:lo + LANES] = (kv[:, lo:lo + LANES] + kr).astype(BF16)
    v_ref[0] = kv[:, 1024:1536].astype(BF16)


def _cache_expand(ckv, kr128, w_ukv):
    b, s, _ = ckv.shape
    return pl.pallas_call(
        _cache_expand_kernel,
        out_shape=[jax.ShapeDtypeStruct((b, s, 1024), BF16), jax.ShapeDtypeStruct((b, s, 512), BF16)],
        grid=(b,),
        in_specs=[pl.BlockSpec((1, s, MLA_KV_LORA), lambda i: (i, 0, 0)),
                  pl.BlockSpec((1, s, LANES), lambda i: (i, 0, 0)),
                  _const_spec(w_ukv.shape)],
        out_specs=[pl.BlockSpec((1, s, 1024), lambda i: (i, 0, 0)),
                   pl.BlockSpec((1, s, 512), lambda i: (i, 0, 0))],
        compiler_params=_params(16 << 20, 1),
        name="mla_cache_expand",
    )(ckv, kr128, w_ukv)


def _odd_in_kernel(x_ref, mod_ref, gmix_ref, w_in_ref, qn_ref, kn_ref, cos_ref, sin_ref, *out_refs,
                   with_state):
    q_ref, k_ref, v_ref = out_refs[:3]
    x = x_ref[...]
    mod = mod_ref[0]
    h = (_rms(x, gmix_ref[...]) * (1.0 + mod[1:2]) + mod[0:1]).astype(BF16)
    cos = cos_ref[...]
    sin = sin_ref[...]
    half = GQA_HEAD_DIM // 2

    def rope(t):
        return t * cos + pltpu.roll(t, half, 1) * sin

    pair = 2 * LANES
    for g in range(GQA_HEADS // 2):
        p = _dot(h, w_in_ref[:, g * pair:(g + 1) * pair])
        for i in range(2):
            lo = g * pair + i * LANES
            qh = rope(_rms(p[:, i * LANES:(i + 1) * LANES], qn_ref[...]))
            q_ref[:, lo:lo + LANES] = qh.astype(BF16)
    k_off = GQA_HEADS * LANES
    v_off = k_off + GQA_KV_HEADS * LANES
    pk = _dot(h, w_in_ref[:, k_off:v_off])
    pv = _dot(h, w_in_ref[:, v_off:v_off + GQA_KV_HEADS * LANES])
    for hd in range(GQA_KV_HEADS):
        lo = hd * LANES
        kh = _rms(pk[:, lo:lo + LANES], kn_ref[...])
        vh = pv[:, lo:lo + LANES]
        k_ref[:, lo:lo + LANES] = rope(kh).astype(BF16)
        v_ref[:, lo:lo + LANES] = vh.astype(BF16)
        if with_state:
            out_refs[3][:, lo:lo + LANES] = kh
            out_refs[4][:, lo:lo + LANES] = vh


def _odd_in(x, mod, tokens_per_group, gmix, w_in, qn, kn, cos, sin, with_state):
    n = x.shape[0]
    tm = TOKEN_TILE
    tiles_per_group = tokens_per_group // tm
    rope_tiles = cos.shape[0] // tm
    row = lambda i: (i, 0)
    outs = [jax.ShapeDtypeStruct((n, 1024), BF16), jax.ShapeDtypeStruct((n, 256), BF16),
            jax.ShapeDtypeStruct((n, 256), BF16)]
    if with_state:
        outs += [jax.ShapeDtypeStruct((n, 256), F32), jax.ShapeDtypeStruct((n, 256), F32)]
    return pl.pallas_call(
        functools.partial(_odd_in_kernel, with_state=with_state),
        out_shape=outs,
        grid=(n // tm,),
        in_specs=[pl.BlockSpec((tm, D_MODEL), row),
                  pl.BlockSpec((1, 6, D_MODEL), lambda i: (i // tiles_per_group, 0, 0)),
                  _const_spec(gmix.shape), _const_spec(w_in.shape), _const_spec(qn.shape),
                  _const_spec(kn.shape),
                  pl.BlockSpec((tm, LANES), lambda i: (i % rope_tiles, 0)),
                  pl.BlockSpec((tm, LANES), lambda i: (i % rope_tiles, 0))],
        out_specs=[pl.BlockSpec((tm, o.shape[1]), row) for o in outs],
        compiler_params=_params(32 << 20, 1),
        name="odd_in",
    )(x, mod, gmix, w_in, qn, kn, cos, sin)


def _attn_kernel(*refs, n_src, heads, q_tile, src_len, k_stride, q_half_mask, pair_out):
    q_ref = refs[0]
    kv_refs = refs[1:1 + 2 * n_src]
    o_ref = refs[1 + 2 * n_src]
    s_ref = refs[2 + 2 * n_src]
    n_tiles = q_ref.shape[1] // q_tile
    chunks = []
    for src in range(n_src):
        ck = min(ATTN_K_CHUNK, src_len[src])
        for c in range(src_len[src] // ck):
            chunks.append((src, c * ck, ck))
    lane = lax.broadcasted_iota(jnp.int32, (q_tile, LANES), 1)

    def rows(t):
        if isinstance(t, int):
            return slice(t * q_tile, (t + 1) * q_tile)
        return pl.ds(pl.multiple_of(t * q_tile, q_tile), q_tile)

    def load_q(t, j):
        if q_half_mask:
            qb = q_ref[0, rows(t), :]
            return jnp.where((lane >= 64) == (j == 1), qb, jnp.zeros_like(qb))
        return q_ref[0, rows(t), j * LANES:(j + 1) * LANES]

    def slot(j, q_next, m_prev):
        if q_next is not None:
            m_part = jnp.full((q_tile, LANES), -jnp.inf, F32)
        if m_prev is not None:
            l_part = jnp.zeros((q_tile, LANES), F32)
            acc = jnp.zeros((q_tile, LANES), F32)
        off = 0
        for src, k0, ck in chunks:
            if m_prev is not None:
                p = jnp.exp2(s_ref[:, off:off + ck] - m_prev)
                for i in range(ck // LANES):
                    l_part = l_part + p[:, i * LANES:(i + 1) * LANES]
                acc = acc + _dot(p.astype(BF16), kv_refs[2 * src + 1][0, k0:k0 + ck, :])
            if q_next is not None:
                kc = kv_refs[2 * src][0, j * k_stride:j * k_stride + LANES, k0:k0 + ck]
                s = _dot(q_next, kc)
                s_ref[:, off:off + ck] = s
                for i in range(ck // LANES):
                    m_part = jnp.maximum(m_part, s[:, i * LANES:(i + 1) * LANES])
            off += ck
        m_next = None if q_next is None else jnp.max(m_part, axis=-1, keepdims=True)
        o_prev = None if m_prev is None else acc / jnp.sum(l_part, axis=-1, keepdims=True)
        return m_next, o_prev

    def write_out(t, j, o):
        o = o.astype(o_ref.dtype)
        if not pair_out:
            o_ref[0, rows(t), j * LANES:(j + 1) * LANES] = o
        elif j == 0:
            o_ref[0, rows(t), :] = o
        else:
            o_ref[0, rows(t), :] = jnp.where(lane < 64, o_ref[0, rows(t), :], o)

    def tile(j, t, m, last):
        m_next, o = slot(j, None if last else load_q(t + 1, j), m)
        write_out(t, j, o)
        return m_next

    for j in range(heads):
        m, _ = slot(j, load_q(0, j), None)
        if n_tiles > 1:
            m = lax.fori_loop(0, n_tiles - 1, lambda t, m, j=j: tile(j, t, m, False), m)
        tile(j, n_tiles - 1, m, True)


def _attention(q, sources, *, groups, heads, k_stride, q_half_mask, pair_out, name):
    b, t, _ = q.shape
    q_tile = min(ATTN_Q_TILE, t)
    q_block = LANES if q_half_mask else heads * LANES
    k_block = LANES if k_stride == 0 else heads * LANES
    out_block = LANES if pair_out else heads * LANES
    src_len = tuple(k.shape[1] for k, _ in sources)
    in_specs = [pl.BlockSpec((1, t, q_block), lambda bi, g: (bi, 0, g))]
    args = [q]
    for k, v in sources:
        s = k.shape[1]
        in_specs.append(pl.BlockSpec((1, k_block, s), lambda bi, g: (bi, g, 0)))
        in_specs.append(pl.BlockSpec((1, s, LANES), lambda bi, g: (bi, 0, g)))
        args += [jnp.swapaxes(k, 1, 2), v]
    total = sum(src_len)
    return pl.pallas_call(
        functools.partial(_attn_kernel, n_src=len(sources), heads=heads, q_tile=q_tile,
                          src_len=src_len, k_stride=k_stride, q_half_mask=q_half_mask,
                          pair_out=pair_out),
        out_shape=jax.ShapeDtypeStruct((b, t, groups * out_block), BF16),
        grid=(b, groups),
        in_specs=in_specs,
        out_specs=pl.BlockSpec((1, t, out_block), lambda bi, g: (bi, 0, g)),
        scratch_shapes=[pltpu.VMEM((q_tile, total), F32)],
        compiler_params=_params(58 << 20, 2),
        name=name,
    )(*args)


def _ctx_attn_kernel(q_ref, k_ref, v_ref, o_ref, *, heads, k_stride, q_half_mask, pair_out):
    nb, t, _ = q_ref.shape
    lane = lax.broadcasted_iota(jnp.int32, (t, LANES), 1)
    for b in range(nb):
        outs = []
        for j in range(heads):
            if q_half_mask:
                qb = q_ref[b]
                q = jnp.where((lane >= 64) == (j == 1), qb, jnp.zeros_like(qb))
            else:
                q = q_ref[b, :, j * LANES:(j + 1) * LANES]
            s = _dot_nt(q, k_ref[b, :, j * k_stride:j * k_stride + LANES])
            p = jnp.exp2(s - jnp.max(s, axis=-1, keepdims=True))
            acc = _dot(p.astype(BF16), v_ref[b])
            outs.append(acc / jnp.sum(p, axis=-1, keepdims=True))
        if pair_out:
            o_ref[b] = jnp.where(lane < 64, outs[0], outs[1]).astype(o_ref.dtype)
        else:
            for j in range(heads):
                o_ref[b, :, j * LANES:(j + 1) * LANES] = outs[j].astype(o_ref.dtype)


def _ctx_attention(q, k, v, *, groups, heads, k_stride, q_half_mask, pair_out, name):
    b, t, _ = q.shape
    nb = 4
    q_block = LANES if q_half_mask else heads * LANES
    k_block = LANES if k_stride == 0 else heads * LANES
    out_block = LANES if pair_out else heads * LANES
    spec = lambda w: pl.BlockSpec((nb, t, w), lambda bi, g: (bi, 0, g))
    return pl.pallas_call(
        functools.partial(_ctx_attn_kernel, heads=heads, k_stride=k_stride, q_half_mask=q_half_mask,
                          pair_out=pair_out),
        out_shape=jax.ShapeDtypeStruct((b, t, groups * out_block), BF16),
        grid=(b // nb, groups),
        in_specs=[spec(q_block), spec(k_block), spec(LANES)],
        out_specs=spec(out_block),
        compiler_params=_params(32 << 20, 2),
        name=name,
    )(q, k, v)


def _na_bias_tables(rpb):
    n_rows = GRID_W
    h, n_dr, n_dc = rpb.shape
    edge = n_dc - 1 - (NA_WIN_W - 1)
    w = jnp.concatenate([rpb[..., NA_WIN_W - 1:],
                         jnp.broadcast_to(rpb[..., n_dc - 1:], (h, n_dr, GRID_W - 1 - edge)),
                         jnp.broadcast_to(rpb[..., :1], (h, n_dr, GRID_W - (NA_WIN_W - 1) + 1)),
                         rpb[..., 1:NA_WIN_W - 1]], axis=-1)
    toe = jnp.tile(w, (1, 1, GRID_W))[..., :GRID_W * (2 * GRID_W - 1)]
    toe = toe.reshape(h, n_dr, GRID_W, 2 * GRID_W - 1)[..., :GRID_W]
    cols = np.arange(GRID_W)
    cs = np.clip(cols - NA_WIN_W // 2, 0, GRID_W - NA_WIN_W)
    col_ok = (cols[None, :] >= cs[:, None]) & (cols[None, :] < cs[:, None] + NA_WIN_W)
    toe = jnp.where(col_ok, toe * LOG2E, NEG_INF)
    masked = jnp.full((h, GRID_W, GRID_W), NEG_INF, F32)
    out = []
    for blk in (0, 1, n_rows // NA_Q_ROWS - 1):
        b0 = int(np.clip(NA_Q_ROWS * blk - NA_WIN_H // 2, 0, n_rows - NA_BAND_ROWS))
        q_rows = []
        for qr in range(NA_Q_ROWS):
            r = NA_Q_ROWS * blk + qr
            rs = int(np.clip(r - NA_WIN_H // 2, 0, n_rows - NA_WIN_H))
            parts = []
            for j in range(NA_BAND_ROWS):
                kr = b0 + j
                parts.append(toe[:, kr - r + NA_WIN_H - 1] if rs <= kr < rs + NA_WIN_H else masked)
            q_rows.append(jnp.concatenate(parts, axis=-1))
        out.append(jnp.concatenate(q_rows, axis=-2))
    return jnp.stack(out)


def _na_kernel(q_ref, k_ref, v_ref, kc_ref, vc_ref, *rest):
    bias_refs, o_ref = rest[:-1], rest[-1]
    step = pl.program_id(2)
    nq = NA_Q_ROWS * GRID_W
    nb = NA_BAND_ROWS * GRID_W
    t = k_ref.shape[1]
    kc = kc_ref[0]
    vc = vc_ref[0]
    lane = lax.broadcasted_iota(jnp.int32, (nq, LANES), 1)
    for blk, bias_ref in enumerate(bias_refs):
        i = step * len(bias_refs) + blk
        start = pl.multiple_of(jnp.clip(nq * i - (NA_WIN_H // 2) * GRID_W, 0, t - nb), nq)
        kb = k_ref[0, pl.ds(start, nb), :]
        vb = v_ref[0, pl.ds(start, nb), :]
        qb = q_ref[0, blk * nq:(blk + 1) * nq, :]
        outs = []
        for j in range(2):
            q = jnp.where((lane >= 64) == (j == 1), qb, jnp.zeros_like(qb))
            s_band = _dot_nt(q, kb) + bias_ref[0, j]
            s_ctx = _dot_nt(q, kc)
            m = jnp.maximum(jnp.max(s_band, axis=-1, keepdims=True),
                            jnp.max(s_ctx, axis=-1, keepdims=True))
            p_band = jnp.exp2(s_band - m)
            p_ctx = jnp.exp2(s_ctx - m)
            l = jnp.sum(p_band, axis=-1, keepdims=True) + jnp.sum(p_ctx, axis=-1, keepdims=True)
            acc = _dot(p_band.astype(BF16), vb) + _dot(p_ctx.astype(BF16), vc)
            outs.append(acc / l)
        o_ref[0, blk * nq:(blk + 1) * nq, :] = jnp.where(lane < 64, outs[0], outs[1]).astype(o_ref.dtype)


def _neighbourhood_attention(q, k, v, kc, vc, bias):
    b, t, w = q.shape
    pairs = w // LANES
    nq = NA_Q_ROWS * GRID_W
    nb = NA_BAND_ROWS * GRID_W
    nblk = t // nq
    per = NA_BLOCKS_PER_STEP
    c = kc.shape[1]
    cls = lambda i: jnp.minimum(i, 1) + jnp.maximum(i - (nblk - 2), 0)
    bias_specs = [pl.BlockSpec((1, 2, nq, nb), lambda bi, g, i, blk=blk: (cls(per * i + blk), g, 0, 0))
                  for blk in range(per)]
    return pl.pallas_call(
        _na_kernel,
        out_shape=jax.ShapeDtypeStruct((b, t, w), BF16),
        grid=(b, pairs, nblk // per),
        in_specs=[pl.BlockSpec((1, per * nq, LANES), lambda bi, g, i: (bi, i, g)),
                  pl.BlockSpec((1, t, LANES), lambda bi, g, i: (bi, 0, g)),
                  pl.BlockSpec((1, t, LANES), lambda bi, g, i: (bi, 0, g)),
                  pl.BlockSpec((1, c, LANES), lambda bi, g, i: (bi, 0, g)),
                  pl.BlockSpec((1, c, LANES), lambda bi, g, i: (bi, 0, g))] + bias_specs,
        out_specs=pl.BlockSpec((1, per * nq, LANES), lambda bi, g, i: (bi, i, g)),
        compiler_params=_params(40 << 20, 3),
        name="neighbourhood_attention",
    )(q, k, v, kc, vc, *([bias] * per))


def _out_ffn_kernel(*refs, n_attn, ff_chunk, final_norm):
    x_ref, mod_ref = refs[0], refs[1]
    a_refs = refs[2:2 + n_attn]
    w_refs = refs[2 + n_attn:2 + 2 * n_attn]
    gffn_ref, w_in_ref, w_out_ref, gfin_ref, o_ref = refs[2 + 2 * n_attn:]
    mod = mod_ref[0]
    mix = _dot(a_refs[0][...], w_refs[0][...])
    for a_ref, w_ref in zip(a_refs[1:], w_refs[1:]):
        mix = mix + _dot(a_ref[...], w_ref[...])
    x1 = x_ref[...] + mod[2:3] * mix
    h = (_rms(x1, gffn_ref[...]) * (1.0 + mod[4:5]) + mod[3:4]).astype(BF16)
    d_ff = w_out_ref.shape[0]
    acc = None
    for c in range(d_ff // ff_chunk):
        lo = c * ff_chunk
        gate = _dot(h, w_in_ref[:, lo:lo + ff_chunk])
        up = _dot(h, w_in_ref[:, d_ff + lo:d_ff + lo + ff_chunk])
        act = (gate * jax.nn.sigmoid(gate) * up).astype(BF16)
        part = _dot(act, w_out_ref[lo:lo + ff_chunk, :])
        acc = part if acc is None else acc + part
    x2 = x1 + mod[5:6] * acc
    if final_norm:
        x2 = _rms(x2, gfin_ref[...])
    o_ref[...] = x2


def _out_ffn(x, mod, tokens_per_group, attn, w_outs, gffn, w_ffn_in, w_ffn_out, gfin, final_norm):
    n = x.shape[0]
    tm = TOKEN_TILE
    tiles_per_group = tokens_per_group // tm
    row = lambda i: (i, 0)
    in_specs = [pl.BlockSpec((tm, D_MODEL), row),
                pl.BlockSpec((1, 6, D_MODEL), lambda i: (i // tiles_per_group, 0, 0))]
    in_specs += [pl.BlockSpec((tm, a.shape[1]), row) for a in attn]
    in_specs += [_const_spec(w.shape) for w in w_outs]
    in_specs += [_const_spec(gffn.shape), _const_spec(w_ffn_in.shape), _const_spec(w_ffn_out.shape),
                 _const_spec(gfin.shape)]
    return pl.pallas_call(
        functools.partial(_out_ffn_kernel, n_attn=len(attn), ff_chunk=256, final_norm=final_norm),
        out_shape=jax.ShapeDtypeStruct((n, D_MODEL), F32),
        grid=(n // tm,),
        in_specs=in_specs,
        out_specs=pl.BlockSpec((tm, D_MODEL), row),
        compiler_params=_params(56 << 20, 1),
        name="out_ffn",
    )(x, mod, *attn, *w_outs, gffn, w_ffn_in, w_ffn_out, gfin)


def _rope_tables(n_tokens, rot_dim):
    t = np.arange(n_tokens)
    row = (t // GRID_W).astype(np.float32)
    col = (t % GRID_W).astype(np.float32)
    axis_dim = rot_dim // 2
    inv_freq = np.float32(ROPE_THETA) ** (-np.arange(0, axis_dim, 2, dtype=np.float32) / axis_dim)
    ang = np.concatenate([row[:, None] * inv_freq, col[:, None] * inv_freq], axis=-1).astype(np.float32)
    return np.cos(ang), np.sin(ang)


def _mla_rope_lanes(n_tokens):
    cos, sin = _rope_tables(n_tokens, MLA_ROPE_DIM)
    one = np.ones((n_tokens, MLA_NOPE_DIM), np.float32)
    zero = np.zeros((n_tokens, MLA_NOPE_DIM), np.float32)
    pad1 = np.ones((n_tokens, LANES - MLA_QK_DIM), np.float32)
    pad0 = np.zeros((n_tokens, LANES - MLA_QK_DIM), np.float32)
    return (jnp.asarray(np.concatenate([one, cos, cos, pad1], axis=-1)),
            jnp.asarray(np.concatenate([zero, -sin, sin, pad0], axis=-1)))


def _gqa_rope_lanes(n_tokens):
    cos, sin = _rope_tables(n_tokens, GQA_HEAD_DIM)
    return (jnp.asarray(np.concatenate([cos, cos], axis=-1)),
            jnp.asarray(np.concatenate([-sin, sin], axis=-1)))


def _swap_halves(w):
    half = w.shape[-1] // 2
    return jnp.concatenate([w[..., half:], w[..., :half]], axis=-1)


def _even_weights(w_in, w_uq, w_ukv):
    d = w_in.shape[0]
    i0 = MLA_Q_LORA
    i1 = i0 + MLA_KV_LORA
    i2 = i1 + MLA_ROPE_DIM
    w_kr = w_in[:, i1:i2]
    zl = jnp.zeros((d, MLA_NOPE_DIM), F32)
    zr = jnp.zeros((d, LANES - MLA_QK_DIM), F32)
    w_in_k = jnp.concatenate([w_in[:, :i1], w_in[:, i2:], zl, w_kr, zr, zl, _swap_halves(w_kr), zr],
                             axis=-1).astype(BF16)
    r = w_uq.shape[0]
    uq = w_uq.reshape(r, MLA_HEADS, MLA_QK_DIM)
    zpad = jnp.zeros((r, MLA_HEADS, LANES - MLA_QK_DIM), F32)
    znope = jnp.zeros((r, MLA_HEADS, MLA_NOPE_DIM), F32)
    q_plain = jnp.concatenate([uq, zpad], axis=-1).reshape(r, MLA_HEADS * LANES)
    q_swap = jnp.concatenate([znope, _swap_halves(uq[..., MLA_NOPE_DIM:]), zpad], axis=-1)
    w_uq_k = jnp.concatenate([q_plain, q_swap.reshape(r, MLA_HEADS * LANES)], axis=-1).astype(BF16)
    r = w_ukv.shape[0]
    ukv = w_ukv.reshape(r, MLA_HEADS, MLA_NOPE_DIM + MLA_V_DIM)
    k_pad = jnp.concatenate([ukv[..., :MLA_NOPE_DIM], jnp.zeros((r, MLA_HEADS, LANES - MLA_NOPE_DIM), F32)],
                            axis=-1).reshape(r, MLA_HEADS * LANES)
    v_cat = ukv[..., MLA_NOPE_DIM:].reshape(r, MLA_HEADS * MLA_V_DIM)
    w_ukv_k = jnp.concatenate([k_pad, v_cat], axis=-1).astype(BF16)
    return w_in_k, w_uq_k, w_ukv_k


def kernel(x_prompt, x_sample, cache_mla_ckv, cache_mla_krope, cache_na_k, cache_na_v, cache_gqa_k, cache_gqa_v, c, c_ctx, w_mod, b_mod, norm_mix, norm_ffn, norm_final, w_in_a, mla_q_norm, mla_w_uq, mla_kv_norm, mla_w_ukv, na_rpb, w_out_a, w_in_c, gqa_q_norm, gqa_k_norm, w_out_c, w_ffn_in, w_ffn_out):
    batch, seq, d = x_prompt.shape
    dec_batch, dec_seq, _ = x_sample.shape
    depth = w_mod.shape[0]
    past = cache_mla_ckv.shape[2]
    n_ctx = batch * seq
    n_lat = dec_batch * dec_seq

    cond = jnp.concatenate([c_ctx[None], c, jnp.zeros((8 - 1 - dec_batch, d), F32)], axis=0)
    mod = _modulation(cond, w_mod, b_mod).reshape(depth, 8, 6, d)

    xp = x_prompt.reshape(n_ctx, d)
    xs = x_sample.reshape(n_lat, d)
    cos_m, sin_m = _mla_rope_lanes(dec_seq)
    cos_g, sin_g = _gqa_rope_lanes(dec_seq)
    ident_cos = jnp.ones((TOKEN_TILE, LANES), F32)
    ident_sin = jnp.zeros((TOKEN_TILE, LANES), F32)
    gfin = norm_final.reshape(1, d)
    states = {k: [] for k in ("ckv", "krope", "nk", "nv", "gk", "gv")}

    for l in range(depth):
        mod_p = mod[l, 0:1]
        mod_s = mod[l, 1:1 + dec_batch]
        gmix = norm_mix[l].reshape(1, d)
        gffn = norm_ffn[l].reshape(1, d)
        if l % 2 == 0:
            e = l // 2
            w_in_k, w_uq_k, w_ukv_k = _even_weights(w_in_a[e], mla_w_uq[e], mla_w_ukv[e])
            qn = (mla_q_norm[e] * MLA_SCALE).reshape(1, -1)
            kvn = mla_kv_norm[e].reshape(1, -1)
            (qp, kp, vp, nqp, nkp, nvp, s_ckv, s_kr, s_nk, s_nv) = _even_in(
                xp, mod_p, n_ctx, gmix, w_in_k, qn, kvn, w_uq_k, w_ukv_k, ident_cos, ident_sin, True)
            states["ckv"].append(s_ckv.reshape(batch, seq, MLA_KV_LORA))
            states["krope"].append(s_kr[:, MLA_NOPE_DIM:MLA_QK_DIM].reshape(batch, seq, MLA_ROPE_DIM))
            states["nk"].append(s_nk.reshape(batch, seq, NA_HEADS, NA_HEAD_DIM))
            states["nv"].append(s_nv.reshape(batch, seq, NA_HEADS, NA_HEAD_DIM))
            qs, ks, vs, nqs, nks, nvs = _even_in(
                xs, mod_s, dec_seq, gmix, w_in_k, qn, kvn, w_uq_k, w_ukv_k, cos_m, sin_m, False)
            kr_cache = jnp.pad(cache_mla_krope[:, e],
                               ((0, 0), (0, 0), (MLA_NOPE_DIM, LANES - MLA_QK_DIM)))
            kc, vc = _cache_expand(cache_mla_ckv[:, e], kr_cache, w_ukv_k)

            r3 = lambda a, b_: a.reshape(b_, a.shape[0] // b_, a.shape[1])
            mla_kw = dict(groups=MLA_HEADS // 2, heads=2, k_stride=LANES, q_half_mask=False, pair_out=True)
            na_kw = dict(groups=NA_HEADS // 2, heads=2, k_stride=0, q_half_mask=True, pair_out=True)
            a_mla_p = _ctx_attention(r3(qp, batch), r3(kp, batch), r3(vp, batch), name="mla_ctx", **mla_kw)
            a_na_p = _ctx_attention(r3(nqp, batch), r3(nkp, batch), r3(nvp, batch), name="na_ctx", **na_kw)
            a_mla_s = _attention(r3(qs, dec_batch), [(r3(ks, dec_batch), r3(vs, dec_batch)), (kc, vc)],
                                 name="mla_lat", **mla_kw)
            bias = _na_bias_tables(na_rpb[e])
            a_na_s = _neighbourhood_attention(
                r3(nqs, dec_batch), r3(nks, dec_batch), r3(nvs, dec_batch),
                cache_na_k[:, e].reshape(dec_batch, past, NA_W).astype(BF16),
                cache_na_v[:, e].reshape(dec_batch, past, NA_W).astype(BF16), bias)
            attn_p = [a_mla_p.reshape(n_ctx, -1), a_na_p.reshape(n_ctx, -1)]
            attn_s = [a_mla_s.reshape(n_lat, -1), a_na_s.reshape(n_lat, -1)]
            wo = w_out_a[e].astype(BF16)
            half = MLA_HEADS * MLA_V_DIM
            w_outs = [wo[:half], wo[half:]]
        else:
            o = l // 2
            w_in_k = w_in_c[o].astype(BF16)
            qn = (gqa_q_norm[o] * GQA_SCALE).reshape(1, -1)
            kn = gqa_k_norm[o].reshape(1, -1)
            qp, kp, vp, s_gk, s_gv = _odd_in(xp, mod_p, n_ctx, gmix, w_in_k, qn, kn, ident_cos, ident_sin, True)
            states["gk"].append(s_gk.reshape(batch, seq, GQA_KV_HEADS, GQA_HEAD_DIM))
            states["gv"].append(s_gv.reshape(batch, seq, GQA_KV_HEADS, GQA_HEAD_DIM))
            qs, ks, vs = _odd_in(xs, mod_s, dec_seq, gmix, w_in_k, qn, kn, cos_g, sin_g, False)
            r3 = lambda a, b_: a.reshape(b_, a.shape[0] // b_, a.shape[1])
            gqa_kw = dict(groups=GQA_KV_HEADS, heads=GQA_GROUP, k_stride=0, q_half_mask=False, pair_out=False)
            a_p = _ctx_attention(r3(qp, batch), r3(kp, batch), r3(vp, batch), name="gqa_ctx", **gqa_kw)
            kcache = cache_gqa_k[:, o].reshape(dec_batch, past, -1).astype(BF16)
            vcache = cache_gqa_v[:, o].reshape(dec_batch, past, -1).astype(BF16)
            a_s = _attention(r3(qs, dec_batch), [(r3(ks, dec_batch), r3(vs, dec_batch)), (kcache, vcache)],
                             name="gqa_lat", **gqa_kw)
            attn_p = [a_p.reshape(n_ctx, -1)]
            attn_s = [a_s.reshape(n_lat, -1)]
            w_outs = [w_out_c[o].astype(BF16)]
        last = l == depth - 1
        wfi = w_ffn_in[l].astype(BF16)
        wfo = w_ffn_out[l].astype(BF16)
        xp = _out_ffn(xp, mod_p, n_ctx, attn_p, w_outs, gffn, wfi, wfo, gfin, last)
        xs = _out_ffn(xs, mod_s, dec_seq, attn_s, w_outs, gffn, wfi, wfo, gfin, last)

    y_prompt = xp.reshape(batch, seq, d)
    y_sample = xs.reshape(dec_batch, dec_seq, d)
    return (y_prompt, y_sample,
            jnp.stack(states["ckv"], axis=1), jnp.stack(states["krope"], axis=1),
            jnp.stack(states["nk"], axis=1), jnp.stack(states["nv"], axis=1),
            jnp.stack(states["gk"], axis=1), jnp.stack(states["gv"], axis=1))
```

```python
import functools
import math

import numpy as np
import jax
import jax.numpy as jnp
from jax import lax
from jax.experimental import pallas as pl
from jax.experimental.pallas import tpu as pltpu

LANES = 128
V7X_VMEM_BYTES = 64 * 1024 * 1024

D_MODEL = 1024
GRID_W = 64
ROPE_THETA = 10000.0
RMS_EPS = 1e-6
NEG_INF = -1e30
MLA_HEADS = 8
MLA_Q_LORA = 256
MLA_KV_LORA = 256
MLA_NOPE_DIM = 64
MLA_ROPE_DIM = 32
MLA_V_DIM = 64
MLA_QK_DIM = MLA_NOPE_DIM + MLA_ROPE_DIM
LOG2E = math.log2(math.e)
MLA_SCALE = MLA_QK_DIM ** -0.5 * LOG2E
NA_HEADS = 8
NA_HEAD_DIM = 64
NA_WIN_H = 8
NA_WIN_W = 16
NA_SCALE = NA_HEAD_DIM ** -0.5 * LOG2E
NA_W = NA_HEADS * NA_HEAD_DIM
GQA_HEADS = 8
GQA_KV_HEADS = 2
GQA_HEAD_DIM = 128
GQA_SCALE = GQA_HEAD_DIM ** -0.5 * LOG2E
GQA_GROUP = GQA_HEADS // GQA_KV_HEADS

TOKEN_TILE = 512
ATTN_Q_TILE = 1024
ATTN_K_CHUNK = 512
NA_Q_ROWS = 4
NA_BAND_ROWS = 12
NA_BLOCKS_PER_STEP = 4

BF16 = jnp.bfloat16
F32 = jnp.float32


def _vmem_limit(nbytes):
    return int(min(V7X_VMEM_BYTES - (4 << 20), max(nbytes, 16 << 20)))


def _params(nbytes, ndims):
    return pltpu.CompilerParams(dimension_semantics=("arbitrary",) * ndims,
                                vmem_limit_bytes=_vmem_limit(nbytes))


def _rms(x, gain):
    return x * lax.rsqrt(jnp.mean(x * x, axis=-1, keepdims=True) + RMS_EPS) * gain


def _dot(a, b):
    return jnp.dot(a, b, preferred_element_type=F32)


def _dot_nt(a, b):
    return lax.dot_general(a, b, (((1,), (1,)), ((), ())), preferred_element_type=F32)


def _const_spec(shape):
    nd = len(shape)
    return pl.BlockSpec(shape, lambda *_: (0,) * nd, pipeline_mode=pl.Buffered(1))


def _mod_kernel(cond_ref, w_ref, b_ref, o_ref):
    c = cond_ref[...]
    s = (c * jax.nn.sigmoid(c)).astype(BF16)
    o_ref[0] = _dot(s, w_ref[0].astype(BF16)) + b_ref[0]


def _modulation(cond, w_mod, b_mod):
    depth, d, n = w_mod.shape
    rows = cond.shape[0]
    bn = 1024
    return pl.pallas_call(
        _mod_kernel,
        out_shape=jax.ShapeDtypeStruct((depth, rows, n), F32),
        grid=(depth, n // bn),
        in_specs=[pl.BlockSpec((rows, d), lambda l, j: (0, 0)),
                  pl.BlockSpec((1, d, bn), lambda l, j: (l, 0, j)),
                  pl.BlockSpec((1, 1, bn), lambda l, j: (l, 0, j))],
        out_specs=pl.BlockSpec((1, rows, bn), lambda l, j: (l, 0, j)),
        compiler_params=_params(3 * d * bn * 4, 2),
        name="ada_modulation",
    )(cond, w_mod, b_mod.reshape(depth, 1, n))


def _even_in_kernel(x_ref, mod_ref, gmix_ref, w_in_ref, qn_ref, kvn_ref, w_uq_ref, w_ukv_ref,
                    cos_ref, sin_ref, *out_refs, with_state):
    q_ref, k_ref, v_ref, nq_ref, nk_ref, nv_ref = out_refs[:6]
    x = x_ref[...]
    mod = mod_ref[0]
    h = _rms(x, gmix_ref[...]) * (1.0 + mod[1:2]) + mod[0:1]
    p = _dot(h.astype(BF16), w_in_ref[...])
    cq = p[:, 0:256]
    ckv = _rms(p[:, 256:512], kvn_ref[...])
    nq = p[:, 512:1024]
    nk = p[:, 1024:1536]
    nv = p[:, 1536:2048]
    kr = p[:, 2048:2176]
    kr_sw = p[:, 2176:2304]
    cos = cos_ref[...]
    sin = sin_ref[...]
    qq = _dot(_rms(cq, qn_ref[...]).astype(BF16), w_uq_ref[...])
    kv = _dot(ckv.astype(BF16), w_ukv_ref[...])
    kr_rot = kr * cos + kr_sw * sin
    for hd in range(MLA_HEADS):
        lo = hd * LANES
        qh = qq[:, lo:lo + LANES] * cos + qq[:, 1024 + lo:1024 + lo + LANES] * sin
        q_ref[:, lo:lo + LANES] = qh.astype(BF16)
        k_ref[:, lo:lo + LANES] = (kv[:, lo:lo + LANES] + kr_rot).astype(BF16)
    v_ref[...] = kv[:, 1024:1536].astype(BF16)
    nq_ref[...] = (nq * NA_SCALE).astype(BF16)
    nk_ref[...] = nk.astype(BF16)
    nv_ref[...] = nv.astype(BF16)
    if with_state:
        s_ckv_ref, s_kr_ref, s_nk_ref, s_nv_ref = out_refs[6:]
        s_ckv_ref[...] = ckv
        s_kr_ref[...] = kr
        s_nk_ref[...] = nk
        s_nv_ref[...] = nv


def _even_in(x, mod, tokens_per_group, gmix, w_in, qn, kvn, w_uq, w_ukv, cos, sin, with_state):
    n = x.shape[0]
    tm = TOKEN_TILE
    tiles_per_group = tokens_per_group // tm
    rope_tiles = cos.shape[0] // tm
    row = lambda i: (i, 0)
    outs = [jax.ShapeDtypeStruct((n, 1024), BF16), jax.ShapeDtypeStruct((n, 1024), BF16),
            jax.ShapeDtypeStruct((n, 512), BF16), jax.ShapeDtypeStruct((n, 512), BF16),
            jax.ShapeDtypeStruct((n, 512), BF16), jax.ShapeDtypeStruct((n, 512), BF16)]
    if with_state:
        outs += [jax.ShapeDtypeStruct((n, 256), F32), jax.ShapeDtypeStruct((n, 128), F32),
                 jax.ShapeDtypeStruct((n, 512), F32), jax.ShapeDtypeStruct((n, 512), F32)]
    return pl.pallas_call(
        functools.partial(_even_in_kernel, with_state=with_state),
        out_shape=outs,
        grid=(n // tm,),
        in_specs=[pl.BlockSpec((tm, D_MODEL), row),
                  pl.BlockSpec((1, 6, D_MODEL), lambda i: (i // tiles_per_group, 0, 0)),
                  _const_spec(gmix.shape), _const_spec(w_in.shape), _const_spec(qn.shape),
                  _const_spec(kvn.shape), _const_spec(w_uq.shape), _const_spec(w_ukv.shape),
                  pl.BlockSpec((tm, LANES), lambda i: (i % rope_tiles, 0)),
                  pl.BlockSpec((tm, LANES), lambda i: (i % rope_tiles, 0))],
        out_specs=[pl.BlockSpec((tm, o.shape[1]), row) for o in outs],
        compiler_params=_params(40 << 20, 1),
        name="even_in",
    )(x, mod, gmix, w_in, qn, kvn, w_uq, w_ukv, cos, sin)


def _cache_expand_kernel(ckv_ref, kr_ref, w_ukv_ref, k_ref, v_ref):
    kv = _dot(ckv_ref[0].astype(BF16), w_ukv_ref[...])
    kr = kr_ref[0]
    for hd in range(MLA_HEADS):
        lo = hd * LANES
        k_ref[0, :, lo:lo + LANES] = (kv[:, lo:lo + LANES] + kr).astype(BF16)
    v_ref[0] = kv[:, 1024:1536].astype(BF16)


def _cache_expand(ckv, kr128, w_ukv):
    b, s, _ = ckv.shape
    return pl.pallas_call(
        _cache_expand_kernel,
        out_shape=[jax.ShapeDtypeStruct((b, s, 1024), BF16), jax.ShapeDtypeStruct((b, s, 512), BF16)],
        grid=(b,),
        in_specs=[pl.BlockSpec((1, s, MLA_KV_LORA), lambda i: (i, 0, 0)),
                  pl.BlockSpec((1, s, LANES), lambda i: (i, 0, 0)),
                  _const_spec(w_ukv.shape)],
        out_specs=[pl.BlockSpec((1, s, 1024), lambda i: (i, 0, 0)),
                   pl.BlockSpec((1, s, 512), lambda i: (i, 0, 0))],
        compiler_params=_params(16 << 20, 1),
        name="mla_cache_expand",
    )(ckv, kr128, w_ukv)


def _odd_in_kernel(x_ref, mod_ref, gmix_ref, w_in_ref, qn_ref, kn_ref, cos_ref, sin_ref, *out_refs,
                   with_state):
    q_ref, k_ref, v_ref = out_refs[:3]
    x = x_ref[...]
    mod = mod_ref[0]
    h = (_rms(x, gmix_ref[...]) * (1.0 + mod[1:2]) + mod[0:1]).astype(BF16)
    cos = cos_ref[...]
    sin = sin_ref[...]
    half = GQA_HEAD_DIM // 2

    def rope(t):
        return t * cos + pltpu.roll(t, half, 1) * sin

    pair = 2 * LANES
    for g in range(GQA_HEADS // 2):
        p = _dot(h, w_in_ref[:, g * pair:(g + 1) * pair])
        for i in range(2):
            lo = g * pair + i * LANES
            qh = rope(_rms(p[:, i * LANES:(i + 1) * LANES], qn_ref[...]))
            q_ref[:, lo:lo + LANES] = qh.astype(BF16)
    k_off = GQA_HEADS * LANES
    v_off = k_off + GQA_KV_HEADS * LANES
    pk = _dot(h, w_in_ref[:, k_off:v_off])
    pv = _dot(h, w_in_ref[:, v_off:v_off + GQA_KV_HEADS * LANES])
    for hd in range(GQA_KV_HEADS):
        lo = hd * LANES
        kh = _rms(pk[:, lo:lo + LANES], kn_ref[...])
        vh = pv[:, lo:lo + LANES]
        k_ref[:, lo:lo + LANES] = rope(kh).astype(BF16)
        v_ref[:, lo:lo + LANES] = vh.astype(BF16)
        if with_state:
            out_refs[3][:, lo:lo + LANES] = kh
            out_refs[4][:, lo:lo + LANES] = vh


def _odd_in(x, mod, tokens_per_group, gmix, w_in, qn, kn, cos, sin, with_state):
    n = x.shape[0]
    tm = TOKEN_TILE
    tiles_per_group = tokens_per_group // tm
    rope_tiles = cos.shape[0] // tm
    row = lambda i: (i, 0)
    outs = [jax.ShapeDtypeStruct((n, 1024), BF16), jax.ShapeDtypeStruct((n, 256), BF16),
            jax.ShapeDtypeStruct((n, 256), BF16)]
    if with_state:
        outs += [jax.ShapeDtypeStruct((n, 256), F32), jax.ShapeDtypeStruct((n, 256), F32)]
    return pl.pallas_call(
        functools.partial(_odd_in_kernel, with_state=with_state),
        out_shape=outs,
        grid=(n // tm,),
        in_specs=[pl.BlockSpec((tm, D_MODEL), row),
                  pl.BlockSpec((1, 6, D_MODEL), lambda i: (i // tiles_per_group, 0, 0)),
                  _const_spec(gmix.shape), _const_spec(w_in.shape), _const_spec(qn.shape),
                  _const_spec(kn.shape),
                  pl.BlockSpec((tm, LANES), lambda i: (i % rope_tiles, 0)),
                  pl.BlockSpec((tm, LANES), lambda i: (i % rope_tiles, 0))],
        out_specs=[pl.BlockSpec((tm, o.shape[1]), row) for o in outs],
        compiler_params=_params(32 << 20, 1),
        name="odd_in",
    )(x, mod, gmix, w_in, qn, kn, cos, sin)


def _attn_kernel(*refs, n_src, heads, q_tile, src_len, k_stride, q_half_mask, pair_out):
    q_ref = refs[0]
    kv_refs = refs[1:1 + 2 * n_src]
    o_ref = refs[1 + 2 * n_src]
    s_ref = refs[2 + 2 * n_src]
    n_tiles = q_ref.shape[1] // q_tile
    chunks = []
    for src in range(n_src):
        ck = min(ATTN_K_CHUNK, src_len[src])
        for c in range(src_len[src] // ck):
            chunks.append((src, c * ck, ck))
    lane = lax.broadcasted_iota(jnp.int32, (q_tile, LANES), 1)

    def rows(t):
        if isinstance(t, int):
            return slice(t * q_tile, (t + 1) * q_tile)
        return pl.ds(pl.multiple_of(t * q_tile, q_tile), q_tile)

    def load_q(t, j):
        if q_half_mask:
            qb = q_ref[0, rows(t), :]
            return jnp.where((lane >= 64) == (j == 1), qb, jnp.zeros_like(qb))
        return q_ref[0, rows(t), j * LANES:(j + 1) * LANES]

    def slot(j, q_next, m_prev):
        if q_next is not None:
            m_part = jnp.full((q_tile, LANES), -jnp.inf, F32)
        if m_prev is not None:
            l_part = jnp.zeros((q_tile, LANES), F32)
            acc = jnp.zeros((q_tile, LANES), F32)
        off = 0
        for src, k0, ck in chunks:
            if m_prev is not None:
                p = jnp.exp2(s_ref[:, off:off + ck] - m_prev)
                for i in range(ck // LANES):
                    l_part = l_part + p[:, i * LANES:(i + 1) * LANES]
                acc = acc + _dot(p.astype(BF16), kv_refs[2 * src + 1][0, k0:k0 + ck, :])
            if q_next is not None:
                kc = kv_refs[2 * src][0, j * k_stride:j * k_stride + LANES, k0:k0 + ck]
                s = _dot(q_next, kc)
                s_ref[:, off:off + ck] = s
                for i in range(ck // LANES):
                    m_part = jnp.maximum(m_part, s[:, i * LANES:(i + 1) * LANES])
            off += ck
        m_next = None if q_next is None else jnp.max(m_part, axis=-1, keepdims=True)
        o_prev = None if m_prev is None else acc / jnp.sum(l_part, axis=-1, keepdims=True)
        return m_next, o_prev

    def write_out(t, j, o):
        o = o.astype(o_ref.dtype)
        if not pair_out:
            o_ref[0, rows(t), j * LANES:(j + 1) * LANES] = o
        elif j == 0:
            o_ref[0, rows(t), :] = o
        else:
            o_ref[0, rows(t), :] = jnp.where(lane < 64, o_ref[0, rows(t), :], o)

    def tile(j, t, m, last):
        m_next, o = slot(j, None if last else load_q(t + 1, j), m)
        write_out(t, j, o)
        return m_next

    for j in range(heads):
        m, _ = slot(j, load_q(0, j), None)
        if n_tiles > 1:
            m = lax.fori_loop(0, n_tiles - 1, lambda t, m, j=j: tile(j, t, m, False), m)
        tile(j, n_tiles - 1, m, True)


def _attention(q, sources, *, groups, heads, k_stride, q_half_mask, pair_out, name):
    b, t, _ = q.shape
    q_tile = min(ATTN_Q_TILE, t)
    q_block = LANES if q_half_mask else heads * LANES
    k_block = LANES if k_stride == 0 else heads * LANES
    out_block = LANES if pair_out else heads * LANES
    src_len = tuple(k.shape[1] for k, _ in sources)
    in_specs = [pl.BlockSpec((1, t, q_block), lambda bi, g: (bi, 0, g))]
    args = [q]
    for k, v in sources:
        s = k.shape[1]
        in_specs.append(pl.BlockSpec((1, k_block, s), lambda bi, g: (bi, g, 0)))
        in_specs.append(pl.BlockSpec((1, s, LANES), lambda bi, g: (bi, 0, g)))
        args += [jnp.swapaxes(k, 1, 2), v]
    total = sum(src_len)
    return pl.pallas_call(
        functools.partial(_attn_kernel, n_src=len(sources), heads=heads, q_tile=q_tile,
                          src_len=src_len, k_stride=k_stride, q_half_mask=q_half_mask,
                          pair_out=pair_out),
        out_shape=jax.ShapeDtypeStruct((b, t, groups * out_block), BF16),
        grid=(b, groups),
        in_specs=in_specs,
        out_specs=pl.BlockSpec((1, t, out_block), lambda bi, g: (bi, 0, g)),
        scratch_shapes=[pltpu.VMEM((q_tile, total), F32)],
        compiler_params=_params(58 << 20, 2),
        name=name,
    )(*args)


def _ctx_attn_kernel(q_ref, k_ref, v_ref, o_ref, *, heads, k_stride, q_half_mask, pair_out):
    nb, t, _ = q_ref.shape
    lane = lax.broadcasted_iota(jnp.int32, (t, LANES), 1)
    for b in range(nb):
        outs = []
        for j in range(heads):
            if q_half_mask:
                qb = q_ref[b]
                q = jnp.where((lane >= 64) == (j == 1), qb, jnp.zeros_like(qb))
            else:
                q = q_ref[b, :, j * LANES:(j + 1) * LANES]
            s = _dot_nt(q, k_ref[b, :, j * k_stride:j * k_stride + LANES])
            p = jnp.exp2(s - jnp.max(s, axis=-1, keepdims=True))
            acc = _dot(p.astype(BF16), v_ref[b])
            outs.append(acc / jnp.sum(p, axis=-1, keepdims=True))
        if pair_out:
            o_ref[b] = jnp.where(lane < 64, outs[0], outs[1]).astype(o_ref.dtype)
        else:
            for j in range(heads):
                o_ref[b, :, j * LANES:(j + 1) * LANES] = outs[j].astype(o_ref.dtype)


def _ctx_attention(q, k, v, *, groups, heads, k_stride, q_half_mask, pair_out, name):
    b, t, _ = q.shape
    nb = 4
    q_block = LANES if q_half_mask else heads * LANES
    k_block = LANES if k_stride == 0 else heads * LANES
    out_block = LANES if pair_out else heads * LANES
    spec = lambda w: pl.BlockSpec((nb, t, w), lambda bi, g: (bi, 0, g))
    return pl.pallas_call(
        functools.partial(_ctx_attn_kernel, heads=heads, k_stride=k_stride, q_half_mask=q_half_mask,
                          pair_out=pair_out),
        out_shape=jax.ShapeDtypeStruct((b, t, groups * out_block), BF16),
        grid=(b // nb, groups),
        in_specs=[spec(q_block), spec(k_block), spec(LANES)],
        out_specs=spec(out_block),
        compiler_params=_params(32 << 20, 2),
        name=name,
    )(q, k, v)


def _na_bias_tables(rpb):
    n_rows = GRID_W
    h, n_dr, n_dc = rpb.shape
    edge = n_dc - 1 - (NA_WIN_W - 1)
    w = jnp.concatenate([rpb[..., NA_WIN_W - 1:],
                         jnp.broadcast_to(rpb[..., n_dc - 1:], (h, n_dr, GRID_W - 1 - edge)),
                         jnp.broadcast_to(rpb[..., :1], (h, n_dr, GRID_W - (NA_WIN_W - 1) + 1)),
                         rpb[..., 1:NA_WIN_W - 1]], axis=-1)
    toe = jnp.tile(w, (1, 1, GRID_W))[..., :GRID_W * (2 * GRID_W - 1)]
    toe = toe.reshape(h, n_dr, GRID_W, 2 * GRID_W - 1)[..., :GRID_W]
    cols = np.arange(GRID_W)
    cs = np.clip(cols - NA_WIN_W // 2, 0, GRID_W - NA_WIN_W)
    col_ok = (cols[None, :] >= cs[:, None]) & (cols[None, :] < cs[:, None] + NA_WIN_W)
    toe = jnp.where(col_ok, toe * LOG2E, NEG_INF)
    toe = toe.transpose(0, 2, 1, 3).reshape(h, GRID_W, n_dr * GRID_W)
    pieces = []
    for blk in (0, 1, n_rows // NA_Q_ROWS - 1):
        b0 = int(np.clip(NA_Q_ROWS * blk - NA_WIN_H // 2, 0, n_rows - NA_BAND_ROWS))
        for qr in range(NA_Q_ROWS):
            r = NA_Q_ROWS * blk + qr
            rs = int(np.clip(r - NA_WIN_H // 2, 0, n_rows - NA_WIN_H))
            dr0 = rs - r + NA_WIN_H - 1
            seen = toe[:, None, :, dr0 * GRID_W:(dr0 + NA_WIN_H) * GRID_W]
            before = (rs - b0) * GRID_W
            after = (NA_BAND_ROWS - NA_WIN_H) * GRID_W - before
            pieces.append(jnp.pad(seen, ((0, 0), (0, 0), (0, 0), (before, after)), constant_values=NEG_INF))
    return jnp.concatenate(pieces, axis=1).reshape(h, 3, NA_Q_ROWS * GRID_W, NA_BAND_ROWS * GRID_W)


def _na_kernel(q_ref, k_ref, v_ref, kc_ref, vc_ref, bias_ref, o_ref, s_ref):
    nq = NA_Q_ROWS * GRID_W
    nb = NA_BAND_ROWS * GRID_W
    per = NA_BLOCKS_PER_STEP
    t = k_ref.shape[1]
    n_blocks = t // nq
    n_groups = n_blocks // per
    lane = lax.broadcasted_iota(jnp.int32, (nq, LANES), 1)

    def block_rows(i):
        if isinstance(i, int):
            return slice(i * nq, (i + 1) * nq)
        return pl.ds(pl.multiple_of(i * nq, nq), nq)

    def band_rows(i):
        first = nq * i - (NA_WIN_H // 2) * GRID_W
        if isinstance(i, int):
            first = min(max(first, 0), t - nb)
            return slice(first, first + nb)
        return pl.ds(pl.multiple_of(jnp.clip(first, 0, t - nb), nq), nb)

    def bias_class(i):
        if isinstance(i, int):
            return min(i, 1) + max(i - (n_blocks - 2), 0)
        return jnp.minimum(i, 1) + jnp.maximum(i - (n_blocks - 2), 0)

    def slot(j, g_next, g_prev, ms_prev):
        new_ms = []
        for blk in range(per):
            srow = slice(blk * nq, (blk + 1) * nq)
            if g_prev is not None:
                i = g_prev * per + blk
                m = ms_prev[blk]
                p_band = jnp.exp2(s_ref[srow, :nb] - m)
                p_ctx = jnp.exp2(s_ref[srow, nb:] - m)
                l = jnp.sum(p_band, axis=-1, keepdims=True) + jnp.sum(p_ctx, axis=-1, keepdims=True)
                acc = (_dot(p_band.astype(BF16), v_ref[0, band_rows(i), :])
                       + _dot(p_ctx.astype(BF16), vc_ref[0]))
                o = (acc / l).astype(o_ref.dtype)
                if j == 0:
                    o_ref[0, block_rows(i), :] = o
                else:
                    o_ref[0, block_rows(i), :] = jnp.where(lane < 64, o_ref[0, block_rows(i), :], o)
            if g_next is not None:
                i = g_next * per + blk
                qb = q_ref[0, block_rows(i), :]
                q = jnp.where((lane >= 64) == (j == 1), qb, jnp.zeros_like(qb))
                s_band = _dot_nt(q, k_ref[0, band_rows(i), :]) + bias_ref[j, bias_class(i)]
                s_ctx = _dot_nt(q, kc_ref[0])
                s_ref[srow, :nb] = s_band
                s_ref[srow, nb:] = s_ctx
                new_ms.append(jnp.maximum(jnp.max(s_band, axis=-1, keepdims=True),
                                          jnp.max(s_ctx, axis=-1, keepdims=True)))
        return tuple(new_ms)

    for j in range(2):
        ms = slot(j, 0, None, None)
        if n_groups > 1:
            ms = lax.fori_loop(0, n_groups - 1, lambda g, ms, j=j: slot(j, g + 1, g, ms), ms)
        slot(j, None, n_groups - 1, ms)


def _neighbourhood_attention(q, k, v, kc, vc, bias):
    b, t, w = q.shape
    pairs = w // LANES
    nq = NA_Q_ROWS * GRID_W
    nb = NA_BAND_ROWS * GRID_W
    c = kc.shape[1]
    seq = lambda n: pl.BlockSpec((1, n, LANES), lambda bi, g: (bi, 0, g))
    return pl.pallas_call(
        _na_kernel,
        out_shape=jax.ShapeDtypeStruct((b, t, w), BF16),
        grid=(b, pairs),
        in_specs=[seq(t), seq(t), seq(t), seq(c), seq(c),
                  pl.BlockSpec((2, 3, nq, nb), lambda bi, g: (g, 0, 0, 0))],
        out_specs=seq(t),
        scratch_shapes=[pltpu.VMEM((NA_BLOCKS_PER_STEP * nq, nb + c), F32)],
        compiler_params=_params(40 << 20, 2),
        name="neighbourhood_attention",
    )(q, k, v, kc, vc, bias)


def _out_ffn_kernel(*refs, n_attn, ff_chunk, final_norm):
    x_ref, mod_ref = refs[0], refs[1]
    a_refs = refs[2:2 + n_attn]
    w_refs = refs[2 + n_attn:2 + 2 * n_attn]
    gffn_ref, w_in_ref, w_out_ref, gfin_ref, o_ref = refs[2 + 2 * n_attn:]
    mod = mod_ref[0]
    mix = _dot(a_refs[0][...], w_refs[0][...])
    for a_ref, w_ref in zip(a_refs[1:], w_refs[1:]):
        mix = mix + _dot(a_ref[...], w_ref[...])
    x1 = x_ref[...] + mod[2:3] * mix
    h = (_rms(x1, gffn_ref[...]) * (1.0 + mod[4:5]) + mod[3:4]).astype(BF16)
    d_ff = w_out_ref.shape[0]
    acc = None
    for c in range(d_ff // ff_chunk):
        lo = c * ff_chunk
        gate = _dot(h, w_in_ref[:, lo:lo + ff_chunk])
        up = _dot(h, w_in_ref[:, d_ff + lo:d_ff + lo + ff_chunk])
        act = (gate * jax.nn.sigmoid(gate) * up).astype(BF16)
        part = _dot(act, w_out_ref[lo:lo + ff_chunk, :])
        acc = part if acc is None else acc + part
    x2 = x1 + mod[5:6] * acc
    if final_norm:
        x2 = _rms(x2, gfin_ref[...])
    o_ref[...] = x2


def _out_ffn(x, mod, tokens_per_group, attn, w_outs, gffn, w_ffn_in, w_ffn_out, gfin, final_norm):
    n = x.shape[0]
    tm = TOKEN_TILE
    tiles_per_group = tokens_per_group // tm
    row = lambda i: (i, 0)
    in_specs = [pl.BlockSpec((tm, D_MODEL), row),
                pl.BlockSpec((1, 6, D_MODEL), lambda i: (i // tiles_per_group, 0, 0))]
    in_specs += [pl.BlockSpec((tm, a.shape[1]), row) for a in attn]
    in_specs += [_const_spec(w.shape) for w in w_outs]
    in_specs += [_const_spec(gffn.shape), _const_spec(w_ffn_in.shape), _const_spec(w_ffn_out.shape),
                 _const_spec(gfin.shape)]
    return pl.pallas_call(
        functools.partial(_out_ffn_kernel, n_attn=len(attn), ff_chunk=256, final_norm=final_norm),
        out_shape=jax.ShapeDtypeStruct((n, D_MODEL), F32),
        grid=(n // tm,),
        in_specs=in_specs,
        out_specs=pl.BlockSpec((tm, D_MODEL), row),
        compiler_params=_params(56 << 20, 1),
        name="out_ffn",
    )(x, mod, *attn, *w_outs, gffn, w_ffn_in, w_ffn_out, gfin)


def _rope_tables(n_tokens, rot_dim):
    t = np.arange(n_tokens)
    row = (t // GRID_W).astype(np.float32)
    col = (t % GRID_W).astype(np.float32)
    axis_dim = rot_dim // 2
    inv_freq = np.float32(ROPE_THETA) ** (-np.arange(0, axis_dim, 2, dtype=np.float32) / axis_dim)
    ang = np.concatenate([row[:, None] * inv_freq, col[:, None] * inv_freq], axis=-1).astype(np.float32)
    return np.cos(ang), np.sin(ang)


def _mla_rope_lanes(n_tokens):
    cos, sin = _rope_tables(n_tokens, MLA_ROPE_DIM)
    one = np.ones((n_tokens, MLA_NOPE_DIM), np.float32)
    zero = np.zeros((n_tokens, MLA_NOPE_DIM), np.float32)
    pad1 = np.ones((n_tokens, LANES - MLA_QK_DIM), np.float32)
    pad0 = np.zeros((n_tokens, LANES - MLA_QK_DIM), np.float32)
    return (jnp.asarray(np.concatenate([one, cos, cos, pad1], axis=-1)),
            jnp.asarray(np.concatenate([zero, -sin, sin, pad0], axis=-1)))


def _gqa_rope_lanes(n_tokens):
    cos, sin = _rope_tables(n_tokens, GQA_HEAD_DIM)
    return (jnp.asarray(np.concatenate([cos, cos], axis=-1)),
            jnp.asarray(np.concatenate([-sin, sin], axis=-1)))


def _swap_halves(w):
    half = w.shape[-1] // 2
    return jnp.concatenate([w[..., half:], w[..., :half]], axis=-1)


def _even_weights(w_in, w_uq, w_ukv):
    d = w_in.shape[0]
    i0 = MLA_Q_LORA
    i1 = i0 + MLA_KV_LORA
    i2 = i1 + MLA_ROPE_DIM
    w_kr = w_in[:, i1:i2]
    zl = jnp.zeros((d, MLA_NOPE_DIM), F32)
    zr = jnp.zeros((d, LANES - MLA_QK_DIM), F32)
    w_in_k = jnp.concatenate([w_in[:, :i1], w_in[:, i2:], zl, w_kr, zr, zl, _swap_halves(w_kr), zr],
                             axis=-1).astype(BF16)
    r = w_uq.shape[0]
    uq = w_uq.reshape(r, MLA_HEADS, MLA_QK_DIM)
    zpad = jnp.zeros((r, MLA_HEADS, LANES - MLA_QK_DIM), F32)
    znope = jnp.zeros((r, MLA_HEADS, MLA_NOPE_DIM), F32)
    q_plain = jnp.concatenate([uq, zpad], axis=-1).reshape(r, MLA_HEADS * LANES)
    q_swap = jnp.concatenate([znope, _swap_halves(uq[..., MLA_NOPE_DIM:]), zpad], axis=-1)
    w_uq_k = jnp.concatenate([q_plain, q_swap.reshape(r, MLA_HEADS * LANES)], axis=-1).astype(BF16)
    r = w_ukv.shape[0]
    ukv = w_ukv.reshape(r, MLA_HEADS, MLA_NOPE_DIM + MLA_V_DIM)
    k_pad = jnp.concatenate([ukv[..., :MLA_NOPE_DIM], jnp.zeros((r, MLA_HEADS, LANES - MLA_NOPE_DIM), F32)],
                            axis=-1).reshape(r, MLA_HEADS * LANES)
    v_cat = ukv[..., MLA_NOPE_DIM:].reshape(r, MLA_HEADS * MLA_V_DIM)
    w_ukv_k = jnp.concatenate([k_pad, v_cat], axis=-1).astype(BF16)
    return w_in_k, w_uq_k, w_ukv_k


def kernel(x_prompt, x_sample, cache_mla_ckv, cache_mla_krope, cache_na_k, cache_na_v, cache_gqa_k, cache_gqa_v, c, c_ctx, w_mod, b_mod, norm_mix, norm_ffn, norm_final, w_in_a, mla_q_norm, mla_w_uq, mla_kv_norm, mla_w_ukv, na_rpb, w_out_a, w_in_c, gqa_q_norm, gqa_k_norm, w_out_c, w_ffn_in, w_ffn_out):
    batch, seq, d = x_prompt.shape
    dec_batch, dec_seq, _ = x_sample.shape
    depth = w_mod.shape[0]
    past = cache_mla_ckv.shape[2]
    n_ctx = batch * seq
    n_lat = dec_batch * dec_seq

    cond = jnp.concatenate([c_ctx[None], c, jnp.zeros((8 - 1 - dec_batch, d), F32)], axis=0)
    mod = _modulation(cond, w_mod, b_mod).reshape(depth, 8, 6, d)

    xp = x_prompt.reshape(n_ctx, d)
    xs = x_sample.reshape(n_lat, d)
    cos_m, sin_m = _mla_rope_lanes(dec_seq)
    cos_g, sin_g = _gqa_rope_lanes(dec_seq)
    ident_cos = jnp.ones((TOKEN_TILE, LANES), F32)
    ident_sin = jnp.zeros((TOKEN_TILE, LANES), F32)
    gfin = norm_final.reshape(1, d)
    states = {k: [] for k in ("ckv", "krope", "nk", "nv", "gk", "gv")}

    for l in range(depth):
        mod_p = mod[l, 0:1]
        mod_s = mod[l, 1:1 + dec_batch]
        gmix = norm_mix[l].reshape(1, d)
        gffn = norm_ffn[l].reshape(1, d)
        if l % 2 == 0:
            e = l // 2
            w_in_k, w_uq_k, w_ukv_k = _even_weights(w_in_a[e], mla_w_uq[e], mla_w_ukv[e])
            qn = (mla_q_norm[e] * MLA_SCALE).reshape(1, -1)
            kvn = mla_kv_norm[e].reshape(1, -1)
            (qp, kp, vp, nqp, nkp, nvp, s_ckv, s_kr, s_nk, s_nv) = _even_in(
                xp, mod_p, n_ctx, gmix, w_in_k, qn, kvn, w_uq_k, w_ukv_k, ident_cos, ident_sin, True)
            states["ckv"].append(s_ckv.reshape(batch, seq, MLA_KV_LORA))
            states["krope"].append(s_kr[:, MLA_NOPE_DIM:MLA_QK_DIM].reshape(batch, seq, MLA_ROPE_DIM))
            states["nk"].append(s_nk.reshape(batch, seq, NA_HEADS, NA_HEAD_DIM))
            states["nv"].append(s_nv.reshape(batch, seq, NA_HEADS, NA_HEAD_DIM))
            qs, ks, vs, nqs, nks, nvs = _even_in(
                xs, mod_s, dec_seq, gmix, w_in_k, qn, kvn, w_uq_k, w_ukv_k, cos_m, sin_m, False)
            kr_cache = jnp.pad(cache_mla_krope[:, e],
                               ((0, 0), (0, 0), (MLA_NOPE_DIM, LANES - MLA_QK_DIM)))
            kc, vc = _cache_expand(cache_mla_ckv[:, e], kr_cache, w_ukv_k)

            r3 = lambda a, b_: a.reshape(b_, a.shape[0] // b_, a.shape[1])
            mla_kw = dict(groups=MLA_HEADS // 2, heads=2, k_stride=LANES, q_half_mask=False, pair_out=True)
            na_kw = dict(groups=NA_HEADS // 2, heads=2, k_stride=0, q_half_mask=True, pair_out=True)
            a_mla_p = _ctx_attention(r3(qp, batch), r3(kp, batch), r3(vp, batch), name="mla_ctx", **mla_kw)
            a_na_p = _ctx_attention(r3(nqp, batch), r3(nkp, batch), r3(nvp, batch), name="na_ctx", **na_kw)
            a_mla_s = _attention(r3(qs, dec_batch), [(r3(ks, dec_batch), r3(vs, dec_batch)), (kc, vc)],
                                 name="mla_lat", **mla_kw)
            bias = _na_bias_tables(na_rpb[e])
            a_na_s = _neighbourhood_attention(
                r3(nqs, dec_batch), r3(nks, dec_batch), r3(nvs, dec_batch),
                cache_na_k[:, e].reshape(dec_batch, past, NA_W).astype(BF16),
                cache_na_v[:, e].reshape(dec_batch, past, NA_W).astype(BF16), bias)
            attn_p = [a_mla_p.reshape(n_ctx, -1), a_na_p.reshape(n_ctx, -1)]
            attn_s = [a_mla_s.reshape(n_lat, -1), a_na_s.reshape(n_lat, -1)]
            wo = w_out_a[e].astype(BF16)
            half = MLA_HEADS * MLA_V_DIM
            w_outs = [wo[:half], wo[half:]]
        else:
            o = l // 2
            w_in_k = w_in_c[o].astype(BF16)
            qn = (gqa_q_norm[o] * GQA_SCALE).reshape(1, -1)
            kn = gqa_k_norm[o].reshape(1, -1)
            qp, kp, vp, s_gk, s_gv = _odd_in(xp, mod_p, n_ctx, gmix, w_in_k, qn, kn, ident_cos, ident_sin, True)
            states["gk"].append(s_gk.reshape(batch, seq, GQA_KV_HEADS, GQA_HEAD_DIM))
            states["gv"].append(s_gv.reshape(batch, seq, GQA_KV_HEADS, GQA_HEAD_DIM))
            qs, ks, vs = _odd_in(xs, mod_s, dec_seq, gmix, w_in_k, qn, kn, cos_g, sin_g, False)
            r3 = lambda a, b_: a.reshape(b_, a.shape[0] // b_, a.shape[1])
            gqa_kw = dict(groups=GQA_KV_HEADS, heads=GQA_GROUP, k_stride=0, q_half_mask=False, pair_out=False)
            a_p = _ctx_attention(r3(qp, batch), r3(kp, batch), r3(vp, batch), name="gqa_ctx", **gqa_kw)
            kcache = cache_gqa_k[:, o].reshape(dec_batch, past, -1).astype(BF16)
            vcache = cache_gqa_v[:, o].reshape(dec_batch, past, -1).astype(BF16)
            a_s = _attention(r3(qs, dec_batch), [(r3(ks, dec_batch), r3(vs, dec_batch)), (kcache, vcache)],
                             name="gqa_lat", **gqa_kw)
            attn_p = [a_p.reshape(n_ctx, -1)]
            attn_s = [a_s.reshape(n_lat, -1)]
            w_outs = [w_out_c[o].astype(BF16)]
        last = l == depth - 1
        wfi = w_ffn_in[l].astype(BF16)
        wfo = w_ffn_out[l].astype(BF16)
        xp = _out_ffn(xp, mod_p, n_ctx, attn_p, w_outs, gffn, wfi, wfo, gfin, last)
        xs = _out_ffn(xs, mod_s, dec_seq, attn_s, w_outs, gffn, wfi, wfo, gfin, last)

    y_prompt = xp.reshape(batch, seq, d)
    y_sample = xs.reshape(dec_batch, dec_seq, d)
    return (y_prompt, y_sample,
            jnp.stack(states["ckv"], axis=1), jnp.stack(states["krope"], axis=1),
            jnp.stack(states["nk"], axis=1), jnp.stack(states["nv"], axis=1),
            jnp.stack(states["gk"], axis=1), jnp.stack(states["gv"], axis=1))
```

```python
import functools
import math

import numpy as np
import jax
import jax.numpy as jnp
from jax import lax
from jax.experimental import pallas as pl
from jax.experimental.pallas import tpu as pltpu

LANES = 128
V7X_VMEM_BYTES = 64 * 1024 * 1024

D_MODEL = 1024
GRID_W = 64
ROPE_THETA = 10000.0
RMS_EPS = 1e-6
NEG_INF = -1e30
MLA_HEADS = 8
MLA_Q_LORA = 256
MLA_KV_LORA = 256
MLA_NOPE_DIM = 64
MLA_ROPE_DIM = 32
MLA_V_DIM = 64
MLA_QK_DIM = MLA_NOPE_DIM + MLA_ROPE_DIM
LOG2E = math.log2(math.e)
MLA_SCALE = MLA_QK_DIM ** -0.5 * LOG2E
NA_HEADS = 8
NA_HEAD_DIM = 64
NA_WIN_H = 8
NA_WIN_W = 16
NA_SCALE = NA_HEAD_DIM ** -0.5 * LOG2E
NA_W = NA_HEADS * NA_HEAD_DIM
GQA_HEADS = 8
GQA_KV_HEADS = 2
GQA_HEAD_DIM = 128
GQA_SCALE = GQA_HEAD_DIM ** -0.5 * LOG2E
GQA_GROUP = GQA_HEADS // GQA_KV_HEADS

TOKEN_TILE = 512
MLA_Q_TILE = 1024
GQA_Q_TILE = 512
ATTN_K_CHUNK = 512
NA_Q_ROWS = 4
NA_BAND_ROWS = 12
NA_BLOCKS_PER_STEP = 4

BF16 = jnp.bfloat16
F32 = jnp.float32


def _vmem_limit(nbytes):
    return int(min(V7X_VMEM_BYTES - (4 << 20), max(nbytes, 16 << 20)))


def _params(nbytes, ndims):
    return pltpu.CompilerParams(dimension_semantics=("arbitrary",) * ndims,
                                vmem_limit_bytes=_vmem_limit(nbytes))


def _rms(x, gain):
    return x * lax.rsqrt(jnp.mean(x * x, axis=-1, keepdims=True) + RMS_EPS) * gain


def _dot(a, b):
    return jnp.dot(a, b, preferred_element_type=F32)


def _dot_nt(a, b):
    return lax.dot_general(a, b, (((1,), (1,)), ((), ())), preferred_element_type=F32)


def _const_spec(shape):
    nd = len(shape)
    return pl.BlockSpec(shape, lambda *_: (0,) * nd, pipeline_mode=pl.Buffered(1))


def _mod_kernel(cond_ref, w_ref, b_ref, o_ref):
    c = cond_ref[...]
    s = (c * jax.nn.sigmoid(c)).astype(BF16)
    o_ref[0] = _dot(s, w_ref[0].astype(BF16)) + b_ref[0]


def _modulation(cond, w_mod, b_mod):
    depth, d, n = w_mod.shape
    rows = cond.shape[0]
    bn = 1024
    return pl.pallas_call(
        _mod_kernel,
        out_shape=jax.ShapeDtypeStruct((depth, rows, n), F32),
        grid=(depth, n // bn),
        in_specs=[pl.BlockSpec((rows, d), lambda l, j: (0, 0)),
                  pl.BlockSpec((1, d, bn), lambda l, j: (l, 0, j)),
                  pl.BlockSpec((1, 1, bn), lambda l, j: (l, 0, j))],
        out_specs=pl.BlockSpec((1, rows, bn), lambda l, j: (l, 0, j)),
        compiler_params=_params(3 * d * bn * 4, 2),
        name="ada_modulation",
    )(cond, w_mod, b_mod.reshape(depth, 1, n))


def _even_in_kernel(x_ref, mod_ref, gmix_ref, w_in_ref, qn_ref, kvn_ref, w_uq_ref, w_ukv_ref,
                    cos_ref, sin_ref, *out_refs, with_state):
    q_ref, k_ref, v_ref, nq_ref, nk_ref, nv_ref = out_refs[:6]
    x = x_ref[...]
    mod = mod_ref[0]
    h = _rms(x, gmix_ref[...]) * (1.0 + mod[1:2]) + mod[0:1]
    p = _dot(h.astype(BF16), w_in_ref[...])
    cq = p[:, 0:256]
    ckv = _rms(p[:, 256:512], kvn_ref[...])
    nq = p[:, 512:1024]
    nk = p[:, 1024:1536]
    nv = p[:, 1536:2048]
    kr = p[:, 2048:2176]
    kr_sw = p[:, 2176:2304]
    cos = cos_ref[...]
    sin = sin_ref[...]
    qq = _dot(_rms(cq, qn_ref[...]).astype(BF16), w_uq_ref[...])
    kv = _dot(ckv.astype(BF16), w_ukv_ref[...])
    kr_rot = kr * cos + kr_sw * sin
    for hd in range(MLA_HEADS):
        lo = hd * LANES
        qh = qq[:, lo:lo + LANES] * cos + qq[:, 1024 + lo:1024 + lo + LANES] * sin
        q_ref[:, lo:lo + LANES] = qh.astype(BF16)
        k_ref[:, lo:lo + LANES] = (kv[:, lo:lo + LANES] + kr_rot).astype(BF16)
    v_ref[...] = kv[:, 1024:1536].astype(BF16)
    nq_ref[...] = (nq * NA_SCALE).astype(BF16)
    nk_ref[...] = nk.astype(BF16)
    nv_ref[...] = nv.astype(BF16)
    if with_state:
        s_ckv_ref, s_kr_ref, s_nk_ref, s_nv_ref = out_refs[6:]
        s_ckv_ref[...] = ckv
        s_kr_ref[...] = kr
        s_nk_ref[...] = nk
        s_nv_ref[...] = nv


def _even_in(x, mod, tokens_per_group, gmix, w_in, qn, kvn, w_uq, w_ukv, cos, sin, with_state):
    n = x.shape[0]
    tm = TOKEN_TILE
    tiles_per_group = tokens_per_group // tm
    rope_tiles = cos.shape[0] // tm
    row = lambda i: (i, 0)
    outs = [jax.ShapeDtypeStruct((n, 1024), BF16), jax.ShapeDtypeStruct((n, 1024), BF16),
            jax.ShapeDtypeStruct((n, 512), BF16), jax.ShapeDtypeStruct((n, 512), BF16),
            jax.ShapeDtypeStruct((n, 512), BF16), jax.ShapeDtypeStruct((n, 512), BF16)]
    if with_state:
        outs += [jax.ShapeDtypeStruct((n, 256), F32), jax.ShapeDtypeStruct((n, 128), F32),
                 jax.ShapeDtypeStruct((n, 512), F32), jax.ShapeDtypeStruct((n, 512), F32)]
    return pl.pallas_call(
        functools.partial(_even_in_kernel, with_state=with_state),
        out_shape=outs,
        grid=(n // tm,),
        in_specs=[pl.BlockSpec((tm, D_MODEL), row),
                  pl.BlockSpec((1, 6, D_MODEL), lambda i: (i // tiles_per_group, 0, 0)),
                  _const_spec(gmix.shape), _const_spec(w_in.shape), _const_spec(qn.shape),
                  _const_spec(kvn.shape), _const_spec(w_uq.shape), _const_spec(w_ukv.shape),
                  pl.BlockSpec((tm, LANES), lambda i: (i % rope_tiles, 0)),
                  pl.BlockSpec((tm, LANES), lambda i: (i % rope_tiles, 0))],
        out_specs=[pl.BlockSpec((tm, o.shape[1]), row) for o in outs],
        compiler_params=_params(40 << 20, 1),
        name="even_in",
    )(x, mod, gmix, w_in, qn, kvn, w_uq, w_ukv, cos, sin)


def _cache_expand_kernel(ckv_ref, kr_ref, w_ukv_ref, k_ref, v_ref):
    kv = _dot(ckv_ref[0].astype(BF16), w_ukv_ref[...])
    kr = kr_ref[0]
    for hd in range(MLA_HEADS):
        lo = hd * LANES
        k_ref[0, :, lo:lo + LANES] = (kv[:, lo:lo + LANES] + kr).astype(BF16)
    v_ref[0] = kv[:, 1024:1536].astype(BF16)


def _cache_expand(ckv, kr128, w_ukv):
    b, s, _ = ckv.shape
    return pl.pallas_call(
        _cache_expand_kernel,
        out_shape=[jax.ShapeDtypeStruct((b, s, 1024), BF16), jax.ShapeDtypeStruct((b, s, 512), BF16)],
        grid=(b,),
        in_specs=[pl.BlockSpec((1, s, MLA_KV_LORA), lambda i: (i, 0, 0)),
                  pl.BlockSpec((1, s, LANES), lambda i: (i, 0, 0)),
                  _const_spec(w_ukv.shape)],
        out_specs=[pl.BlockSpec((1, s, 1024), lambda i: (i, 0, 0)),
                   pl.BlockSpec((1, s, 512), lambda i: (i, 0, 0))],
        compiler_params=_params(16 << 20, 1),
        name="mla_cache_expand",
    )(ckv, kr128, w_ukv)


def _odd_in_kernel(x_ref, mod_ref, gmix_ref, w_in_ref, qn_ref, kn_ref, avg_ref, cos_ref, sin_ref, *out_refs,
                   with_state):
    q_ref, k_ref, v_ref = out_refs[:3]
    x = x_ref[...]
    mod = mod_ref[0]
    h = (_rms(x, gmix_ref[...]) * (1.0 + mod[1:2]) + mod[0:1]).astype(BF16)
    cos = cos_ref[...]
    sin = sin_ref[...]
    half = GQA_HEAD_DIM // 2
    q_cos, q_sin = qn_ref[0:1] * cos, qn_ref[1:2] * sin
    k_cos, k_sin = kn_ref[0:1] * cos, kn_ref[1:2] * sin

    pair = 2 * LANES

    def project(col):
        return _dot(h, w_in_ref[:, col:col + pair])

    def inv_rms(p):
        sq = p * p
        hi = sq.astype(BF16)
        lo = (sq - hi.astype(F32)).astype(BF16)
        return lax.rsqrt(_dot(hi, avg_ref[...]) + _dot(lo, avg_ref[...]) + RMS_EPS)

    k_off = GQA_HEADS * LANES
    v_off = k_off + GQA_KV_HEADS * LANES
    cols = [g * pair for g in range(GQA_HEADS // 2)] + [k_off, v_off]
    p_next = project(cols[0])
    for idx, col in enumerate(cols[:-1]):
        p, p_next = p_next, project(cols[idx + 1])
        r = inv_rms(p)
        for i in range(2):
            ph = p[:, i * LANES:(i + 1) * LANES]
            rh = r[:, i * LANES:(i + 1) * LANES]
            lo = i * LANES
            if col < k_off:
                qh = (ph * q_cos + pltpu.roll(ph, half, 1) * q_sin) * rh
                q_ref[:, col + lo:col + lo + LANES] = qh.astype(BF16)
            else:
                kh = (ph * k_cos + pltpu.roll(ph, half, 1) * k_sin) * rh
                k_ref[:, lo:lo + LANES] = kh.astype(BF16)
                if with_state:
                    out_refs[3][:, lo:lo + LANES] = ph * rh * kn_ref[0:1]
    v_ref[...] = p_next.astype(BF16)
    if with_state:
        out_refs[4][...] = p_next


def _odd_in(x, mod, tokens_per_group, gmix, w_in, qn, kn, cos, sin, with_state):
    n = x.shape[0]
    tm = TOKEN_TILE
    tiles_per_group = tokens_per_group // tm
    rope_tiles = cos.shape[0] // tm
    row = lambda i: (i, 0)
    outs = [jax.ShapeDtypeStruct((n, 1024), BF16), jax.ShapeDtypeStruct((n, 256), BF16),
            jax.ShapeDtypeStruct((n, 256), BF16)]
    if with_state:
        outs += [jax.ShapeDtypeStruct((n, 256), F32), jax.ShapeDtypeStruct((n, 256), F32)]
    avg = np.kron(np.eye(2), np.full((GQA_HEAD_DIM, GQA_HEAD_DIM), 1.0 / GQA_HEAD_DIM))
    avg = jnp.asarray(avg, BF16)
    return pl.pallas_call(
        functools.partial(_odd_in_kernel, with_state=with_state),
        out_shape=outs,
        grid=(n // tm,),
        in_specs=[pl.BlockSpec((tm, D_MODEL), row),
                  pl.BlockSpec((1, 6, D_MODEL), lambda i: (i // tiles_per_group, 0, 0)),
                  _const_spec(gmix.shape), _const_spec(w_in.shape), _const_spec(qn.shape),
                  _const_spec(kn.shape), _const_spec(avg.shape),
                  pl.BlockSpec((tm, LANES), lambda i: (i % rope_tiles, 0)),
                  pl.BlockSpec((tm, LANES), lambda i: (i % rope_tiles, 0))],
        out_specs=[pl.BlockSpec((tm, o.shape[1]), row) for o in outs],
        compiler_params=_params(32 << 20, 1),
        name="odd_in",
    )(x, mod, gmix, w_in, qn, kn, avg, cos, sin)


def _attn_kernel(*refs, n_src, heads, q_tile, src_len, k_stride, q_half_mask, pair_out):
    q_ref = refs[0]
    kv_refs = refs[1:1 + 2 * n_src]
    o_ref = refs[1 + 2 * n_src]
    s_ref = refs[2 + 2 * n_src]
    n_tiles = q_ref.shape[1] // q_tile
    chunks = []
    for src in range(n_src):
        ck = min(ATTN_K_CHUNK, src_len[src])
        for c in range(src_len[src] // ck):
            chunks.append((src, c * ck, ck))
    lane = lax.broadcasted_iota(jnp.int32, (q_tile, LANES), 1)

    def rows(t):
        if isinstance(t, int):
            return slice(t * q_tile, (t + 1) * q_tile)
        return pl.ds(pl.multiple_of(t * q_tile, q_tile), q_tile)

    def load_q(t, j):
        if q_half_mask:
            qb = q_ref[0, rows(t), :]
            return jnp.where((lane >= 64) == (j == 1), qb, jnp.zeros_like(qb))
        return q_ref[0, rows(t), j * LANES:(j + 1) * LANES]

    def slot(j, q_next, m_prev):
        if q_next is not None:
            m_part = jnp.full((q_tile, LANES), -jnp.inf, F32)
        if m_prev is not None:
            l_part = jnp.zeros((q_tile, LANES), F32)
            acc = jnp.zeros((q_tile, LANES), F32)
        off = 0
        for src, k0, ck in chunks:
            if q_next is not None:
                kc = kv_refs[2 * src][0, j * k_stride:j * k_stride + LANES, k0:k0 + ck]
                s_new = _dot(q_next, kc)
            if m_prev is not None:
                s_old = s_ref[:, off:off + ck]
            if q_next is not None:
                s_ref[:, off:off + ck] = s_new
                for i in range(ck // LANES):
                    m_part = jnp.maximum(m_part, s_new[:, i * LANES:(i + 1) * LANES])
            if m_prev is not None:
                p = jnp.exp2(s_old - m_prev)
                for i in range(ck // LANES):
                    l_part = l_part + p[:, i * LANES:(i + 1) * LANES]
                acc = acc + _dot(p.astype(BF16), kv_refs[2 * src + 1][0, k0:k0 + ck, :])
            off += ck
        m_next = None if q_next is None else jnp.max(m_part, axis=-1, keepdims=True)
        o_prev = None if m_prev is None else acc / jnp.sum(l_part, axis=-1, keepdims=True)
        return m_next, o_prev

    def write_out(t, j, o):
        o = o.astype(o_ref.dtype)
        if not pair_out:
            o_ref[0, rows(t), j * LANES:(j + 1) * LANES] = o
        elif j == 0:
            o_ref[0, rows(t), :] = o
        else:
            o_ref[0, rows(t), :] = jnp.where(lane < 64, o_ref[0, rows(t), :], o)

    def tile(j, t, m, last):
        m_next, o = slot(j, None if last else load_q(t + 1, j), m)
        write_out(t, j, o)
        return m_next

    for j in range(heads):
        m, _ = slot(j, load_q(0, j), None)
        if n_tiles > 1:
            m = lax.fori_loop(0, n_tiles - 1, lambda t, m, j=j: tile(j, t, m, False), m)
        tile(j, n_tiles - 1, m, True)


def _attention(q, sources, *, groups, heads, k_stride, q_half_mask, pair_out, q_tile, name):
    b, t, _ = q.shape
    q_block = LANES if q_half_mask else heads * LANES
    k_block = LANES if k_stride == 0 else heads * LANES
    out_block = LANES if pair_out else heads * LANES
    src_len = tuple(k.shape[1] for k, _ in sources)
    in_specs = [pl.BlockSpec((1, t, q_block), lambda bi, g: (bi, 0, g))]
    args = [q]
    for k, v in sources:
        s = k.shape[1]
        in_specs.append(pl.BlockSpec((1, k_block, s), lambda bi, g: (bi, g, 0)))
        in_specs.append(pl.BlockSpec((1, s, LANES), lambda bi, g: (bi, 0, g)))
        args += [jnp.swapaxes(k, 1, 2), v]
    total = sum(src_len)
    return pl.pallas_call(
        functools.partial(_attn_kernel, n_src=len(sources), heads=heads, q_tile=q_tile,
                          src_len=src_len, k_stride=k_stride, q_half_mask=q_half_mask,
                          pair_out=pair_out),
        out_shape=jax.ShapeDtypeStruct((b, t, groups * out_block), BF16),
        grid=(b, groups),
        in_specs=in_specs,
        out_specs=pl.BlockSpec((1, t, out_block), lambda bi, g: (bi, 0, g)),
        scratch_shapes=[pltpu.VMEM((q_tile, total), F32)],
        compiler_params=_params(58 << 20, 2),
        name=name,
    )(*args)


def _ctx_attn_kernel(q_ref, k_ref, v_ref, o_ref, *, heads, k_stride, q_half_mask, pair_out):
    nb, t, _ = q_ref.shape
    lane = lax.broadcasted_iota(jnp.int32, (t, LANES), 1)
    for b in range(nb):
        outs = []
        for j in range(heads):
            if q_half_mask:
                qb = q_ref[b]
                q = jnp.where((lane >= 64) == (j == 1), qb, jnp.zeros_like(qb))
            else:
                q = q_ref[b, :, j * LANES:(j + 1) * LANES]
            s = _dot_nt(q, k_ref[b, :, j * k_stride:j * k_stride + LANES])
            p = jnp.exp2(s - jnp.max(s, axis=-1, keepdims=True))
            acc = _dot(p.astype(BF16), v_ref[b])
            outs.append(acc / jnp.sum(p, axis=-1, keepdims=True))
        if pair_out:
            o_ref[b] = jnp.where(lane < 64, outs[0], outs[1]).astype(o_ref.dtype)
        else:
            for j in range(heads):
                o_ref[b, :, j * LANES:(j + 1) * LANES] = outs[j].astype(o_ref.dtype)


def _ctx_attention(q, k, v, *, groups, heads, k_stride, q_half_mask, pair_out, name):
    b, t, _ = q.shape
    nb = 4
    q_block = LANES if q_half_mask else heads * LANES
    k_block = LANES if k_stride == 0 else heads * LANES
    out_block = LANES if pair_out else heads * LANES
    spec = lambda w: pl.BlockSpec((nb, t, w), lambda bi, g: (bi, 0, g))
    return pl.pallas_call(
        functools.partial(_ctx_attn_kernel, heads=heads, k_stride=k_stride, q_half_mask=q_half_mask,
                          pair_out=pair_out),
        out_shape=jax.ShapeDtypeStruct((b, t, groups * out_block), BF16),
        grid=(b // nb, groups),
        in_specs=[spec(q_block), spec(k_block), spec(LANES)],
        out_specs=spec(out_block),
        compiler_params=_params(32 << 20, 2),
        name=name,
    )(q, k, v)


def _na_bias_tables(rpb):
    n_rows = GRID_W
    h, n_dr, n_dc = rpb.shape
    edge = n_dc - 1 - (NA_WIN_W - 1)
    w = jnp.concatenate([rpb[..., NA_WIN_W - 1:],
                         jnp.broadcast_to(rpb[..., n_dc - 1:], (h, n_dr, GRID_W - 1 - edge)),
                         jnp.broadcast_to(rpb[..., :1], (h, n_dr, GRID_W - (NA_WIN_W - 1) + 1)),
                         rpb[..., 1:NA_WIN_W - 1]], axis=-1)
    toe = jnp.tile(w, (1, 1, GRID_W))[..., :GRID_W * (2 * GRID_W - 1)]
    toe = toe.reshape(h, n_dr, GRID_W, 2 * GRID_W - 1)[..., :GRID_W]
    cols = np.arange(GRID_W)
    cs = np.clip(cols - NA_WIN_W // 2, 0, GRID_W - NA_WIN_W)
    col_ok = (cols[None, :] >= cs[:, None]) & (cols[None, :] < cs[:, None] + NA_WIN_W)
    toe = jnp.where(col_ok, toe * LOG2E, NEG_INF)
    toe = toe.transpose(0, 2, 1, 3).reshape(h, GRID_W, n_dr * GRID_W)
    pieces = []
    for blk in (0, 1, n_rows // NA_Q_ROWS - 1):
        b0 = int(np.clip(NA_Q_ROWS * blk - NA_WIN_H // 2, 0, n_rows - NA_BAND_ROWS))
        for qr in range(NA_Q_ROWS):
            r = NA_Q_ROWS * blk + qr
            rs = int(np.clip(r - NA_WIN_H // 2, 0, n_rows - NA_WIN_H))
            dr0 = rs - r + NA_WIN_H - 1
            seen = toe[:, None, :, dr0 * GRID_W:(dr0 + NA_WIN_H) * GRID_W]
            before = (rs - b0) * GRID_W
            after = (NA_BAND_ROWS - NA_WIN_H) * GRID_W - before
            pieces.append(jnp.pad(seen, ((0, 0), (0, 0), (0, 0), (before, after)), constant_values=NEG_INF))
    return jnp.concatenate(pieces, axis=1).reshape(h, 3, NA_Q_ROWS * GRID_W, NA_BAND_ROWS * GRID_W)


def _na_kernel(q_ref, k_ref, v_ref, kc_ref, vc_ref, bias_ref, o_ref, s_ref):
    nq = NA_Q_ROWS * GRID_W
    nb = NA_BAND_ROWS * GRID_W
    per = NA_BLOCKS_PER_STEP
    t = k_ref.shape[1]
    n_blocks = t // nq
    n_groups = n_blocks // per
    lane = lax.broadcasted_iota(jnp.int32, (nq, LANES), 1)

    def block_rows(i):
        if isinstance(i, int):
            return slice(i * nq, (i + 1) * nq)
        return pl.ds(pl.multiple_of(i * nq, nq), nq)

    def band_rows(i):
        first = nq * i - (NA_WIN_H // 2) * GRID_W
        if isinstance(i, int):
            first = min(max(first, 0), t - nb)
            return slice(first, first + nb)
        return pl.ds(pl.multiple_of(jnp.clip(first, 0, t - nb), nq), nb)

    def bias_class(i):
        if isinstance(i, int):
            return min(i, 1) + max(i - (n_blocks - 2), 0)
        return jnp.minimum(i, 1) + jnp.maximum(i - (n_blocks - 2), 0)

    def slot(j, g_next, g_prev, ms_prev):
        new_ms = []
        for blk in range(per):
            srow = slice(blk * nq, (blk + 1) * nq)
            if g_next is not None:
                i = g_next * per + blk
                qb = q_ref[0, block_rows(i), :]
                q = jnp.where((lane >= 64) == (j == 1), qb, jnp.zeros_like(qb))
                s_band = _dot_nt(q, k_ref[0, band_rows(i), :]) + bias_ref[j, bias_class(i)]
                s_ctx = _dot_nt(q, kc_ref[0])
            if g_prev is not None:
                old_band = s_ref[srow, :nb]
                old_ctx = s_ref[srow, nb:]
            if g_next is not None:
                s_ref[srow, :nb] = s_band
                s_ref[srow, nb:] = s_ctx
                new_ms.append(jnp.maximum(jnp.max(s_band, axis=-1, keepdims=True),
                                          jnp.max(s_ctx, axis=-1, keepdims=True)))
            if g_prev is not None:
                i = g_prev * per + blk
                m = ms_prev[blk]
                p_band = jnp.exp2(old_band - m)
                p_ctx = jnp.exp2(old_ctx - m)
                l = jnp.sum(p_band, axis=-1, keepdims=True) + jnp.sum(p_ctx, axis=-1, keepdims=True)
                acc = (_dot(p_band.astype(BF16), v_ref[0, band_rows(i), :])
                       + _dot(p_ctx.astype(BF16), vc_ref[0]))
                o = (acc / l).astype(o_ref.dtype)
                if j == 0:
                    o_ref[0, block_rows(i), :] = o
                else:
                    o_ref[0, block_rows(i), :] = jnp.where(lane < 64, o_ref[0, block_rows(i), :], o)
        return tuple(new_ms)

    for j in range(2):
        ms = slot(j, 0, None, None)
        if n_groups > 1:
            ms = lax.fori_loop(0, n_groups - 1, lambda g, ms, j=j: slot(j, g + 1, g, ms), ms)
        slot(j, None, n_groups - 1, ms)


def _neighbourhood_attention(q, k, v, kc, vc, bias):
    b, t, w = q.shape
    pairs = w // LANES
    nq = NA_Q_ROWS * GRID_W
    nb = NA_BAND_ROWS * GRID_W
    c = kc.shape[1]
    seq = lambda n: pl.BlockSpec((1, n, LANES), lambda bi, g: (bi, 0, g))
    return pl.pallas_call(
        _na_kernel,
        out_shape=jax.ShapeDtypeStruct((b, t, w), BF16),
        grid=(b, pairs),
        in_specs=[seq(t), seq(t), seq(t), seq(c), seq(c),
                  pl.BlockSpec((2, 3, nq, nb), lambda bi, g: (g, 0, 0, 0))],
        out_specs=seq(t),
        scratch_shapes=[pltpu.VMEM((NA_BLOCKS_PER_STEP * nq, nb + c), F32)],
        compiler_params=_params(40 << 20, 2),
        name="neighbourhood_attention",
    )(q, k, v, kc, vc, bias)


def _out_ffn_kernel(*refs, n_attn, ff_chunk, final_norm):
    x_ref, mod_ref = refs[0], refs[1]
    a_refs = refs[2:2 + n_attn]
    w_refs = refs[2 + n_attn:2 + 2 * n_attn]
    gffn_ref, w_in_ref, w_out_ref, gfin_ref, o_ref = refs[2 + 2 * n_attn:]
    mod = mod_ref[0]
    mix = _dot(a_refs[0][...], w_refs[0][...])
    for a_ref, w_ref in zip(a_refs[1:], w_refs[1:]):
        mix = mix + _dot(a_ref[...], w_ref[...])
    x1 = x_ref[...] + mod[2:3] * mix
    h = (_rms(x1, gffn_ref[...]) * (1.0 + mod[4:5]) + mod[3:4]).astype(BF16)
    d_ff = w_out_ref.shape[1]
    acc = None
    for c in range(d_ff // ff_chunk):
        lo = c * ff_chunk
        gate = _dot(h, w_in_ref[0, :, lo:lo + ff_chunk])
        up = _dot(h, w_in_ref[0, :, d_ff + lo:d_ff + lo + ff_chunk])
        act = (gate * jax.nn.sigmoid(gate) * up).astype(BF16)
        part = _dot(act, w_out_ref[0, lo:lo + ff_chunk, :])
        acc = part if acc is None else acc + part
    x2 = x1 + mod[5:6] * acc
    if final_norm:
        x2 = _rms(x2, gfin_ref[...])
    o_ref[...] = x2


def _out_ffn(x, mod, tokens_per_group, attn, w_outs, gffn, layer, w_ffn_in, w_ffn_out, gfin, final_norm):
    n = x.shape[0]
    tm = TOKEN_TILE
    tiles_per_group = tokens_per_group // tm
    row = lambda i: (i, 0)
    in_specs = [pl.BlockSpec((tm, D_MODEL), row),
                pl.BlockSpec((1, 6, D_MODEL), lambda i: (i // tiles_per_group, 0, 0))]
    in_specs += [pl.BlockSpec((tm, a.shape[1]), row) for a in attn]
    in_specs += [_const_spec(w.shape) for w in w_outs]
    layer_spec = lambda w: pl.BlockSpec((1,) + w.shape[1:], lambda i: (layer, 0, 0),
                                        pipeline_mode=pl.Buffered(1))
    in_specs += [_const_spec(gffn.shape), layer_spec(w_ffn_in), layer_spec(w_ffn_out),
                 _const_spec(gfin.shape)]
    return pl.pallas_call(
        functools.partial(_out_ffn_kernel, n_attn=len(attn), ff_chunk=256, final_norm=final_norm),
        out_shape=jax.ShapeDtypeStruct((n, D_MODEL), F32),
        grid=(n // tm,),
        in_specs=in_specs,
        out_specs=pl.BlockSpec((tm, D_MODEL), row),
        compiler_params=_params(56 << 20, 1),
        name="out_ffn",
    )(x, mod, *attn, *w_outs, gffn, w_ffn_in, w_ffn_out, gfin)


def _rope_tables(n_tokens, rot_dim):
    t = np.arange(n_tokens)
    row = (t // GRID_W).astype(np.float32)
    col = (t % GRID_W).astype(np.float32)
    axis_dim = rot_dim // 2
    inv_freq = np.float32(ROPE_THETA) ** (-np.arange(0, axis_dim, 2, dtype=np.float32) / axis_dim)
    ang = np.concatenate([row[:, None] * inv_freq, col[:, None] * inv_freq], axis=-1).astype(np.float32)
    return np.cos(ang), np.sin(ang)


def _mla_rope_lanes(n_tokens):
    cos, sin = _rope_tables(n_tokens, MLA_ROPE_DIM)
    one = np.ones((n_tokens, MLA_NOPE_DIM), np.float32)
    zero = np.zeros((n_tokens, MLA_NOPE_DIM), np.float32)
    pad1 = np.ones((n_tokens, LANES - MLA_QK_DIM), np.float32)
    pad0 = np.zeros((n_tokens, LANES - MLA_QK_DIM), np.float32)
    return (jnp.asarray(np.concatenate([one, cos, cos, pad1], axis=-1)),
            jnp.asarray(np.concatenate([zero, -sin, sin, pad0], axis=-1)))


def _gqa_rope_lanes(n_tokens):
    cos, sin = _rope_tables(n_tokens, GQA_HEAD_DIM)
    return (jnp.asarray(np.concatenate([cos, cos], axis=-1)),
            jnp.asarray(np.concatenate([-sin, sin], axis=-1)))


def _swap_halves(w):
    half = w.shape[-1] // 2
    return jnp.concatenate([w[..., half:], w[..., :half]], axis=-1)


def _even_weights(w_in, w_uq, w_ukv):
    d = w_in.shape[0]
    i0 = MLA_Q_LORA
    i1 = i0 + MLA_KV_LORA
    i2 = i1 + MLA_ROPE_DIM
    w_kr = w_in[:, i1:i2]
    zl = jnp.zeros((d, MLA_NOPE_DIM), F32)
    zr = jnp.zeros((d, LANES - MLA_QK_DIM), F32)
    w_in_k = jnp.concatenate([w_in[:, :i1], w_in[:, i2:], zl, w_kr, zr, zl, _swap_halves(w_kr), zr],
                             axis=-1).astype(BF16)
    r = w_uq.shape[0]
    uq = w_uq.reshape(r, MLA_HEADS, MLA_QK_DIM)
    zpad = jnp.zeros((r, MLA_HEADS, LANES - MLA_QK_DIM), F32)
    znope = jnp.zeros((r, MLA_HEADS, MLA_NOPE_DIM), F32)
    q_plain = jnp.concatenate([uq, zpad], axis=-1).reshape(r, MLA_HEADS * LANES)
    q_swap = jnp.concatenate([znope, _swap_halves(uq[..., MLA_NOPE_DIM:]), zpad], axis=-1)
    w_uq_k = jnp.concatenate([q_plain, q_swap.reshape(r, MLA_HEADS * LANES)], axis=-1).astype(BF16)
    r = w_ukv.shape[0]
    ukv = w_ukv.reshape(r, MLA_HEADS, MLA_NOPE_DIM + MLA_V_DIM)
    k_pad = jnp.concatenate([ukv[..., :MLA_NOPE_DIM], jnp.zeros((r, MLA_HEADS, LANES - MLA_NOPE_DIM), F32)],
                            axis=-1).reshape(r, MLA_HEADS * LANES)
    v_cat = ukv[..., MLA_NOPE_DIM:].reshape(r, MLA_HEADS * MLA_V_DIM)
    w_ukv_k = jnp.concatenate([k_pad, v_cat], axis=-1).astype(BF16)
    return w_in_k, w_uq_k, w_ukv_k


def kernel(x_prompt, x_sample, cache_mla_ckv, cache_mla_krope, cache_na_k, cache_na_v, cache_gqa_k, cache_gqa_v, c, c_ctx, w_mod, b_mod, norm_mix, norm_ffn, norm_final, w_in_a, mla_q_norm, mla_w_uq, mla_kv_norm, mla_w_ukv, na_rpb, w_out_a, w_in_c, gqa_q_norm, gqa_k_norm, w_out_c, w_ffn_in, w_ffn_out):
    batch, seq, d = x_prompt.shape
    dec_batch, dec_seq, _ = x_sample.shape
    depth = w_mod.shape[0]
    past = cache_mla_ckv.shape[2]
    n_ctx = batch * seq
    n_lat = dec_batch * dec_seq

    cond = jnp.concatenate([c_ctx[None], c, jnp.zeros((8 - 1 - dec_batch, d), F32)], axis=0)
    mod = _modulation(cond, w_mod, b_mod).reshape(depth, 8, 6, d)

    xp = x_prompt.reshape(n_ctx, d)
    xs = x_sample.reshape(n_lat, d)
    cos_m, sin_m = _mla_rope_lanes(dec_seq)
    cos_g, sin_g = _gqa_rope_lanes(dec_seq)
    ident_cos = jnp.ones((TOKEN_TILE, LANES), F32)
    ident_sin = jnp.zeros((TOKEN_TILE, LANES), F32)
    gfin = norm_final.reshape(1, d)
    states = {k: [] for k in ("ckv", "krope", "nk", "nv", "gk", "gv")}
    wfi = w_ffn_in.astype(BF16)
    wfo = w_ffn_out.astype(BF16)

    for l in range(depth):
        mod_p = mod[l, 0:1]
        mod_s = mod[l, 1:1 + dec_batch]
        gmix = norm_mix[l].reshape(1, d)
        gffn = norm_ffn[l].reshape(1, d)
        if l % 2 == 0:
            e = l // 2
            w_in_k, w_uq_k, w_ukv_k = _even_weights(w_in_a[e], mla_w_uq[e], mla_w_ukv[e])
            qn = (mla_q_norm[e] * MLA_SCALE).reshape(1, -1)
            kvn = mla_kv_norm[e].reshape(1, -1)
            (qp, kp, vp, nqp, nkp, nvp, s_ckv, s_kr, s_nk, s_nv) = _even_in(
                xp, mod_p, n_ctx, gmix, w_in_k, qn, kvn, w_uq_k, w_ukv_k, ident_cos, ident_sin, True)
            states["ckv"].append(s_ckv.reshape(batch, seq, MLA_KV_LORA))
            states["krope"].append(s_kr[:, MLA_NOPE_DIM:MLA_QK_DIM].reshape(batch, seq, MLA_ROPE_DIM))
            states["nk"].append(s_nk.reshape(batch, seq, NA_HEADS, NA_HEAD_DIM))
            states["nv"].append(s_nv.reshape(batch, seq, NA_HEADS, NA_HEAD_DIM))
            qs, ks, vs, nqs, nks, nvs = _even_in(
                xs, mod_s, dec_seq, gmix, w_in_k, qn, kvn, w_uq_k, w_ukv_k, cos_m, sin_m, False)
            kr_cache = jnp.pad(cache_mla_krope[:, e],
                               ((0, 0), (0, 0), (MLA_NOPE_DIM, LANES - MLA_QK_DIM)))
            kc, vc = _cache_expand(cache_mla_ckv[:, e], kr_cache, w_ukv_k)

            r3 = lambda a, b_: a.reshape(b_, a.shape[0] // b_, a.shape[1])
            mla_kw = dict(groups=MLA_HEADS // 2, heads=2, k_stride=LANES, q_half_mask=False, pair_out=True)
            na_kw = dict(groups=NA_HEADS // 2, heads=2, k_stride=0, q_half_mask=True, pair_out=True)
            a_mla_p = _ctx_attention(r3(qp, batch), r3(kp, batch), r3(vp, batch), name="mla_ctx", **mla_kw)
            a_na_p = _ctx_attention(r3(nqp, batch), r3(nkp, batch), r3(nvp, batch), name="na_ctx", **na_kw)
            a_mla_s = _attention(r3(qs, dec_batch), [(r3(ks, dec_batch), r3(vs, dec_batch)), (kc, vc)],
                                 q_tile=MLA_Q_TILE, name="mla_lat", **mla_kw)
            bias = _na_bias_tables(na_rpb[e])
            a_na_s = _neighbourhood_attention(
                r3(nqs, dec_batch), r3(nks, dec_batch), r3(nvs, dec_batch),
                cache_na_k[:, e].reshape(dec_batch, past, NA_W).astype(BF16),
                cache_na_v[:, e].reshape(dec_batch, past, NA_W).astype(BF16), bias)
            attn_p = [a_mla_p.reshape(n_ctx, -1), a_na_p.reshape(n_ctx, -1)]
            attn_s = [a_mla_s.reshape(n_lat, -1), a_na_s.reshape(n_lat, -1)]
            wo = w_out_a[e].astype(BF16)
            half = MLA_HEADS * MLA_V_DIM
            w_outs = [wo[:half], wo[half:]]
        else:
            o = l // 2
            w_in_k = w_in_c[o].astype(BF16)
            qn = gqa_q_norm[o] * GQA_SCALE
            qn = jnp.stack([qn, _swap_halves(qn)])
            kn = jnp.stack([gqa_k_norm[o], _swap_halves(gqa_k_norm[o])])
            qp, kp, vp, s_gk, s_gv = _odd_in(xp, mod_p, n_ctx, gmix, w_in_k, qn, kn, ident_cos, ident_sin, True)
            states["gk"].append(s_gk.reshape(batch, seq, GQA_KV_HEADS, GQA_HEAD_DIM))
            states["gv"].append(s_gv.reshape(batch, seq, GQA_KV_HEADS, GQA_HEAD_DIM))
            qs, ks, vs = _odd_in(xs, mod_s, dec_seq, gmix, w_in_k, qn, kn, cos_g, sin_g, False)
            r3 = lambda a, b_: a.reshape(b_, a.shape[0] // b_, a.shape[1])
            gqa_kw = dict(groups=GQA_KV_HEADS, heads=GQA_GROUP, k_stride=0, q_half_mask=False, pair_out=False)
            a_p = _ctx_attention(r3(qp, batch), r3(kp, batch), r3(vp, batch), name="gqa_ctx", **gqa_kw)
            kcache = cache_gqa_k[:, o].reshape(dec_batch, past, -1).astype(BF16)
            vcache = cache_gqa_v[:, o].reshape(dec_batch, past, -1).astype(BF16)
            a_s = _attention(r3(qs, dec_batch), [(r3(ks, dec_batch), r3(vs, dec_batch)), (kcache, vcache)],
                             q_tile=GQA_Q_TILE, name="gqa_lat", **gqa_kw)
            attn_p = [a_p.reshape(n_ctx, -1)]
            attn_s = [a_s.reshape(n_lat, -1)]
            w_outs = [w_out_c[o].astype(BF16)]
        last = l == depth - 1
        xp = _out_ffn(xp, mod_p, n_ctx, attn_p, w_outs, gffn, l, wfi, wfo, gfin, last)
        xs = _out_ffn(xs, mod_s, dec_seq, attn_s, w_outs, gffn, l, wfi, wfo, gfin, last)

    y_prompt = xp.reshape(batch, seq, d)
    y_sample = xs.reshape(dec_batch, dec_seq, d)
    return (y_prompt, y_sample,
            jnp.stack(states["ckv"], axis=1), jnp.stack(states["krope"], axis=1),
            jnp.stack(states["nk"], axis=1), jnp.stack(states["nv"], axis=1),
            jnp.stack(states["gk"], axis=1), jnp.stack(states["gv"], axis=1))
```

```python
import functools
import math

import numpy as np
import jax
import jax.numpy as jnp
from jax import lax
from jax.experimental import pallas as pl
from jax.experimental.pallas import tpu as pltpu

LANES = 128
V7X_VMEM_BYTES = 64 * 1024 * 1024

D_MODEL = 1024
GRID_W = 64
ROPE_THETA = 10000.0
RMS_EPS = 1e-6
NEG_INF = -1e30
MLA_HEADS = 8
MLA_Q_LORA = 256
MLA_KV_LORA = 256
MLA_NOPE_DIM = 64
MLA_ROPE_DIM = 32
MLA_V_DIM = 64
MLA_QK_DIM = MLA_NOPE_DIM + MLA_ROPE_DIM
LOG2E = math.log2(math.e)
MLA_SCALE = MLA_QK_DIM ** -0.5 * LOG2E
NA_HEADS = 8
NA_HEAD_DIM = 64
NA_WIN_H = 8
NA_WIN_W = 16
NA_SCALE = NA_HEAD_DIM ** -0.5 * LOG2E
NA_W = NA_HEADS * NA_HEAD_DIM
GQA_HEADS = 8
GQA_KV_HEADS = 2
GQA_HEAD_DIM = 128
GQA_SCALE = GQA_HEAD_DIM ** -0.5 * LOG2E
GQA_GROUP = GQA_HEADS // GQA_KV_HEADS

TOKEN_TILE = 512
MLA_Q_TILE = 512
GQA_Q_TILE = 512
ATTN_K_CHUNK = 256
NA_Q_ROWS = 4
NA_BAND_ROWS = 12
NA_BLOCKS_PER_STEP = 4

BF16 = jnp.bfloat16
F32 = jnp.float32


def _vmem_limit(nbytes):
    return int(min(V7X_VMEM_BYTES - (4 << 20), max(nbytes, 16 << 20)))


def _params(nbytes, ndims):
    return pltpu.CompilerParams(dimension_semantics=("arbitrary",) * ndims,
                                vmem_limit_bytes=_vmem_limit(nbytes))


def _rms(x, gain):
    return x * lax.rsqrt(jnp.mean(x * x, axis=-1, keepdims=True) + RMS_EPS) * gain


def _dot(a, b):
    return jnp.dot(a, b, preferred_element_type=F32)


def _dot_nt(a, b):
    return lax.dot_general(a, b, (((1,), (1,)), ((), ())), preferred_element_type=F32)


def _store_mla_values(v_ref, index, v):
    ones = jnp.ones((v.shape[0], LANES), BF16)
    for i in range(MLA_HEADS // 2):
        v_ref[index + (slice(None), slice(2 * i * LANES, (2 * i + 1) * LANES))] = (
            v[:, i * LANES:(i + 1) * LANES].astype(BF16))
        v_ref[index + (slice(None), slice((2 * i + 1) * LANES, (2 * i + 2) * LANES))] = ones


def _const_spec(shape):
    nd = len(shape)
    return pl.BlockSpec(shape, lambda *_: (0,) * nd, pipeline_mode=pl.Buffered(1))


def _mod_kernel(cond_ref, w_ref, b_ref, o_ref):
    c = cond_ref[...]
    s = (c * jax.nn.sigmoid(c)).astype(BF16)
    o_ref[0] = _dot(s, w_ref[0].astype(BF16)) + b_ref[0]


def _modulation(cond, w_mod, b_mod):
    depth, d, n = w_mod.shape
    rows = cond.shape[0]
    bn = 1024
    return pl.pallas_call(
        _mod_kernel,
        out_shape=jax.ShapeDtypeStruct((depth, rows, n), F32),
        grid=(depth, n // bn),
        in_specs=[pl.BlockSpec((rows, d), lambda l, j: (0, 0)),
                  pl.BlockSpec((1, d, bn), lambda l, j: (l, 0, j)),
                  pl.BlockSpec((1, 1, bn), lambda l, j: (l, 0, j))],
        out_specs=pl.BlockSpec((1, rows, bn), lambda l, j: (l, 0, j)),
        compiler_params=_params(3 * d * bn * 4, 2),
        name="ada_modulation",
    )(cond, w_mod, b_mod.reshape(depth, 1, n))


def _even_in_kernel(x_ref, mod_ref, gmix_ref, w_in_ref, qn_ref, kvn_ref, w_uq_ref, w_ukv_ref,
                    cos_ref, sin_ref, *out_refs, with_state):
    q_ref, k_ref, v_ref, nq_ref, nk_ref, nv_ref = out_refs[:6]
    x = x_ref[...]
    mod = mod_ref[0]
    h = _rms(x, gmix_ref[...]) * (1.0 + mod[1:2]) + mod[0:1]
    p = _dot(h.astype(BF16), w_in_ref[...])
    cq = p[:, 0:256]
    ckv = _rms(p[:, 256:512], kvn_ref[...])
    nq = p[:, 512:1024]
    nk = p[:, 1024:1536]
    nv = p[:, 1536:2048]
    kr = p[:, 2048:2176]
    kr_sw = p[:, 2176:2304]
    cos = cos_ref[...]
    sin = sin_ref[...]
    qq = _dot(_rms(cq, qn_ref[...]).astype(BF16), w_uq_ref[...])
    kv = _dot(ckv.astype(BF16), w_ukv_ref[...])
    kr_rot = kr * cos + kr_sw * sin
    for hd in range(MLA_HEADS):
        lo = hd * LANES
        qh = qq[:, lo:lo + LANES] * cos + qq[:, 1024 + lo:1024 + lo + LANES] * sin
        q_ref[:, lo:lo + LANES] = qh.astype(BF16)
        k_ref[:, lo:lo + LANES] = (kv[:, lo:lo + LANES] + kr_rot).astype(BF16)
    _store_mla_values(v_ref, (), kv[:, 1024:1536])
    nq_ref[...] = (nq * NA_SCALE).astype(BF16)
    nk_ref[...] = nk.astype(BF16)
    nv_ref[...] = nv.astype(BF16)
    if with_state:
        s_ckv_ref, s_kr_ref, s_nk_ref, s_nv_ref = out_refs[6:]
        s_ckv_ref[...] = ckv
        s_kr_ref[...] = kr
        s_nk_ref[...] = nk
        s_nv_ref[...] = nv


def _even_in(x, mod, tokens_per_group, gmix, w_in, qn, kvn, w_uq, w_ukv, cos, sin, with_state):
    n = x.shape[0]
    tm = TOKEN_TILE
    tiles_per_group = tokens_per_group // tm
    rope_tiles = cos.shape[0] // tm
    row = lambda i: (i, 0)
    outs = [jax.ShapeDtypeStruct((n, 1024), BF16), jax.ShapeDtypeStruct((n, 1024), BF16),
            jax.ShapeDtypeStruct((n, 1024), BF16), jax.ShapeDtypeStruct((n, 512), BF16),
            jax.ShapeDtypeStruct((n, 512), BF16), jax.ShapeDtypeStruct((n, 512), BF16)]
    if with_state:
        outs += [jax.ShapeDtypeStruct((n, 256), F32), jax.ShapeDtypeStruct((n, 128), F32),
                 jax.ShapeDtypeStruct((n, 512), F32), jax.ShapeDtypeStruct((n, 512), F32)]
    return pl.pallas_call(
        functools.partial(_even_in_kernel, with_state=with_state),
        out_shape=outs,
        grid=(n // tm,),
        in_specs=[pl.BlockSpec((tm, D_MODEL), row),
                  pl.BlockSpec((1, 6, D_MODEL), lambda i: (i // tiles_per_group, 0, 0)),
                  _const_spec(gmix.shape), _const_spec(w_in.shape), _const_spec(qn.shape),
                  _const_spec(kvn.shape), _const_spec(w_uq.shape), _const_spec(w_ukv.shape),
                  pl.BlockSpec((tm, LANES), lambda i: (i % rope_tiles, 0)),
                  pl.BlockSpec((tm, LANES), lambda i: (i % rope_tiles, 0))],
        out_specs=[pl.BlockSpec((tm, o.shape[1]), row) for o in outs],
        compiler_params=_params(40 << 20, 1),
        name="even_in",
    )(x, mod, gmix, w_in, qn, kvn, w_uq, w_ukv, cos, sin)


def _cache_expand_kernel(ckv_ref, kr_ref, w_ukv_ref, k_ref, v_ref):
    kv = _dot(ckv_ref[0].astype(BF16), w_ukv_ref[...])
    kr = kr_ref[0]
    for hd in range(MLA_HEADS):
        lo = hd * LANES
        k_ref[0, :, lo:lo + LANES] = (kv[:, lo:lo + LANES] + kr).astype(BF16)
    _store_mla_values(v_ref, (0,), kv[:, 1024:1536])


def _cache_expand(ckv, kr128, w_ukv):
    b, s, _ = ckv.shape
    return pl.pallas_call(
        _cache_expand_kernel,
        out_shape=[jax.ShapeDtypeStruct((b, s, 1024), BF16), jax.ShapeDtypeStruct((b, s, 1024), BF16)],
        grid=(b,),
        in_specs=[pl.BlockSpec((1, s, MLA_KV_LORA), lambda i: (i, 0, 0)),
                  pl.BlockSpec((1, s, LANES), lambda i: (i, 0, 0)),
                  _const_spec(w_ukv.shape)],
        out_specs=[pl.BlockSpec((1, s, 1024), lambda i: (i, 0, 0)),
                   pl.BlockSpec((1, s, 1024), lambda i: (i, 0, 0))],
        compiler_params=_params(16 << 20, 1),
        name="mla_cache_expand",
    )(ckv, kr128, w_ukv)


def _odd_in_kernel(x_ref, mod_ref, gmix_ref, w_in_ref, qn_ref, kn_ref, avg_ref, cos_ref, sin_ref, *out_refs,
                   with_state):
    q_ref, k_ref, v_ref = out_refs[:3]
    x = x_ref[...]
    mod = mod_ref[0]
    h = (_rms(x, gmix_ref[...]) * (1.0 + mod[1:2]) + mod[0:1]).astype(BF16)
    cos = cos_ref[...]
    sin = sin_ref[...]
    half = GQA_HEAD_DIM // 2
    q_cos, q_sin = qn_ref[0:1] * cos, qn_ref[1:2] * sin
    k_cos, k_sin = kn_ref[0:1] * cos, kn_ref[1:2] * sin

    pair = 2 * LANES

    def project(col):
        return _dot(h, w_in_ref[:, col:col + pair])

    def inv_rms(p):
        sq = p * p
        hi = sq.astype(BF16)
        lo = (sq - hi.astype(F32)).astype(BF16)
        return lax.rsqrt(_dot(hi, avg_ref[...]) + _dot(lo, avg_ref[...]) + RMS_EPS)

    k_off = GQA_HEADS * LANES
    v_off = k_off + GQA_KV_HEADS * LANES
    cols = [g * pair for g in range(GQA_HEADS // 2)] + [k_off, v_off]
    p_next = project(cols[0])
    for idx, col in enumerate(cols[:-1]):
        p, p_next = p_next, project(cols[idx + 1])
        r = inv_rms(p)
        for i in range(2):
            ph = p[:, i * LANES:(i + 1) * LANES]
            rh = r[:, i * LANES:(i + 1) * LANES]
            lo = i * LANES
            if col < k_off:
                qh = (ph * q_cos + pltpu.roll(ph, half, 1) * q_sin) * rh
                q_ref[:, col + lo:col + lo + LANES] = qh.astype(BF16)
            else:
                kh = (ph * k_cos + pltpu.roll(ph, half, 1) * k_sin) * rh
                k_ref[:, lo:lo + LANES] = kh.astype(BF16)
                if with_state:
                    out_refs[3][:, lo:lo + LANES] = ph * rh * kn_ref[0:1]
    ones = jnp.ones((x.shape[0], LANES), BF16)
    for hd in range(GQA_KV_HEADS):
        v_ref[:, 2 * hd * LANES:(2 * hd + 1) * LANES] = p_next[:, hd * LANES:(hd + 1) * LANES].astype(BF16)
        v_ref[:, (2 * hd + 1) * LANES:(2 * hd + 2) * LANES] = ones
    if with_state:
        out_refs[4][...] = p_next


def _odd_in(x, mod, tokens_per_group, gmix, w_in, qn, kn, cos, sin, with_state):
    n = x.shape[0]
    tm = TOKEN_TILE
    tiles_per_group = tokens_per_group // tm
    rope_tiles = cos.shape[0] // tm
    row = lambda i: (i, 0)
    outs = [jax.ShapeDtypeStruct((n, 1024), BF16), jax.ShapeDtypeStruct((n, 256), BF16),
            jax.ShapeDtypeStruct((n, 512), BF16)]
    if with_state:
        outs += [jax.ShapeDtypeStruct((n, 256), F32), jax.ShapeDtypeStruct((n, 256), F32)]
    avg = np.kron(np.eye(2), np.full((GQA_HEAD_DIM, GQA_HEAD_DIM), 1.0 / GQA_HEAD_DIM))
    avg = jnp.asarray(avg, BF16)
    return pl.pallas_call(
        functools.partial(_odd_in_kernel, with_state=with_state),
        out_shape=outs,
        grid=(n // tm,),
        in_specs=[pl.BlockSpec((tm, D_MODEL), row),
                  pl.BlockSpec((1, 6, D_MODEL), lambda i: (i // tiles_per_group, 0, 0)),
                  _const_spec(gmix.shape), _const_spec(w_in.shape), _const_spec(qn.shape),
                  _const_spec(kn.shape), _const_spec(avg.shape),
                  pl.BlockSpec((tm, LANES), lambda i: (i % rope_tiles, 0)),
                  pl.BlockSpec((tm, LANES), lambda i: (i % rope_tiles, 0))],
        out_specs=[pl.BlockSpec((tm, o.shape[1]), row) for o in outs],
        compiler_params=_params(32 << 20, 1),
        name="odd_in",
    )(x, mod, gmix, w_in, qn, kn, avg, cos, sin)


def _attn_kernel(*refs, n_src, heads, q_tile, src_len, k_stride, q_half_mask, pair_out):
    q_ref = refs[0]
    kv_refs = refs[1:1 + 2 * n_src]
    o_ref = refs[1 + 2 * n_src]
    s_ref = refs[2 + 2 * n_src]
    n_tiles = q_ref.shape[1] // q_tile
    chunks = []
    for src in range(n_src):
        ck = min(ATTN_K_CHUNK, src_len[src])
        for c in range(src_len[src] // ck):
            chunks.append((src, c * ck, ck))
    lane = lax.broadcasted_iota(jnp.int32, (q_tile, LANES), 1)

    def rows(t):
        if isinstance(t, int):
            return slice(t * q_tile, (t + 1) * q_tile)
        return pl.ds(pl.multiple_of(t * q_tile, q_tile), q_tile)

    def load_q(t, j):
        if q_half_mask:
            qb = q_ref[0, rows(t), :]
            return jnp.where((lane >= 64) == (j == 1), qb, jnp.zeros_like(qb))
        return q_ref[0, rows(t), j * LANES:(j + 1) * LANES]

    def slot(j, q_next, m_prev):
        if q_next is not None:
            m_part = jnp.full((q_tile, LANES), -jnp.inf, F32)
        if m_prev is not None:
            acc = jnp.zeros((q_tile, 2 * LANES), F32)
        off = 0
        for src, k0, ck in chunks:
            if q_next is not None:
                kc = kv_refs[2 * src][0, j * k_stride:j * k_stride + LANES, k0:k0 + ck]
                s_new = _dot(q_next, kc)
            if m_prev is not None:
                s_old = s_ref[:, off:off + ck]
            if q_next is not None:
                s_ref[:, off:off + ck] = s_new
                for i in range(ck // LANES):
                    m_part = jnp.maximum(m_part, s_new[:, i * LANES:(i + 1) * LANES])
            if m_prev is not None:
                p = jnp.exp2(s_old - m_prev)
                acc = acc + _dot(p.astype(BF16), kv_refs[2 * src + 1][0, k0:k0 + ck, :])
            off += ck
        m_next = None if q_next is None else jnp.max(m_part, axis=-1, keepdims=True)
        o_prev = None if m_prev is None else acc[:, :LANES] / acc[:, LANES:]
        return m_next, o_prev

    def write_out(t, j, o):
        o = o.astype(o_ref.dtype)
        if not pair_out:
            o_ref[0, rows(t), j * LANES:(j + 1) * LANES] = o
        elif j == 0:
            o_ref[0, rows(t), :] = o
        else:
            o_ref[0, rows(t), :] = jnp.where(lane < 64, o_ref[0, rows(t), :], o)

    def tile(j, t, m, last):
        m_next, o = slot(j, None if last else load_q(t + 1, j), m)
        write_out(t, j, o)
        return m_next

    for j in range(heads):
        m, _ = slot(j, load_q(0, j), None)
        if n_tiles > 1:
            m = lax.fori_loop(0, n_tiles - 1, lambda t, m, j=j: tile(j, t, m, False), m)
        tile(j, n_tiles - 1, m, True)


def _attention(q, sources, *, groups, heads, k_stride, q_half_mask, pair_out, q_tile, name):
    b, t, _ = q.shape
    q_block = LANES if q_half_mask else heads * LANES
    k_block = LANES if k_stride == 0 else heads * LANES
    out_block = LANES if pair_out else heads * LANES
    src_len = tuple(k.shape[1] for k, _ in sources)
    in_specs = [pl.BlockSpec((1, t, q_block), lambda bi, g: (bi, 0, g))]
    args = [q]
    for k, v in sources:
        s = k.shape[1]
        in_specs.append(pl.BlockSpec((1, k_block, s), lambda bi, g: (bi, g, 0)))
        in_specs.append(pl.BlockSpec((1, s, 2 * LANES), lambda bi, g: (bi, 0, g)))
        args += [jnp.swapaxes(k, 1, 2), v]
    total = sum(src_len)
    return pl.pallas_call(
        functools.partial(_attn_kernel, n_src=len(sources), heads=heads, q_tile=q_tile,
                          src_len=src_len, k_stride=k_stride, q_half_mask=q_half_mask,
                          pair_out=pair_out),
        out_shape=jax.ShapeDtypeStruct((b, t, groups * out_block), BF16),
        grid=(b, groups),
        in_specs=in_specs,
        out_specs=pl.BlockSpec((1, t, out_block), lambda bi, g: (bi, 0, g)),
        scratch_shapes=[pltpu.VMEM((q_tile, total), F32)],
        compiler_params=_params(58 << 20, 2),
        name=name,
    )(*args)


def _ctx_attn_kernel(q_ref, k_ref, v_ref, o_ref, *, heads, k_stride, q_half_mask, pair_out):
    nb, t, _ = q_ref.shape
    lane = lax.broadcasted_iota(jnp.int32, (t, LANES), 1)
    for b in range(nb):
        outs = []
        for j in range(heads):
            if q_half_mask:
                qb = q_ref[b]
                q = jnp.where((lane >= 64) == (j == 1), qb, jnp.zeros_like(qb))
            else:
                q = q_ref[b, :, j * LANES:(j + 1) * LANES]
            s = _dot_nt(q, k_ref[b, :, j * k_stride:j * k_stride + LANES])
            p = jnp.exp2(s - jnp.max(s, axis=-1, keepdims=True))
            acc = _dot(p.astype(BF16), v_ref[b, :, :LANES])
            outs.append(acc / jnp.sum(p, axis=-1, keepdims=True))
        if pair_out:
            o_ref[b] = jnp.where(lane < 64, outs[0], outs[1]).astype(o_ref.dtype)
        else:
            for j in range(heads):
                o_ref[b, :, j * LANES:(j + 1) * LANES] = outs[j].astype(o_ref.dtype)


def _ctx_attention(q, k, v, *, groups, heads, k_stride, q_half_mask, pair_out, name, v_block=LANES):
    b, t, _ = q.shape
    nb = 4
    q_block = LANES if q_half_mask else heads * LANES
    k_block = LANES if k_stride == 0 else heads * LANES
    out_block = LANES if pair_out else heads * LANES
    spec = lambda w: pl.BlockSpec((nb, t, w), lambda bi, g: (bi, 0, g))
    return pl.pallas_call(
        functools.partial(_ctx_attn_kernel, heads=heads, k_stride=k_stride, q_half_mask=q_half_mask,
                          pair_out=pair_out),
        out_shape=jax.ShapeDtypeStruct((b, t, groups * out_block), BF16),
        grid=(b // nb, groups),
        in_specs=[spec(q_block), spec(k_block), spec(v_block)],
        out_specs=spec(out_block),
        compiler_params=_params(32 << 20, 2),
        name=name,
    )(q, k, v)


def _na_bias_tables(rpb):
    n_rows = GRID_W
    h, n_dr, n_dc = rpb.shape
    edge = n_dc - 1 - (NA_WIN_W - 1)
    w = jnp.concatenate([rpb[..., NA_WIN_W - 1:],
                         jnp.broadcast_to(rpb[..., n_dc - 1:], (h, n_dr, GRID_W - 1 - edge)),
                         jnp.broadcast_to(rpb[..., :1], (h, n_dr, GRID_W - (NA_WIN_W - 1) + 1)),
                         rpb[..., 1:NA_WIN_W - 1]], axis=-1)
    toe = jnp.tile(w, (1, 1, GRID_W))[..., :GRID_W * (2 * GRID_W - 1)]
    toe = toe.reshape(h, n_dr, GRID_W, 2 * GRID_W - 1)[..., :GRID_W]
    cols = np.arange(GRID_W)
    cs = np.clip(cols - NA_WIN_W // 2, 0, GRID_W - NA_WIN_W)
    col_ok = (cols[None, :] >= cs[:, None]) & (cols[None, :] < cs[:, None] + NA_WIN_W)
    toe = jnp.where(col_ok, toe * LOG2E, NEG_INF)
    toe = toe.transpose(0, 2, 1, 3).reshape(h, GRID_W, n_dr * GRID_W)
    pieces = []
    for blk in (0, 1, n_rows // NA_Q_ROWS - 1):
        b0 = int(np.clip(NA_Q_ROWS * blk - NA_WIN_H // 2, 0, n_rows - NA_BAND_ROWS))
        for qr in range(NA_Q_ROWS):
            r = NA_Q_ROWS * blk + qr
            rs = int(np.clip(r - NA_WIN_H // 2, 0, n_rows - NA_WIN_H))
            dr0 = rs - r + NA_WIN_H - 1
            seen = toe[:, None, :, dr0 * GRID_W:(dr0 + NA_WIN_H) * GRID_W]
            before = (rs - b0) * GRID_W
            after = (NA_BAND_ROWS - NA_WIN_H) * GRID_W - before
            pieces.append(jnp.pad(seen, ((0, 0), (0, 0), (0, 0), (before, after)), constant_values=NEG_INF))
    return jnp.concatenate(pieces, axis=1).reshape(h, 3, NA_Q_ROWS * GRID_W, NA_BAND_ROWS * GRID_W)


def _na_kernel(q_ref, k_ref, v_ref, kc_ref, vc_ref, bias_ref, o_ref, s_ref):
    nq = NA_Q_ROWS * GRID_W
    nb = NA_BAND_ROWS * GRID_W
    per = NA_BLOCKS_PER_STEP
    t = k_ref.shape[1]
    n_blocks = t // nq
    n_groups = n_blocks // per
    lane = lax.broadcasted_iota(jnp.int32, (nq, LANES), 1)

    def block_rows(i):
        if isinstance(i, int):
            return slice(i * nq, (i + 1) * nq)
        return pl.ds(pl.multiple_of(i * nq, nq), nq)

    def band_rows(i):
        first = nq * i - (NA_WIN_H // 2) * GRID_W
        if isinstance(i, int):
            first = min(max(first, 0), t - nb)
            return slice(first, first + nb)
        return pl.ds(pl.multiple_of(jnp.clip(first, 0, t - nb), nq), nb)

    def bias_class(i):
        if isinstance(i, int):
            return min(i, 1) + max(i - (n_blocks - 2), 0)
        return jnp.minimum(i, 1) + jnp.maximum(i - (n_blocks - 2), 0)

    def slot(j, g_next, g_prev, ms_prev):
        new_ms = []
        for blk in range(per):
            srow = slice(blk * nq, (blk + 1) * nq)
            if g_next is not None:
                i = g_next * per + blk
                qb = q_ref[0, block_rows(i), :]
                q = jnp.where((lane >= 64) == (j == 1), qb, jnp.zeros_like(qb))
                s_band = _dot_nt(q, k_ref[0, band_rows(i), :]) + bias_ref[j, bias_class(i)]
                s_ctx = _dot_nt(q, kc_ref[0])
            if g_prev is not None:
                old_band = s_ref[srow, :nb]
                old_ctx = s_ref[srow, nb:]
            if g_next is not None:
                s_ref[srow, :nb] = s_band
                s_ref[srow, nb:] = s_ctx
                new_ms.append(jnp.maximum(jnp.max(s_band, axis=-1, keepdims=True),
                                          jnp.max(s_ctx, axis=-1, keepdims=True)))
            if g_prev is not None:
                i = g_prev * per + blk
                m = ms_prev[blk]
                p_band = jnp.exp2(old_band - m)
                p_ctx = jnp.exp2(old_ctx - m)
                l = jnp.sum(p_band, axis=-1, keepdims=True) + jnp.sum(p_ctx, axis=-1, keepdims=True)
                acc = (_dot(p_band.astype(BF16), v_ref[0, band_rows(i), :])
                       + _dot(p_ctx.astype(BF16), vc_ref[0]))
                o = (acc / l).astype(o_ref.dtype)
                if j == 0:
                    o_ref[0, block_rows(i), :] = o
                else:
                    o_ref[0, block_rows(i), :] = jnp.where(lane < 64, o_ref[0, block_rows(i), :], o)
        return tuple(new_ms)

    for j in range(2):
        ms = slot(j, 0, None, None)
        if n_groups > 1:
            ms = lax.fori_loop(0, n_groups - 1, lambda g, ms, j=j: slot(j, g + 1, g, ms), ms)
        slot(j, None, n_groups - 1, ms)


def _neighbourhood_attention(q, k, v, kc, vc, bias):
    b, t, w = q.shape
    pairs = w // LANES
    nq = NA_Q_ROWS * GRID_W
    nb = NA_BAND_ROWS * GRID_W
    c = kc.shape[1]
    seq = lambda n: pl.BlockSpec((1, n, LANES), lambda bi, g: (bi, 0, g))
    return pl.pallas_call(
        _na_kernel,
        out_shape=jax.ShapeDtypeStruct((b, t, w), BF16),
        grid=(b, pairs),
        in_specs=[seq(t), seq(t), seq(t), seq(c), seq(c),
                  pl.BlockSpec((2, 3, nq, nb), lambda bi, g: (g, 0, 0, 0))],
        out_specs=seq(t),
        scratch_shapes=[pltpu.VMEM((NA_BLOCKS_PER_STEP * nq, nb + c), F32)],
        compiler_params=_params(40 << 20, 2),
        name="neighbourhood_attention",
    )(q, k, v, kc, vc, bias)


def _out_ffn_kernel(*refs, n_attn, ff_chunk, final_norm):
    x_ref, mod_ref = refs[0], refs[1]
    a_refs = refs[2:2 + n_attn]
    w_refs = refs[2 + n_attn:2 + 2 * n_attn]
    gffn_ref, w_in_ref, w_out_ref, gfin_ref, o_ref = refs[2 + 2 * n_attn:]
    mod = mod_ref[0]
    mix = _dot(a_refs[0][...], w_refs[0][...])
    for a_ref, w_ref in zip(a_refs[1:], w_refs[1:]):
        mix = mix + _dot(a_ref[...], w_ref[...])
    x1 = x_ref[...] + mod[2:3] * mix
    h = (_rms(x1, gffn_ref[...]) * (1.0 + mod[4:5]) + mod[3:4]).astype(BF16)
    d_ff = w_out_ref.shape[1]
    acc = None
    for c in range(d_ff // ff_chunk):
        lo = c * ff_chunk
        gate = _dot(h, w_in_ref[0, :, lo:lo + ff_chunk])
        up = _dot(h, w_in_ref[0, :, d_ff + lo:d_ff + lo + ff_chunk])
        act = (gate * jax.nn.sigmoid(gate) * up).astype(BF16)
        part = _dot(act, w_out_ref[0, lo:lo + ff_chunk, :])
        acc = part if acc is None else acc + part
    x2 = x1 + mod[5:6] * acc
    if final_norm:
        x2 = _rms(x2, gfin_ref[...])
    o_ref[...] = x2


def _out_ffn(x, mod, tokens_per_group, attn, w_outs, gffn, layer, w_ffn_in, w_ffn_out, gfin, final_norm):
    n = x.shape[0]
    tm = TOKEN_TILE
    tiles_per_group = tokens_per_group // tm
    row = lambda i: (i, 0)
    in_specs = [pl.BlockSpec((tm, D_MODEL), row),
                pl.BlockSpec((1, 6, D_MODEL), lambda i: (i // tiles_per_group, 0, 0))]
    in_specs += [pl.BlockSpec((tm, a.shape[1]), row) for a in attn]
    in_specs += [_const_spec(w.shape) for w in w_outs]
    layer_spec = lambda w: pl.BlockSpec((1,) + w.shape[1:], lambda i: (layer, 0, 0),
                                        pipeline_mode=pl.Buffered(1))
    in_specs += [_const_spec(gffn.shape), layer_spec(w_ffn_in), layer_spec(w_ffn_out),
                 _const_spec(gfin.shape)]
    return pl.pallas_call(
        functools.partial(_out_ffn_kernel, n_attn=len(attn), ff_chunk=256, final_norm=final_norm),
        out_shape=jax.ShapeDtypeStruct((n, D_MODEL), F32),
        grid=(n // tm,),
        in_specs=in_specs,
        out_specs=pl.BlockSpec((tm, D_MODEL), row),
        compiler_params=_params(56 << 20, 1),
        name="out_ffn",
    )(x, mod, *attn, *w_outs, gffn, w_ffn_in, w_ffn_out, gfin)


def _rope_tables(n_tokens, rot_dim):
    t = np.arange(n_tokens)
    row = (t // GRID_W).astype(np.float32)
    col = (t % GRID_W).astype(np.float32)
    axis_dim = rot_dim // 2
    inv_freq = np.float32(ROPE_THETA) ** (-np.arange(0, axis_dim, 2, dtype=np.float32) / axis_dim)
    ang = np.concatenate([row[:, None] * inv_freq, col[:, None] * inv_freq], axis=-1).astype(np.float32)
    return np.cos(ang), np.sin(ang)


def _mla_rope_lanes(n_tokens):
    cos, sin = _rope_tables(n_tokens, MLA_ROPE_DIM)
    one = np.ones((n_tokens, MLA_NOPE_DIM), np.float32)
    zero = np.zeros((n_tokens, MLA_NOPE_DIM), np.float32)
    pad1 = np.ones((n_tokens, LANES - MLA_QK_DIM), np.float32)
    pad0 = np.zeros((n_tokens, LANES - MLA_QK_DIM), np.float32)
    return (jnp.asarray(np.concatenate([one, cos, cos, pad1], axis=-1)),
            jnp.asarray(np.concatenate([zero, -sin, sin, pad0], axis=-1)))


def _gqa_rope_lanes(n_tokens):
    cos, sin = _rope_tables(n_tokens, GQA_HEAD_DIM)
    return (jnp.asarray(np.concatenate([cos, cos], axis=-1)),
            jnp.asarray(np.concatenate([-sin, sin], axis=-1)))


def _swap_halves(w):
    half = w.shape[-1] // 2
    return jnp.concatenate([w[..., half:], w[..., :half]], axis=-1)


def _even_weights(w_in, w_uq, w_ukv):
    d = w_in.shape[0]
    i0 = MLA_Q_LORA
    i1 = i0 + MLA_KV_LORA
    i2 = i1 + MLA_ROPE_DIM
    w_kr = w_in[:, i1:i2]
    zl = jnp.zeros((d, MLA_NOPE_DIM), F32)
    zr = jnp.zeros((d, LANES - MLA_QK_DIM), F32)
    w_in_k = jnp.concatenate([w_in[:, :i1], w_in[:, i2:], zl, w_kr, zr, zl, _swap_halves(w_kr), zr],
                             axis=-1).astype(BF16)
    r = w_uq.shape[0]
    uq = w_uq.reshape(r, MLA_HEADS, MLA_QK_DIM)
    zpad = jnp.zeros((r, MLA_HEADS, LANES - MLA_QK_DIM), F32)
    znope = jnp.zeros((r, MLA_HEADS, MLA_NOPE_DIM), F32)
    q_plain = jnp.concatenate([uq, zpad], axis=-1).reshape(r, MLA_HEADS * LANES)
    q_swap = jnp.concatenate([znope, _swap_halves(uq[..., MLA_NOPE_DIM:]), zpad], axis=-1)
    w_uq_k = jnp.concatenate([q_plain, q_swap.reshape(r, MLA_HEADS * LANES)], axis=-1).astype(BF16)
    r = w_ukv.shape[0]
    ukv = w_ukv.reshape(r, MLA_HEADS, MLA_NOPE_DIM + MLA_V_DIM)
    k_pad = jnp.concatenate([ukv[..., :MLA_NOPE_DIM], jnp.zeros((r, MLA_HEADS, LANES - MLA_NOPE_DIM), F32)],
                            axis=-1).reshape(r, MLA_HEADS * LANES)
    v_cat = ukv[..., MLA_NOPE_DIM:].reshape(r, MLA_HEADS * MLA_V_DIM)
    w_ukv_k = jnp.concatenate([k_pad, v_cat], axis=-1).astype(BF16)
    return w_in_k, w_uq_k, w_ukv_k


def kernel(x_prompt, x_sample, cache_mla_ckv, cache_mla_krope, cache_na_k, cache_na_v, cache_gqa_k, cache_gqa_v, c, c_ctx, w_mod, b_mod, norm_mix, norm_ffn, norm_final, w_in_a, mla_q_norm, mla_w_uq, mla_kv_norm, mla_w_ukv, na_rpb, w_out_a, w_in_c, gqa_q_norm, gqa_k_norm, w_out_c, w_ffn_in, w_ffn_out):
    batch, seq, d = x_prompt.shape
    dec_batch, dec_seq, _ = x_sample.shape
    depth = w_mod.shape[0]
    past = cache_mla_ckv.shape[2]
    n_ctx = batch * seq
    n_lat = dec_batch * dec_seq

    cond = jnp.concatenate([c_ctx[None], c, jnp.zeros((8 - 1 - dec_batch, d), F32)], axis=0)
    mod = _modulation(cond, w_mod, b_mod).reshape(depth, 8, 6, d)

    xp = x_prompt.reshape(n_ctx, d)
    xs = x_sample.reshape(n_lat, d)
    cos_m, sin_m = _mla_rope_lanes(dec_seq)
    cos_g, sin_g = _gqa_rope_lanes(dec_seq)
    ident_cos = jnp.ones((TOKEN_TILE, LANES), F32)
    ident_sin = jnp.zeros((TOKEN_TILE, LANES), F32)
    gfin = norm_final.reshape(1, d)
    states = {k: [] for k in ("ckv", "krope", "nk", "nv", "gk", "gv")}
    wfi = w_ffn_in.astype(BF16)
    wfo = w_ffn_out.astype(BF16)

    for l in range(depth):
        mod_p = mod[l, 0:1]
        mod_s = mod[l, 1:1 + dec_batch]
        gmix = norm_mix[l].reshape(1, d)
        gffn = norm_ffn[l].reshape(1, d)
        if l % 2 == 0:
            e = l // 2
            w_in_k, w_uq_k, w_ukv_k = _even_weights(w_in_a[e], mla_w_uq[e], mla_w_ukv[e])
            qn = (mla_q_norm[e] * MLA_SCALE).reshape(1, -1)
            kvn = mla_kv_norm[e].reshape(1, -1)
            (qp, kp, vp, nqp, nkp, nvp, s_ckv, s_kr, s_nk, s_nv) = _even_in(
                xp, mod_p, n_ctx, gmix, w_in_k, qn, kvn, w_uq_k, w_ukv_k, ident_cos, ident_sin, True)
            states["ckv"].append(s_ckv.reshape(batch, seq, MLA_KV_LORA))
            states["krope"].append(s_kr[:, MLA_NOPE_DIM:MLA_QK_DIM].reshape(batch, seq, MLA_ROPE_DIM))
            states["nk"].append(s_nk.reshape(batch, seq, NA_HEADS, NA_HEAD_DIM))
            states["nv"].append(s_nv.reshape(batch, seq, NA_HEADS, NA_HEAD_DIM))
            qs, ks, vs, nqs, nks, nvs = _even_in(
                xs, mod_s, dec_seq, gmix, w_in_k, qn, kvn, w_uq_k, w_ukv_k, cos_m, sin_m, False)
            kr_cache = jnp.pad(cache_mla_krope[:, e],
                               ((0, 0), (0, 0), (MLA_NOPE_DIM, LANES - MLA_QK_DIM)))
            kc, vc = _cache_expand(cache_mla_ckv[:, e], kr_cache, w_ukv_k)

            r3 = lambda a, b_: a.reshape(b_, a.shape[0] // b_, a.shape[1])
            mla_kw = dict(groups=MLA_HEADS // 2, heads=2, k_stride=LANES, q_half_mask=False, pair_out=True)
            na_kw = dict(groups=NA_HEADS // 2, heads=2, k_stride=0, q_half_mask=True, pair_out=True)
            a_mla_p = _ctx_attention(r3(qp, batch), r3(kp, batch), r3(vp, batch), name="mla_ctx",
                                     v_block=2 * LANES, **mla_kw)
            a_na_p = _ctx_attention(r3(nqp, batch), r3(nkp, batch), r3(nvp, batch), name="na_ctx", **na_kw)
            a_mla_s = _attention(r3(qs, dec_batch), [(r3(ks, dec_batch), r3(vs, dec_batch)), (kc, vc)],
                                 q_tile=MLA_Q_TILE, name="mla_lat", **mla_kw)
            bias = _na_bias_tables(na_rpb[e])
            a_na_s = _neighbourhood_attention(
                r3(nqs, dec_batch), r3(nks, dec_batch), r3(nvs, dec_batch),
                cache_na_k[:, e].reshape(dec_batch, past, NA_W).astype(BF16),
                cache_na_v[:, e].reshape(dec_batch, past, NA_W).astype(BF16), bias)
            attn_p = [a_mla_p.reshape(n_ctx, -1), a_na_p.reshape(n_ctx, -1)]
            attn_s = [a_mla_s.reshape(n_lat, -1), a_na_s.reshape(n_lat, -1)]
            wo = w_out_a[e].astype(BF16)
            half = MLA_HEADS * MLA_V_DIM
            w_outs = [wo[:half], wo[half:]]
        else:
            o = l // 2
            w_in_k = w_in_c[o].astype(BF16)
            qn = gqa_q_norm[o] * GQA_SCALE
            qn = jnp.stack([qn, _swap_halves(qn)])
            kn = jnp.stack([gqa_k_norm[o], _swap_halves(gqa_k_norm[o])])
            qp, kp, vp, s_gk, s_gv = _odd_in(xp, mod_p, n_ctx, gmix, w_in_k, qn, kn, ident_cos, ident_sin, True)
            states["gk"].append(s_gk.reshape(batch, seq, GQA_KV_HEADS, GQA_HEAD_DIM))
            states["gv"].append(s_gv.reshape(batch, seq, GQA_KV_HEADS, GQA_HEAD_DIM))
            qs, ks, vs = _odd_in(xs, mod_s, dec_seq, gmix, w_in_k, qn, kn, cos_g, sin_g, False)
            r3 = lambda a, b_: a.reshape(b_, a.shape[0] // b_, a.shape[1])
            gqa_kw = dict(groups=GQA_KV_HEADS, heads=GQA_GROUP, k_stride=0, q_half_mask=False, pair_out=False)
            a_p = _ctx_attention(r3(qp, batch), r3(kp, batch), r3(vp, batch), name="gqa_ctx",
                                 v_block=2 * LANES, **gqa_kw)
            kcache = cache_gqa_k[:, o].reshape(dec_batch, past, -1).astype(BF16)
            vcache = cache_gqa_v[:, o].astype(BF16)
            vcache = jnp.concatenate([vcache, jnp.ones_like(vcache)], axis=-1).reshape(dec_batch, past, -1)
            a_s = _attention(r3(qs, dec_batch), [(r3(ks, dec_batch), r3(vs, dec_batch)), (kcache, vcache)],
                             q_tile=GQA_Q_TILE, name="gqa_lat", **gqa_kw)
            attn_p = [a_p.reshape(n_ctx, -1)]
            attn_s = [a_s.reshape(n_lat, -1)]
            w_outs = [w_out_c[o].astype(BF16)]
        last = l == depth - 1
        xp = _out_ffn(xp, mod_p, n_ctx, attn_p, w_outs, gffn, l, wfi, wfo, gfin, last)
        xs = _out_ffn(xs, mod_s, dec_seq, attn_s, w_outs, gffn, l, wfi, wfo, gfin, last)

    y_prompt = xp.reshape(batch, seq, d)
    y_sample = xs.reshape(dec_batch, dec_seq, d)
    return (y_prompt, y_sample,
            jnp.stack(states["ckv"], axis=1), jnp.stack(states["krope"], axis=1),
            jnp.stack(states["nk"], axis=1), jnp.stack(states["nv"], axis=1),
            jnp.stack(states["gk"], axis=1), jnp.stack(states["gv"], axis=1))
```

```python
import functools
import math

import numpy as np
import jax
import jax.numpy as jnp
from jax import lax
from jax.experimental import pallas as pl
from jax.experimental.pallas import tpu as pltpu

LANES = 128
V7X_VMEM_BYTES = 64 * 1024 * 1024

D_MODEL = 1024
GRID_W = 64
ROPE_THETA = 10000.0
RMS_EPS = 1e-6
NEG_INF = -1e30
MLA_HEADS = 8
MLA_Q_LORA = 256
MLA_KV_LORA = 256
MLA_NOPE_DIM = 64
MLA_ROPE_DIM = 32
MLA_V_DIM = 64
MLA_QK_DIM = MLA_NOPE_DIM + MLA_ROPE_DIM
LOG2E = math.log2(math.e)
MLA_SCALE = MLA_QK_DIM ** -0.5 * LOG2E
NA_HEADS = 8
NA_HEAD_DIM = 64
NA_WIN_H = 8
NA_WIN_W = 16
NA_SCALE = NA_HEAD_DIM ** -0.5 * LOG2E
NA_W = NA_HEADS * NA_HEAD_DIM
GQA_HEADS = 8
GQA_KV_HEADS = 2
GQA_HEAD_DIM = 128
GQA_SCALE = GQA_HEAD_DIM ** -0.5 * LOG2E
GQA_GROUP = GQA_HEADS // GQA_KV_HEADS

TOKEN_TILE = 512
MLA_Q_TILE = 1024
GQA_Q_TILE = 512
ATTN_K_CHUNK = 256
NA_Q_ROWS = 4
NA_BAND_ROWS = 12
NA_BLOCKS_PER_STEP = 4

BF16 = jnp.bfloat16
F32 = jnp.float32


def _vmem_limit(nbytes):
    return int(min(V7X_VMEM_BYTES - (4 << 20), max(nbytes, 16 << 20)))


def _params(nbytes, ndims):
    return pltpu.CompilerParams(dimension_semantics=("arbitrary",) * ndims,
                                vmem_limit_bytes=_vmem_limit(nbytes))


def _rms(x, gain):
    return x * lax.rsqrt(jnp.mean(x * x, axis=-1, keepdims=True) + RMS_EPS) * gain


def _dot(a, b):
    return jnp.dot(a, b, preferred_element_type=F32)


def _dot_nt(a, b):
    return lax.dot_general(a, b, (((1,), (1,)), ((), ())), preferred_element_type=F32)


def _store_pairs_with_ones(v_ref, index, v):
    ones = jnp.ones((v.shape[0], LANES), BF16)
    for i in range(v.shape[1] // LANES):
        v_ref[index + (slice(None), slice(2 * i * LANES, (2 * i + 1) * LANES))] = (
            v[:, i * LANES:(i + 1) * LANES].astype(BF16))
        v_ref[index + (slice(None), slice((2 * i + 1) * LANES, (2 * i + 2) * LANES))] = ones


def _const_spec(shape):
    nd = len(shape)
    return pl.BlockSpec(shape, lambda *_: (0,) * nd, pipeline_mode=pl.Buffered(1))


def _mod_kernel(cond_ref, w_ref, b_ref, o_ref):
    c = cond_ref[...]
    s = (c * jax.nn.sigmoid(c)).astype(BF16)
    o_ref[0] = _dot(s, w_ref[0].astype(BF16)) + b_ref[0]


def _modulation(cond, w_mod, b_mod):
    depth, d, n = w_mod.shape
    rows = cond.shape[0]
    bn = 1024
    return pl.pallas_call(
        _mod_kernel,
        out_shape=jax.ShapeDtypeStruct((depth, rows, n), F32),
        grid=(depth, n // bn),
        in_specs=[pl.BlockSpec((rows, d), lambda l, j: (0, 0)),
                  pl.BlockSpec((1, d, bn), lambda l, j: (l, 0, j)),
                  pl.BlockSpec((1, 1, bn), lambda l, j: (l, 0, j))],
        out_specs=pl.BlockSpec((1, rows, bn), lambda l, j: (l, 0, j)),
        compiler_params=_params(3 * d * bn * 4, 2),
        name="ada_modulation",
    )(cond, w_mod, b_mod.reshape(depth, 1, n))


def _even_in_kernel(x_ref, mod_ref, gmix_ref, w_in_ref, qn_ref, kvn_ref, w_uq_ref, w_ukv_ref,
                    cos_ref, sin_ref, *out_refs, with_state):
    q_ref, k_ref, v_ref, nq_ref, nk_ref, nv_ref = out_refs[:6]
    x = x_ref[...]
    mod = mod_ref[0]
    h = _rms(x, gmix_ref[...]) * (1.0 + mod[1:2]) + mod[0:1]
    p = _dot(h.astype(BF16), w_in_ref[...])
    cq = p[:, 0:256]
    ckv = _rms(p[:, 256:512], kvn_ref[...])
    nq = p[:, 512:1024]
    nk = p[:, 1024:1536]
    nv = p[:, 1536:2048]
    kr = p[:, 2048:2176]
    kr_sw = p[:, 2176:2304]
    cos = cos_ref[...]
    sin = sin_ref[...]
    qq = _dot(_rms(cq, qn_ref[...]).astype(BF16), w_uq_ref[...])
    kv = _dot(ckv.astype(BF16), w_ukv_ref[...])
    kr_rot = kr * cos + kr_sw * sin
    for hd in range(MLA_HEADS):
        lo = hd * LANES
        qh = qq[:, lo:lo + LANES] * cos + qq[:, 1024 + lo:1024 + lo + LANES] * sin
        q_ref[:, lo:lo + LANES] = qh.astype(BF16)
        k_ref[:, lo:lo + LANES] = (kv[:, lo:lo + LANES] + kr_rot).astype(BF16)
    _store_pairs_with_ones(v_ref, (), kv[:, 1024:1536])
    nq_ref[...] = (nq * NA_SCALE).astype(BF16)
    nk_ref[...] = nk.astype(BF16)
    _store_pairs_with_ones(nv_ref, (), nv)
    if with_state:
        s_ckv_ref, s_kr_ref, s_nk_ref, s_nv_ref = out_refs[6:]
        s_ckv_ref[...] = ckv
        s_kr_ref[...] = kr
        s_nk_ref[...] = nk
        s_nv_ref[...] = nv


def _even_in(x, mod, tokens_per_group, gmix, w_in, qn, kvn, w_uq, w_ukv, cos, sin, with_state):
    n = x.shape[0]
    tm = TOKEN_TILE
    tiles_per_group = tokens_per_group // tm
    rope_tiles = cos.shape[0] // tm
    row = lambda i: (i, 0)
    outs = [jax.ShapeDtypeStruct((n, 1024), BF16), jax.ShapeDtypeStruct((n, 1024), BF16),
            jax.ShapeDtypeStruct((n, 1024), BF16), jax.ShapeDtypeStruct((n, 512), BF16),
            jax.ShapeDtypeStruct((n, 512), BF16), jax.ShapeDtypeStruct((n, 1024), BF16)]
    if with_state:
        outs += [jax.ShapeDtypeStruct((n, 256), F32), jax.ShapeDtypeStruct((n, 128), F32),
                 jax.ShapeDtypeStruct((n, 512), F32), jax.ShapeDtypeStruct((n, 512), F32)]
    return pl.pallas_call(
        functools.partial(_even_in_kernel, with_state=with_state),
        out_shape=outs,
        grid=(n // tm,),
        in_specs=[pl.BlockSpec((tm, D_MODEL), row),
                  pl.BlockSpec((1, 6, D_MODEL), lambda i: (i // tiles_per_group, 0, 0)),
                  _const_spec(gmix.shape), _const_spec(w_in.shape), _const_spec(qn.shape),
                  _const_spec(kvn.shape), _const_spec(w_uq.shape), _const_spec(w_ukv.shape),
                  pl.BlockSpec((tm, LANES), lambda i: (i % rope_tiles, 0)),
                  pl.BlockSpec((tm, LANES), lambda i: (i % rope_tiles, 0))],
        out_specs=[pl.BlockSpec((tm, o.shape[1]), row) for o in outs],
        compiler_params=_params(40 << 20, 1),
        name="even_in",
    )(x, mod, gmix, w_in, qn, kvn, w_uq, w_ukv, cos, sin)


def _cache_expand_kernel(ckv_ref, kr_ref, w_ukv_ref, k_ref, v_ref):
    kv = _dot(ckv_ref[0].astype(BF16), w_ukv_ref[...])
    kr = kr_ref[0]
    for hd in range(MLA_HEADS):
        lo = hd * LANES
        k_ref[0, :, lo:lo + LANES] = (kv[:, lo:lo + LANES] + kr).astype(BF16)
    _store_pairs_with_ones(v_ref, (0,), kv[:, 1024:1536])


def _cache_expand(ckv, kr128, w_ukv):
    b, s, _ = ckv.shape
    return pl.pallas_call(
        _cache_expand_kernel,
        out_shape=[jax.ShapeDtypeStruct((b, s, 1024), BF16), jax.ShapeDtypeStruct((b, s, 1024), BF16)],
        grid=(b,),
        in_specs=[pl.BlockSpec((1, s, MLA_KV_LORA), lambda i: (i, 0, 0)),
                  pl.BlockSpec((1, s, LANES), lambda i: (i, 0, 0)),
                  _const_spec(w_ukv.shape)],
        out_specs=[pl.BlockSpec((1, s, 1024), lambda i: (i, 0, 0)),
                   pl.BlockSpec((1, s, 1024), lambda i: (i, 0, 0))],
        compiler_params=_params(16 << 20, 1),
        name="mla_cache_expand",
    )(ckv, kr128, w_ukv)


def _odd_in_kernel(x_ref, mod_ref, gmix_ref, w_in_ref, qn_ref, kn_ref, avg_ref, cos_ref, sin_ref, *out_refs,
                   with_state):
    q_ref, k_ref, v_ref = out_refs[:3]
    x = x_ref[...]
    mod = mod_ref[0]
    h = (_rms(x, gmix_ref[...]) * (1.0 + mod[1:2]) + mod[0:1]).astype(BF16)
    cos = cos_ref[...]
    sin = sin_ref[...]
    half = GQA_HEAD_DIM // 2
    q_cos, q_sin = qn_ref[0:1] * cos, qn_ref[1:2] * sin
    k_cos, k_sin = kn_ref[0:1] * cos, kn_ref[1:2] * sin

    pair = 2 * LANES

    def project(col):
        return _dot(h, w_in_ref[:, col:col + pair])

    def inv_rms(p):
        sq = p * p
        hi = sq.astype(BF16)
        lo = (sq - hi.astype(F32)).astype(BF16)
        return lax.rsqrt(_dot(hi, avg_ref[...]) + _dot(lo, avg_ref[...]) + RMS_EPS)

    k_off = GQA_HEADS * LANES
    v_off = k_off + GQA_KV_HEADS * LANES
    cols = [g * pair for g in range(GQA_HEADS // 2)] + [k_off, v_off]
    p_next = project(cols[0])
    for idx, col in enumerate(cols[:-1]):
        p, p_next = p_next, project(cols[idx + 1])
        r = inv_rms(p)
        for i in range(2):
            ph = p[:, i * LANES:(i + 1) * LANES]
            rh = r[:, i * LANES:(i + 1) * LANES]
            lo = i * LANES
            if col < k_off:
                qh = (ph * q_cos + pltpu.roll(ph, half, 1) * q_sin) * rh
                q_ref[:, col + lo:col + lo + LANES] = qh.astype(BF16)
            else:
                kh = (ph * k_cos + pltpu.roll(ph, half, 1) * k_sin) * rh
                k_ref[:, lo:lo + LANES] = kh.astype(BF16)
                if with_state:
                    out_refs[3][:, lo:lo + LANES] = ph * rh * kn_ref[0:1]
    ones = jnp.ones((x.shape[0], LANES), BF16)
    for hd in range(GQA_KV_HEADS):
        v_ref[:, 2 * hd * LANES:(2 * hd + 1) * LANES] = p_next[:, hd * LANES:(hd + 1) * LANES].astype(BF16)
        v_ref[:, (2 * hd + 1) * LANES:(2 * hd + 2) * LANES] = ones
    if with_state:
        out_refs[4][...] = p_next


def _odd_in(x, mod, tokens_per_group, gmix, w_in, qn, kn, cos, sin, with_state):
    n = x.shape[0]
    tm = TOKEN_TILE
    tiles_per_group = tokens_per_group // tm
    rope_tiles = cos.shape[0] // tm
    row = lambda i: (i, 0)
    outs = [jax.ShapeDtypeStruct((n, 1024), BF16), jax.ShapeDtypeStruct((n, 256), BF16),
            jax.ShapeDtypeStruct((n, 512), BF16)]
    if with_state:
        outs += [jax.ShapeDtypeStruct((n, 256), F32), jax.ShapeDtypeStruct((n, 256), F32)]
    avg = np.kron(np.eye(2), np.full((GQA_HEAD_DIM, GQA_HEAD_DIM), 1.0 / GQA_HEAD_DIM))
    avg = jnp.asarray(avg, BF16)
    return pl.pallas_call(
        functools.partial(_odd_in_kernel, with_state=with_state),
        out_shape=outs,
        grid=(n // tm,),
        in_specs=[pl.BlockSpec((tm, D_MODEL), row),
                  pl.BlockSpec((1, 6, D_MODEL), lambda i: (i // tiles_per_group, 0, 0)),
                  _const_spec(gmix.shape), _const_spec(w_in.shape), _const_spec(qn.shape),
                  _const_spec(kn.shape), _const_spec(avg.shape),
                  pl.BlockSpec((tm, LANES), lambda i: (i % rope_tiles, 0)),
                  pl.BlockSpec((tm, LANES), lambda i: (i % rope_tiles, 0))],
        out_specs=[pl.BlockSpec((tm, o.shape[1]), row) for o in outs],
        compiler_params=_params(32 << 20, 1),
        name="odd_in",
    )(x, mod, gmix, w_in, qn, kn, avg, cos, sin)


def _attn_kernel(*refs, n_src, heads, q_tile, src_len, k_stride, q_half_mask, pair_out):
    q_ref = refs[0]
    kv_refs = refs[1:1 + 2 * n_src]
    o_ref = refs[1 + 2 * n_src]
    s_ref = refs[2 + 2 * n_src]
    n_tiles = q_ref.shape[1] // q_tile
    chunks = []
    for src in range(n_src):
        ck = min(ATTN_K_CHUNK, src_len[src])
        for c in range(src_len[src] // ck):
            chunks.append((src, c * ck, ck))
    lane = lax.broadcasted_iota(jnp.int32, (q_tile, LANES), 1)

    def rows(t):
        if isinstance(t, int):
            return slice(t * q_tile, (t + 1) * q_tile)
        return pl.ds(pl.multiple_of(t * q_tile, q_tile), q_tile)

    def load_q(t, j):
        if q_half_mask:
            qb = q_ref[0, rows(t), :]
            return jnp.where((lane >= 64) == (j == 1), qb, jnp.zeros_like(qb))
        return q_ref[0, rows(t), j * LANES:(j + 1) * LANES]

    def slot(j, q_next, m_prev):
        if q_next is not None:
            m_part = jnp.full((q_tile, LANES), -jnp.inf, F32)
        if m_prev is not None:
            acc = jnp.zeros((q_tile, 2 * LANES), F32)
        off = 0
        for src, k0, ck in chunks:
            if q_next is not None:
                kc = kv_refs[2 * src][0, j * k_stride:j * k_stride + LANES, k0:k0 + ck]
                s_new = _dot(q_next, kc)
            if m_prev is not None:
                s_old = s_ref[:, off:off + ck]
            if q_next is not None:
                s_ref[:, off:off + ck] = s_new
                for i in range(ck // LANES):
                    m_part = jnp.maximum(m_part, s_new[:, i * LANES:(i + 1) * LANES])
            if m_prev is not None:
                p = jnp.exp2(s_old - m_prev)
                acc = acc + _dot(p.astype(BF16), kv_refs[2 * src + 1][0, k0:k0 + ck, :])
            off += ck
        m_next = None if q_next is None else jnp.max(m_part, axis=-1, keepdims=True)
        o_prev = None if m_prev is None else acc[:, :LANES] / acc[:, LANES:]
        return m_next, o_prev

    def write_out(t, j, o):
        o = o.astype(o_ref.dtype)
        if not pair_out:
            o_ref[0, rows(t), j * LANES:(j + 1) * LANES] = o
        elif j == 0:
            o_ref[0, rows(t), :] = o
        else:
            o_ref[0, rows(t), :] = jnp.where(lane < 64, o_ref[0, rows(t), :], o)

    def tile(j, t, m, last):
        m_next, o = slot(j, None if last else load_q(t + 1, j), m)
        write_out(t, j, o)
        return m_next

    for j in range(heads):
        m, _ = slot(j, load_q(0, j), None)
        if n_tiles > 1:
            m = lax.fori_loop(0, n_tiles - 1, lambda t, m, j=j: tile(j, t, m, False), m)
        tile(j, n_tiles - 1, m, True)


def _attention(q, sources, *, groups, heads, k_stride, q_half_mask, pair_out, q_tile, name):
    b, t, _ = q.shape
    q_block = LANES if q_half_mask else heads * LANES
    k_block = LANES if k_stride == 0 else heads * LANES
    out_block = LANES if pair_out else heads * LANES
    src_len = tuple(k.shape[1] for k, _ in sources)
    in_specs = [pl.BlockSpec((1, t, q_block), lambda bi, g: (bi, 0, g))]
    args = [q]
    for k, v in sources:
        s = k.shape[1]
        in_specs.append(pl.BlockSpec((1, k_block, s), lambda bi, g: (bi, g, 0)))
        in_specs.append(pl.BlockSpec((1, s, 2 * LANES), lambda bi, g: (bi, 0, g)))
        args += [jnp.swapaxes(k, 1, 2), v]
    total = sum(src_len)
    return pl.pallas_call(
        functools.partial(_attn_kernel, n_src=len(sources), heads=heads, q_tile=q_tile,
                          src_len=src_len, k_stride=k_stride, q_half_mask=q_half_mask,
                          pair_out=pair_out),
        out_shape=jax.ShapeDtypeStruct((b, t, groups * out_block), BF16),
        grid=(b, groups),
        in_specs=in_specs,
        out_specs=pl.BlockSpec((1, t, out_block), lambda bi, g: (bi, 0, g)),
        scratch_shapes=[pltpu.VMEM((q_tile, total), F32)],
        compiler_params=_params(58 << 20, 2),
        name=name,
    )(*args)


def _ctx_attn_kernel(q_ref, k_ref, v_ref, o_ref, *, heads, k_stride, q_half_mask, pair_out):
    nb, t, _ = q_ref.shape
    lane = lax.broadcasted_iota(jnp.int32, (t, LANES), 1)
    for b in range(nb):
        outs = []
        for j in range(heads):
            if q_half_mask:
                qb = q_ref[b]
                q = jnp.where((lane >= 64) == (j == 1), qb, jnp.zeros_like(qb))
            else:
                q = q_ref[b, :, j * LANES:(j + 1) * LANES]
            s = _dot_nt(q, k_ref[b, :, j * k_stride:j * k_stride + LANES])
            p = jnp.exp2(s - jnp.max(s, axis=-1, keepdims=True))
            acc = _dot(p.astype(BF16), v_ref[b, :, :LANES])
            outs.append(acc / jnp.sum(p, axis=-1, keepdims=True))
        if pair_out:
            o_ref[b] = jnp.where(lane < 64, outs[0], outs[1]).astype(o_ref.dtype)
        else:
            for j in range(heads):
                o_ref[b, :, j * LANES:(j + 1) * LANES] = outs[j].astype(o_ref.dtype)


def _ctx_attention(q, k, v, *, groups, heads, k_stride, q_half_mask, pair_out, name, v_block=LANES):
    b, t, _ = q.shape
    nb = 4
    q_block = LANES if q_half_mask else heads * LANES
    k_block = LANES if k_stride == 0 else heads * LANES
    out_block = LANES if pair_out else heads * LANES
    spec = lambda w: pl.BlockSpec((nb, t, w), lambda bi, g: (bi, 0, g))
    return pl.pallas_call(
        functools.partial(_ctx_attn_kernel, heads=heads, k_stride=k_stride, q_half_mask=q_half_mask,
                          pair_out=pair_out),
        out_shape=jax.ShapeDtypeStruct((b, t, groups * out_block), BF16),
        grid=(b // nb, groups),
        in_specs=[spec(q_block), spec(k_block), spec(v_block)],
        out_specs=spec(out_block),
        compiler_params=_params(32 << 20, 2),
        name=name,
    )(q, k, v)


def _na_bias_tables(rpb):
    n_rows = GRID_W
    h, n_dr, n_dc = rpb.shape
    edge = n_dc - 1 - (NA_WIN_W - 1)
    w = jnp.concatenate([rpb[..., NA_WIN_W - 1:],
                         jnp.broadcast_to(rpb[..., n_dc - 1:], (h, n_dr, GRID_W - 1 - edge)),
                         jnp.broadcast_to(rpb[..., :1], (h, n_dr, GRID_W - (NA_WIN_W - 1) + 1)),
                         rpb[..., 1:NA_WIN_W - 1]], axis=-1)
    toe = jnp.tile(w, (1, 1, GRID_W))[..., :GRID_W * (2 * GRID_W - 1)]
    toe = toe.reshape(h, n_dr, GRID_W, 2 * GRID_W - 1)[..., :GRID_W]
    cols = np.arange(GRID_W)
    cs = np.clip(cols - NA_WIN_W // 2, 0, GRID_W - NA_WIN_W)
    col_ok = (cols[None, :] >= cs[:, None]) & (cols[None, :] < cs[:, None] + NA_WIN_W)
    toe = jnp.where(col_ok, toe * LOG2E, NEG_INF)
    toe = toe.transpose(0, 2, 1, 3).reshape(h, GRID_W, n_dr * GRID_W)
    pieces = []
    for blk in (0, 1, n_rows // NA_Q_ROWS - 1):
        b0 = int(np.clip(NA_Q_ROWS * blk - NA_WIN_H // 2, 0, n_rows - NA_BAND_ROWS))
        for qr in range(NA_Q_ROWS):
            r = NA_Q_ROWS * blk + qr
            rs = int(np.clip(r - NA_WIN_H // 2, 0, n_rows - NA_WIN_H))
            dr0 = rs - r + NA_WIN_H - 1
            seen = toe[:, None, :, dr0 * GRID_W:(dr0 + NA_WIN_H) * GRID_W]
            before = (rs - b0) * GRID_W
            after = (NA_BAND_ROWS - NA_WIN_H) * GRID_W - before
            pieces.append(jnp.pad(seen, ((0, 0), (0, 0), (0, 0), (before, after)), constant_values=NEG_INF))
    return jnp.concatenate(pieces, axis=1).reshape(h, 3, NA_Q_ROWS * GRID_W, NA_BAND_ROWS * GRID_W)


def _na_kernel(q_ref, k_ref, v_ref, kc_ref, vc_ref, bias_ref, o_ref, s_ref):
    nq = NA_Q_ROWS * GRID_W
    nb = NA_BAND_ROWS * GRID_W
    per = NA_BLOCKS_PER_STEP
    t = k_ref.shape[1]
    n_blocks = t // nq
    n_groups = n_blocks // per
    lane = lax.broadcasted_iota(jnp.int32, (nq, LANES), 1)

    def block_rows(i):
        if isinstance(i, int):
            return slice(i * nq, (i + 1) * nq)
        return pl.ds(pl.multiple_of(i * nq, nq), nq)

    def band_rows(i):
        first = nq * i - (NA_WIN_H // 2) * GRID_W
        if isinstance(i, int):
            first = min(max(first, 0), t - nb)
            return slice(first, first + nb)
        return pl.ds(pl.multiple_of(jnp.clip(first, 0, t - nb), nq), nb)

    def bias_class(i):
        if isinstance(i, int):
            return min(i, 1) + max(i - (n_blocks - 2), 0)
        return jnp.minimum(i, 1) + jnp.maximum(i - (n_blocks - 2), 0)

    def slot(j, g_next, g_prev, ms_prev):
        new_ms = []
        for blk in range(per):
            srow = slice(blk * nq, (blk + 1) * nq)
            if g_next is not None:
                i = g_next * per + blk
                qb = q_ref[0, block_rows(i), :]
                q = jnp.where((lane >= 64) == (j == 1), qb, jnp.zeros_like(qb))
                s_band = _dot_nt(q, k_ref[0, band_rows(i), :]) + bias_ref[j, bias_class(i)]
                s_ctx = _dot_nt(q, kc_ref[0])
            if g_prev is not None:
                old_band = s_ref[srow, :nb]
                old_ctx = s_ref[srow, nb:]
            if g_next is not None:
                s_ref[srow, :nb] = s_band
                s_ref[srow, nb:] = s_ctx
                new_ms.append(jnp.maximum(jnp.max(s_band, axis=-1, keepdims=True),
                                          jnp.max(s_ctx, axis=-1, keepdims=True)))
            if g_prev is not None:
                i = g_prev * per + blk
                m = ms_prev[blk]
                p_band = jnp.exp2(old_band - m)
                p_ctx = jnp.exp2(old_ctx - m)
                acc = (_dot(p_band.astype(BF16), v_ref[0, band_rows(i), :])
                       + _dot(p_ctx.astype(BF16), vc_ref[0]))
                o = (acc[:, :LANES] / acc[:, LANES:]).astype(o_ref.dtype)
                if j == 0:
                    o_ref[0, block_rows(i), :] = o
                else:
                    o_ref[0, block_rows(i), :] = jnp.where(lane < 64, o_ref[0, block_rows(i), :], o)
        return tuple(new_ms)

    for j in range(2):
        ms = slot(j, 0, None, None)
        if n_groups > 1:
            ms = lax.fori_loop(0, n_groups - 1, lambda g, ms, j=j: slot(j, g + 1, g, ms), ms)
        slot(j, None, n_groups - 1, ms)


def _neighbourhood_attention(q, k, v, kc, vc, bias):
    b, t, w = q.shape
    pairs = w // LANES
    nq = NA_Q_ROWS * GRID_W
    nb = NA_BAND_ROWS * GRID_W
    c = kc.shape[1]
    seq = lambda n, lanes=LANES: pl.BlockSpec((1, n, lanes), lambda bi, g: (bi, 0, g))
    return pl.pallas_call(
        _na_kernel,
        out_shape=jax.ShapeDtypeStruct((b, t, w), BF16),
        grid=(b, pairs),
        in_specs=[seq(t), seq(t), seq(t, 2 * LANES), seq(c), seq(c, 2 * LANES),
                  pl.BlockSpec((2, 3, nq, nb), lambda bi, g: (g, 0, 0, 0))],
        out_specs=seq(t),
        scratch_shapes=[pltpu.VMEM((NA_BLOCKS_PER_STEP * nq, nb + c), F32)],
        compiler_params=_params(40 << 20, 2),
        name="neighbourhood_attention",
    )(q, k, v, kc, vc, bias)


def _out_ffn_kernel(*refs, n_attn, ff_chunk, final_norm):
    x_ref, mod_ref = refs[0], refs[1]
    a_refs = refs[2:2 + n_attn]
    w_refs = refs[2 + n_attn:2 + 2 * n_attn]
    gffn_ref, w_in_ref, w_out_ref, gfin_ref, o_ref = refs[2 + 2 * n_attn:]
    mod = mod_ref[0]
    mix = _dot(a_refs[0][...], w_refs[0][...])
    for a_ref, w_ref in zip(a_refs[1:], w_refs[1:]):
        mix = mix + _dot(a_ref[...], w_ref[...])
    x1 = x_ref[...] + mod[2:3] * mix
    h = (_rms(x1, gffn_ref[...]) * (1.0 + mod[4:5]) + mod[3:4]).astype(BF16)
    d_ff = w_out_ref.shape[1]
    acc = None
    for c in range(d_ff // ff_chunk):
        lo = c * ff_chunk
        gate = _dot(h, w_in_ref[0, :, lo:lo + ff_chunk])
        up = _dot(h, w_in_ref[0, :, d_ff + lo:d_ff + lo + ff_chunk])
        act = (gate * jax.nn.sigmoid(gate) * up).astype(BF16)
        part = _dot(act, w_out_ref[0, lo:lo + ff_chunk, :])
        acc = part if acc is None else acc + part
    x2 = x1 + mod[5:6] * acc
    if final_norm:
        x2 = _rms(x2, gfin_ref[...])
    o_ref[...] = x2


def _out_ffn(x, mod, tokens_per_group, attn, w_outs, gffn, layer, w_ffn_in, w_ffn_out, gfin, final_norm):
    n = x.shape[0]
    tm = TOKEN_TILE
    tiles_per_group = tokens_per_group // tm
    row = lambda i: (i, 0)
    in_specs = [pl.BlockSpec((tm, D_MODEL), row),
                pl.BlockSpec((1, 6, D_MODEL), lambda i: (i // tiles_per_group, 0, 0))]
    in_specs += [pl.BlockSpec((tm, a.shape[1]), row) for a in attn]
    in_specs += [_const_spec(w.shape) for w in w_outs]
    layer_spec = lambda w: pl.BlockSpec((1,) + w.shape[1:], lambda i: (layer, 0, 0),
                                        pipeline_mode=pl.Buffered(1))
    in_specs += [_const_spec(gffn.shape), layer_spec(w_ffn_in), layer_spec(w_ffn_out),
                 _const_spec(gfin.shape)]
    return pl.pallas_call(
        functools.partial(_out_ffn_kernel, n_attn=len(attn), ff_chunk=256, final_norm=final_norm),
        out_shape=jax.ShapeDtypeStruct((n, D_MODEL), F32),
        grid=(n // tm,),
        in_specs=in_specs,
        out_specs=pl.BlockSpec((tm, D_MODEL), row),
        compiler_params=_params(56 << 20, 1),
        name="out_ffn",
    )(x, mod, *attn, *w_outs, gffn, w_ffn_in, w_ffn_out, gfin)


def _rope_tables(n_tokens, rot_dim):
    t = np.arange(n_tokens)
    row = (t // GRID_W).astype(np.float32)
    col = (t % GRID_W).astype(np.float32)
    axis_dim = rot_dim // 2
    inv_freq = np.float32(ROPE_THETA) ** (-np.arange(0, axis_dim, 2, dtype=np.float32) / axis_dim)
    ang = np.concatenate([row[:, None] * inv_freq, col[:, None] * inv_freq], axis=-1).astype(np.float32)
    return np.cos(ang), np.sin(ang)


def _mla_rope_lanes(n_tokens):
    cos, sin = _rope_tables(n_tokens, MLA_ROPE_DIM)
    one = np.ones((n_tokens, MLA_NOPE_DIM), np.float32)
    zero = np.zeros((n_tokens, MLA_NOPE_DIM), np.float32)
    pad1 = np.ones((n_tokens, LANES - MLA_QK_DIM), np.float32)
    pad0 = np.zeros((n_tokens, LANES - MLA_QK_DIM), np.float32)
    return (jnp.asarray(np.concatenate([one, cos, cos, pad1], axis=-1)),
            jnp.asarray(np.concatenate([zero, -sin, sin, pad0], axis=-1)))


def _gqa_rope_lanes(n_tokens):
    cos, sin = _rope_tables(n_tokens, GQA_HEAD_DIM)
    return (jnp.asarray(np.concatenate([cos, cos], axis=-1)),
            jnp.asarray(np.concatenate([-sin, sin], axis=-1)))


def _swap_halves(w):
    half = w.shape[-1] // 2
    return jnp.concatenate([w[..., half:], w[..., :half]], axis=-1)


def _even_weights(w_in, w_uq, w_ukv):
    d = w_in.shape[0]
    i0 = MLA_Q_LORA
    i1 = i0 + MLA_KV_LORA
    i2 = i1 + MLA_ROPE_DIM
    w_kr = w_in[:, i1:i2]
    zl = jnp.zeros((d, MLA_NOPE_DIM), F32)
    zr = jnp.zeros((d, LANES - MLA_QK_DIM), F32)
    w_in_k = jnp.concatenate([w_in[:, :i1], w_in[:, i2:], zl, w_kr, zr, zl, _swap_halves(w_kr), zr],
                             axis=-1).astype(BF16)
    r = w_uq.shape[0]
    uq = w_uq.reshape(r, MLA_HEADS, MLA_QK_DIM)
    zpad = jnp.zeros((r, MLA_HEADS, LANES - MLA_QK_DIM), F32)
    znope = jnp.zeros((r, MLA_HEADS, MLA_NOPE_DIM), F32)
    q_plain = jnp.concatenate([uq, zpad], axis=-1).reshape(r, MLA_HEADS * LANES)
    q_swap = jnp.concatenate([znope, _swap_halves(uq[..., MLA_NOPE_DIM:]), zpad], axis=-1)
    w_uq_k = jnp.concatenate([q_plain, q_swap.reshape(r, MLA_HEADS * LANES)], axis=-1).astype(BF16)
    r = w_ukv.shape[0]
    ukv = w_ukv.reshape(r, MLA_HEADS, MLA_NOPE_DIM + MLA_V_DIM)
    k_pad = jnp.concatenate([ukv[..., :MLA_NOPE_DIM], jnp.zeros((r, MLA_HEADS, LANES - MLA_NOPE_DIM), F32)],
                            axis=-1).reshape(r, MLA_HEADS * LANES)
    v_cat = ukv[..., MLA_NOPE_DIM:].reshape(r, MLA_HEADS * MLA_V_DIM)
    w_ukv_k = jnp.concatenate([k_pad, v_cat], axis=-1).astype(BF16)
    return w_in_k, w_uq_k, w_ukv_k


def kernel(x_prompt, x_sample, cache_mla_ckv, cache_mla_krope, cache_na_k, cache_na_v, cache_gqa_k, cache_gqa_v, c, c_ctx, w_mod, b_mod, norm_mix, norm_ffn, norm_final, w_in_a, mla_q_norm, mla_w_uq, mla_kv_norm, mla_w_ukv, na_rpb, w_out_a, w_in_c, gqa_q_norm, gqa_k_norm, w_out_c, w_ffn_in, w_ffn_out):
    batch, seq, d = x_prompt.shape
    dec_batch, dec_seq, _ = x_sample.shape
    depth = w_mod.shape[0]
    past = cache_mla_ckv.shape[2]
    n_ctx = batch * seq
    n_lat = dec_batch * dec_seq

    cond = jnp.concatenate([c_ctx[None], c, jnp.zeros((8 - 1 - dec_batch, d), F32)], axis=0)
    mod = _modulation(cond, w_mod, b_mod).reshape(depth, 8, 6, d)

    xp = x_prompt.reshape(n_ctx, d)
    xs = x_sample.reshape(n_lat, d)
    cos_m, sin_m = _mla_rope_lanes(dec_seq)
    cos_g, sin_g = _gqa_rope_lanes(dec_seq)
    ident_cos = jnp.ones((TOKEN_TILE, LANES), F32)
    ident_sin = jnp.zeros((TOKEN_TILE, LANES), F32)
    gfin = norm_final.reshape(1, d)
    states = {k: [] for k in ("ckv", "krope", "nk", "nv", "gk", "gv")}
    wfi = w_ffn_in.astype(BF16)
    wfo = w_ffn_out.astype(BF16)

    for l in range(depth):
        mod_p = mod[l, 0:1]
        mod_s = mod[l, 1:1 + dec_batch]
        gmix = norm_mix[l].reshape(1, d)
        gffn = norm_ffn[l].reshape(1, d)
        if l % 2 == 0:
            e = l // 2
            w_in_k, w_uq_k, w_ukv_k = _even_weights(w_in_a[e], mla_w_uq[e], mla_w_ukv[e])
            qn = (mla_q_norm[e] * MLA_SCALE).reshape(1, -1)
            kvn = mla_kv_norm[e].reshape(1, -1)
            (qp, kp, vp, nqp, nkp, nvp, s_ckv, s_kr, s_nk, s_nv) = _even_in(
                xp, mod_p, n_ctx, gmix, w_in_k, qn, kvn, w_uq_k, w_ukv_k, ident_cos, ident_sin, True)
            states["ckv"].append(s_ckv.reshape(batch, seq, MLA_KV_LORA))
            states["krope"].append(s_kr[:, MLA_NOPE_DIM:MLA_QK_DIM].reshape(batch, seq, MLA_ROPE_DIM))
            states["nk"].append(s_nk.reshape(batch, seq, NA_HEADS, NA_HEAD_DIM))
            states["nv"].append(s_nv.reshape(batch, seq, NA_HEADS, NA_HEAD_DIM))
            qs, ks, vs, nqs, nks, nvs = _even_in(
                xs, mod_s, dec_seq, gmix, w_in_k, qn, kvn, w_uq_k, w_ukv_k, cos_m, sin_m, False)
            kr_cache = jnp.pad(cache_mla_krope[:, e],
                               ((0, 0), (0, 0), (MLA_NOPE_DIM, LANES - MLA_QK_DIM)))
            kc, vc = _cache_expand(cache_mla_ckv[:, e], kr_cache, w_ukv_k)

            r3 = lambda a, b_: a.reshape(b_, a.shape[0] // b_, a.shape[1])
            mla_kw = dict(groups=MLA_HEADS // 2, heads=2, k_stride=LANES, q_half_mask=False, pair_out=True)
            na_kw = dict(groups=NA_HEADS // 2, heads=2, k_stride=0, q_half_mask=True, pair_out=True)
            a_mla_p = _ctx_attention(r3(qp, batch), r3(kp, batch), r3(vp, batch), name="mla_ctx",
                                     v_block=2 * LANES, **mla_kw)
            a_na_p = _ctx_attention(r3(nqp, batch), r3(nkp, batch), r3(nvp, batch), name="na_ctx",
                                    v_block=2 * LANES, **na_kw)
            a_mla_s = _attention(r3(qs, dec_batch), [(r3(ks, dec_batch), r3(vs, dec_batch)), (kc, vc)],
                                 q_tile=MLA_Q_TILE, name="mla_lat", **mla_kw)
            bias = _na_bias_tables(na_rpb[e])
            nv_cache = cache_na_v[:, e].astype(BF16).reshape(dec_batch, past, NA_HEADS // 2, LANES)
            nv_cache = jnp.concatenate([nv_cache, jnp.ones_like(nv_cache)], axis=-1)
            a_na_s = _neighbourhood_attention(
                r3(nqs, dec_batch), r3(nks, dec_batch), r3(nvs, dec_batch),
                cache_na_k[:, e].reshape(dec_batch, past, NA_W).astype(BF16),
                nv_cache.reshape(dec_batch, past, -1), bias)
            attn_p = [a_mla_p.reshape(n_ctx, -1), a_na_p.reshape(n_ctx, -1)]
            attn_s = [a_mla_s.reshape(n_lat, -1), a_na_s.reshape(n_lat, -1)]
            wo = w_out_a[e].astype(BF16)
            half = MLA_HEADS * MLA_V_DIM
            w_outs = [wo[:half], wo[half:]]
        else:
            o = l // 2
            w_in_k = w_in_c[o].astype(BF16)
            qn = gqa_q_norm[o] * GQA_SCALE
            qn = jnp.stack([qn, _swap_halves(qn)])
            kn = jnp.stack([gqa_k_norm[o], _swap_halves(gqa_k_norm[o])])
            qp, kp, vp, s_gk, s_gv = _odd_in(xp, mod_p, n_ctx, gmix, w_in_k, qn, kn, ident_cos, ident_sin, True)
            states["gk"].append(s_gk.reshape(batch, seq, GQA_KV_HEADS, GQA_HEAD_DIM))
            states["gv"].append(s_gv.reshape(batch, seq, GQA_KV_HEADS, GQA_HEAD_DIM))
            qs, ks, vs = _odd_in(xs, mod_s, dec_seq, gmix, w_in_k, qn, kn, cos_g, sin_g, False)
            r3 = lambda a, b_: a.reshape(b_, a.shape[0] // b_, a.shape[1])
            gqa_kw = dict(groups=GQA_KV_HEADS, heads=GQA_GROUP, k_stride=0, q_half_mask=False, pair_out=False)
            a_p = _ctx_attention(r3(qp, batch), r3(kp, batch), r3(vp, batch), name="gqa_ctx",
                                 v_block=2 * LANES, **gqa_kw)
            kcache = cache_gqa_k[:, o].reshape(dec_batch, past, -1).astype(BF16)
            vcache = cache_gqa_v[:, o].astype(BF16)
            vcache = jnp.concatenate([vcache, jnp.ones_like(vcache)], axis=-1).reshape(dec_batch, past, -1)
            a_s = _attention(r3(qs, dec_batch), [(r3(ks, dec_batch), r3(vs, dec_batch)), (kcache, vcache)],
                             q_tile=GQA_Q_TILE, name="gqa_lat", **gqa_kw)
            attn_p = [a_p.reshape(n_ctx, -1)]
            attn_s = [a_s.reshape(n_lat, -1)]
            w_outs = [w_out_c[o].astype(BF16)]
        last = l == depth - 1
        xp = _out_ffn(xp, mod_p, n_ctx, attn_p, w_outs, gffn, l, wfi, wfo, gfin, last)
        xs = _out_ffn(xs, mod_s, dec_seq, attn_s, w_outs, gffn, l, wfi, wfo, gfin, last)

    y_prompt = xp.reshape(batch, seq, d)
    y_sample = xs.reshape(dec_batch, dec_seq, d)
    return (y_prompt, y_sample,
            jnp.stack(states["ckv"], axis=1), jnp.stack(states["krope"], axis=1),
            jnp.stack(states["nk"], axis=1), jnp.stack(states["nv"], axis=1),
            jnp.stack(states["gk"], axis=1), jnp.stack(states["gv"], axis=1))
```

```python
import functools
import math

import numpy as np
import jax
import jax.numpy as jnp
from jax import lax
from jax.experimental import pallas as pl
from jax.experimental.pallas import tpu as pltpu

LANES = 128
V7X_VMEM_BYTES = 64 * 1024 * 1024

D_MODEL = 1024
GRID_W = 64
ROPE_THETA = 10000.0
RMS_EPS = 1e-6
NEG_INF = -1e30
MLA_HEADS = 8
MLA_Q_LORA = 256
MLA_KV_LORA = 256
MLA_NOPE_DIM = 64
MLA_ROPE_DIM = 32
MLA_V_DIM = 64
MLA_QK_DIM = MLA_NOPE_DIM + MLA_ROPE_DIM
LOG2E = math.log2(math.e)
MLA_SCALE = MLA_QK_DIM ** -0.5 * LOG2E
NA_HEADS = 8
NA_HEAD_DIM = 64
NA_WIN_H = 8
NA_WIN_W = 16
NA_SCALE = NA_HEAD_DIM ** -0.5 * LOG2E
NA_W = NA_HEADS * NA_HEAD_DIM
GQA_HEADS = 8
GQA_KV_HEADS = 2
GQA_HEAD_DIM = 128
GQA_SCALE = GQA_HEAD_DIM ** -0.5 * LOG2E
GQA_GROUP = GQA_HEADS // GQA_KV_HEADS

TOKEN_TILE = 512
MLA_Q_TILE = 1024
GQA_Q_TILE = 512
ATTN_K_CHUNK = 256
NA_Q_ROWS = 4
NA_BAND_ROWS = 12
NA_BLOCKS_PER_STEP = 4

BF16 = jnp.bfloat16
F32 = jnp.float32


def _vmem_limit(nbytes):
    return int(min(V7X_VMEM_BYTES - (4 << 20), max(nbytes, 16 << 20)))


def _params(nbytes, ndims):
    return pltpu.CompilerParams(dimension_semantics=("arbitrary",) * ndims,
                                vmem_limit_bytes=_vmem_limit(nbytes))


def _rms(x, gain):
    return x * lax.rsqrt(jnp.mean(x * x, axis=-1, keepdims=True) + RMS_EPS) * gain


def _dot(a, b):
    return jnp.dot(a, b, preferred_element_type=F32)


def _dot_nt(a, b):
    return lax.dot_general(a, b, (((1,), (1,)), ((), ())), preferred_element_type=F32)


def _store_pairs_with_ones(v_ref, index, v):
    ones = jnp.ones((v.shape[0], LANES), BF16)
    for i in range(v.shape[1] // LANES):
        v_ref[index + (slice(None), slice(2 * i * LANES, (2 * i + 1) * LANES))] = (
            v[:, i * LANES:(i + 1) * LANES].astype(BF16))
        v_ref[index + (slice(None), slice((2 * i + 1) * LANES, (2 * i + 2) * LANES))] = ones


def _const_spec(shape):
    nd = len(shape)
    return pl.BlockSpec(shape, lambda *_: (0,) * nd, pipeline_mode=pl.Buffered(1))


def _mod_kernel(cond_ref, w_ref, b_ref, o_ref):
    c = cond_ref[...]
    s = (c * jax.nn.sigmoid(c)).astype(BF16)
    o_ref[0] = _dot(s, w_ref[0].astype(BF16)) + b_ref[0]


def _modulation(cond, w_mod, b_mod):
    depth, d, n = w_mod.shape
    rows = cond.shape[0]
    bn = 1024
    return pl.pallas_call(
        _mod_kernel,
        out_shape=jax.ShapeDtypeStruct((depth, rows, n), F32),
        grid=(depth, n // bn),
        in_specs=[pl.BlockSpec((rows, d), lambda l, j: (0, 0)),
                  pl.BlockSpec((1, d, bn), lambda l, j: (l, 0, j)),
                  pl.BlockSpec((1, 1, bn), lambda l, j: (l, 0, j))],
        out_specs=pl.BlockSpec((1, rows, bn), lambda l, j: (l, 0, j)),
        compiler_params=_params(3 * d * bn * 4, 2),
        name="ada_modulation",
    )(cond, w_mod, b_mod.reshape(depth, 1, n))


def _even_in_kernel(x_ref, mod_ref, gmix_ref, w_in_ref, qn_ref, kvn_ref, w_uq_ref, w_ukv_ref,
                    cos_ref, sin_ref, *out_refs, with_state):
    q_ref, k_ref, v_ref, nq_ref, nk_ref, nv_ref = out_refs[:6]
    x = x_ref[...]
    mod = mod_ref[0]
    h = _rms(x, gmix_ref[...]) * (1.0 + mod[1:2]) + mod[0:1]
    p = _dot(h.astype(BF16), w_in_ref[...])
    cq = p[:, 0:256]
    ckv = _rms(p[:, 256:512], kvn_ref[...])
    nq = p[:, 512:1024]
    nk = p[:, 1024:1536]
    nv = p[:, 1536:2048]
    kr = p[:, 2048:2176]
    cos = cos_ref[...]
    sin = sin_ref[...]
    lane = lax.broadcasted_iota(jnp.int32, cos.shape, 1)
    first_half = lane < MLA_NOPE_DIM + MLA_ROPE_DIM // 2
    shift = MLA_ROPE_DIM // 2

    def rope(t):
        if with_state:
            return t
        partner = jnp.where(first_half, pltpu.roll(t, LANES - shift, 1), pltpu.roll(t, shift, 1))
        return t * cos + partner * sin

    qq = _dot(_rms(cq, qn_ref[...]).astype(BF16), w_uq_ref[...])
    kv = _dot(ckv.astype(BF16), w_ukv_ref[...])
    kr_rot = rope(kr)
    for hd in range(MLA_HEADS):
        lo = hd * LANES
        q_ref[:, lo:lo + LANES] = rope(qq[:, lo:lo + LANES]).astype(BF16)
        k_ref[:, lo:lo + LANES] = (kv[:, lo:lo + LANES] + kr_rot).astype(BF16)
    _store_pairs_with_ones(v_ref, (), kv[:, 1024:1536])
    nq_ref[...] = (nq * NA_SCALE).astype(BF16)
    nk_ref[...] = nk.astype(BF16)
    _store_pairs_with_ones(nv_ref, (), nv)
    if with_state:
        s_ckv_ref, s_kr_ref, s_nk_ref, s_nv_ref = out_refs[6:]
        s_ckv_ref[...] = ckv
        s_kr_ref[...] = kr
        s_nk_ref[...] = nk
        s_nv_ref[...] = nv


def _even_in(x, mod, tokens_per_group, gmix, w_in, qn, kvn, w_uq, w_ukv, cos, sin, with_state):
    n = x.shape[0]
    tm = TOKEN_TILE
    tiles_per_group = tokens_per_group // tm
    rope_tiles = cos.shape[0] // tm
    row = lambda i: (i, 0)
    outs = [jax.ShapeDtypeStruct((n, 1024), BF16), jax.ShapeDtypeStruct((n, 1024), BF16),
            jax.ShapeDtypeStruct((n, 1024), BF16), jax.ShapeDtypeStruct((n, 512), BF16),
            jax.ShapeDtypeStruct((n, 512), BF16), jax.ShapeDtypeStruct((n, 1024), BF16)]
    if with_state:
        outs += [jax.ShapeDtypeStruct((n, 256), F32), jax.ShapeDtypeStruct((n, 128), F32),
                 jax.ShapeDtypeStruct((n, 512), F32), jax.ShapeDtypeStruct((n, 512), F32)]
    return pl.pallas_call(
        functools.partial(_even_in_kernel, with_state=with_state),
        out_shape=outs,
        grid=(n // tm,),
        in_specs=[pl.BlockSpec((tm, D_MODEL), row),
                  pl.BlockSpec((1, 6, D_MODEL), lambda i: (i // tiles_per_group, 0, 0)),
                  _const_spec(gmix.shape), _const_spec(w_in.shape), _const_spec(qn.shape),
                  _const_spec(kvn.shape), _const_spec(w_uq.shape), _const_spec(w_ukv.shape),
                  pl.BlockSpec((tm, LANES), lambda i: (i % rope_tiles, 0)),
                  pl.BlockSpec((tm, LANES), lambda i: (i % rope_tiles, 0))],
        out_specs=[pl.BlockSpec((tm, o.shape[1]), row) for o in outs],
        compiler_params=_params(40 << 20, 1),
        name="even_in",
    )(x, mod, gmix, w_in, qn, kvn, w_uq, w_ukv, cos, sin)


def _cache_expand_kernel(ckv_ref, kr_ref, w_ukv_ref, k_ref, v_ref):
    kv = _dot(ckv_ref[0].astype(BF16), w_ukv_ref[...])
    kr = kr_ref[0]
    for hd in range(MLA_HEADS):
        lo = hd * LANES
        k_ref[0, :, lo:lo + LANES] = (kv[:, lo:lo + LANES] + kr).astype(BF16)
    _store_pairs_with_ones(v_ref, (0,), kv[:, 1024:1536])


def _cache_expand(ckv, kr128, w_ukv):
    b, s, _ = ckv.shape
    return pl.pallas_call(
        _cache_expand_kernel,
        out_shape=[jax.ShapeDtypeStruct((b, s, 1024), BF16), jax.ShapeDtypeStruct((b, s, 1024), BF16)],
        grid=(b,),
        in_specs=[pl.BlockSpec((1, s, MLA_KV_LORA), lambda i: (i, 0, 0)),
                  pl.BlockSpec((1, s, LANES), lambda i: (i, 0, 0)),
                  _const_spec(w_ukv.shape)],
        out_specs=[pl.BlockSpec((1, s, 1024), lambda i: (i, 0, 0)),
                   pl.BlockSpec((1, s, 1024), lambda i: (i, 0, 0))],
        compiler_params=_params(16 << 20, 1),
        name="mla_cache_expand",
    )(ckv, kr128, w_ukv)


def _odd_in_kernel(x_ref, mod_ref, gmix_ref, w_in_ref, qn_ref, kn_ref, avg_ref, cos_ref, sin_ref, *out_refs,
                   with_state):
    q_ref, k_ref, v_ref = out_refs[:3]
    x = x_ref[...]
    mod = mod_ref[0]
    h = (_rms(x, gmix_ref[...]) * (1.0 + mod[1:2]) + mod[0:1]).astype(BF16)
    cos = cos_ref[...]
    sin = sin_ref[...]
    half = GQA_HEAD_DIM // 2
    q_cos, q_sin = qn_ref[0:1] * cos, qn_ref[1:2] * sin
    k_cos, k_sin = kn_ref[0:1] * cos, kn_ref[1:2] * sin

    pair = 2 * LANES

    def project(col):
        return _dot(h, w_in_ref[:, col:col + pair])

    def inv_rms(p):
        sq = p * p
        hi = sq.astype(BF16)
        lo = (sq - hi.astype(F32)).astype(BF16)
        return lax.rsqrt(_dot(hi, avg_ref[...]) + _dot(lo, avg_ref[...]) + RMS_EPS)

    k_off = GQA_HEADS * LANES
    v_off = k_off + GQA_KV_HEADS * LANES
    cols = [g * pair for g in range(GQA_HEADS // 2)] + [k_off, v_off]
    p_next = project(cols[0])
    for idx, col in enumerate(cols[:-1]):
        p, p_next = p_next, project(cols[idx + 1])
        r = inv_rms(p)
        for i in range(2):
            ph = p[:, i * LANES:(i + 1) * LANES]
            rh = r[:, i * LANES:(i + 1) * LANES]
            lo = i * LANES
            if col < k_off:
                qh = (ph * q_cos + pltpu.roll(ph, half, 1) * q_sin) * rh
                q_ref[:, col + lo:col + lo + LANES] = qh.astype(BF16)
            else:
                kh = (ph * k_cos + pltpu.roll(ph, half, 1) * k_sin) * rh
                k_ref[:, lo:lo + LANES] = kh.astype(BF16)
                if with_state:
                    out_refs[3][:, lo:lo + LANES] = ph * rh * kn_ref[0:1]
    ones = jnp.ones((x.shape[0], LANES), BF16)
    for hd in range(GQA_KV_HEADS):
        v_ref[:, 2 * hd * LANES:(2 * hd + 1) * LANES] = p_next[:, hd * LANES:(hd + 1) * LANES].astype(BF16)
        v_ref[:, (2 * hd + 1) * LANES:(2 * hd + 2) * LANES] = ones
    if with_state:
        out_refs[4][...] = p_next


def _odd_in(x, mod, tokens_per_group, gmix, w_in, qn, kn, cos, sin, with_state):
    n = x.shape[0]
    tm = TOKEN_TILE
    tiles_per_group = tokens_per_group // tm
    rope_tiles = cos.shape[0] // tm
    row = lambda i: (i, 0)
    outs = [jax.ShapeDtypeStruct((n, 1024), BF16), jax.ShapeDtypeStruct((n, 256), BF16),
            jax.ShapeDtypeStruct((n, 512), BF16)]
    if with_state:
        outs += [jax.ShapeDtypeStruct((n, 256), F32), jax.ShapeDtypeStruct((n, 256), F32)]
    avg = np.kron(np.eye(2), np.full((GQA_HEAD_DIM, GQA_HEAD_DIM), 1.0 / GQA_HEAD_DIM))
    avg = jnp.asarray(avg, BF16)
    return pl.pallas_call(
        functools.partial(_odd_in_kernel, with_state=with_state),
        out_shape=outs,
        grid=(n // tm,),
        in_specs=[pl.BlockSpec((tm, D_MODEL), row),
                  pl.BlockSpec((1, 6, D_MODEL), lambda i: (i // tiles_per_group, 0, 0)),
                  _const_spec(gmix.shape), _const_spec(w_in.shape), _const_spec(qn.shape),
                  _const_spec(kn.shape), _const_spec(avg.shape),
                  pl.BlockSpec((tm, LANES), lambda i: (i % rope_tiles, 0)),
                  pl.BlockSpec((tm, LANES), lambda i: (i % rope_tiles, 0))],
        out_specs=[pl.BlockSpec((tm, o.shape[1]), row) for o in outs],
        compiler_params=_params(32 << 20, 1),
        name="odd_in",
    )(x, mod, gmix, w_in, qn, kn, avg, cos, sin)


def _attn_kernel(*refs, n_src, heads, q_tile, src_len, k_stride, q_half_mask, pair_out):
    q_ref = refs[0]
    kv_refs = refs[1:1 + 2 * n_src]
    o_ref = refs[1 + 2 * n_src]
    s_ref = refs[2 + 2 * n_src]
    n_tiles = q_ref.shape[1] // q_tile
    chunks = []
    for src in range(n_src):
        ck = min(ATTN_K_CHUNK, src_len[src])
        for c in range(src_len[src] // ck):
            chunks.append((src, c * ck, ck))
    lane = lax.broadcasted_iota(jnp.int32, (q_tile, LANES), 1)

    def rows(t):
        if isinstance(t, int):
            return slice(t * q_tile, (t + 1) * q_tile)
        return pl.ds(pl.multiple_of(t * q_tile, q_tile), q_tile)

    def load_q(t, j):
        if q_half_mask:
            qb = q_ref[0, rows(t), :]
            return jnp.where((lane >= 64) == (j == 1), qb, jnp.zeros_like(qb))
        return q_ref[0, rows(t), j * LANES:(j + 1) * LANES]

    def slot(j, q_next, m_prev):
        if q_next is not None:
            m_part = jnp.full((q_tile, LANES), -jnp.inf, F32)
        if m_prev is not None:
            acc = jnp.zeros((q_tile, 2 * LANES), F32)
        off = 0
        for src, k0, ck in chunks:
            if q_next is not None:
                kc = kv_refs[2 * src][0, j * k_stride:j * k_stride + LANES, k0:k0 + ck]
                s_new = _dot(q_next, kc)
            if m_prev is not None:
                s_old = s_ref[:, off:off + ck]
            if q_next is not None:
                s_ref[:, off:off + ck] = s_new
                for i in range(ck // LANES):
                    m_part = jnp.maximum(m_part, s_new[:, i * LANES:(i + 1) * LANES])
            if m_prev is not None:
                p = jnp.exp2(s_old - m_prev)
                acc = acc + _dot(p.astype(BF16), kv_refs[2 * src + 1][0, k0:k0 + ck, :])
            off += ck
        m_next = None if q_next is None else jnp.max(m_part, axis=-1, keepdims=True)
        o_prev = None if m_prev is None else acc[:, :LANES] / acc[:, LANES:]
        return m_next, o_prev

    def write_out(t, j, o):
        o = o.astype(o_ref.dtype)
        if not pair_out:
            o_ref[0, rows(t), j * LANES:(j + 1) * LANES] = o
        elif j == 0:
            o_ref[0, rows(t), :] = o
        else:
            o_ref[0, rows(t), :] = jnp.where(lane < 64, o_ref[0, rows(t), :], o)

    def tile(j, t, m, last):
        m_next, o = slot(j, None if last else load_q(t + 1, j), m)
        write_out(t, j, o)
        return m_next

    for j in range(heads):
        m, _ = slot(j, load_q(0, j), None)
        if n_tiles > 1:
            m = lax.fori_loop(0, n_tiles - 1, lambda t, m, j=j: tile(j, t, m, False), m)
        tile(j, n_tiles - 1, m, True)


def _attention(q, sources, *, groups, heads, k_stride, q_half_mask, pair_out, q_tile, name):
    b, t, _ = q.shape
    q_block = LANES if q_half_mask else heads * LANES
    k_block = LANES if k_stride == 0 else heads * LANES
    out_block = LANES if pair_out else heads * LANES
    src_len = tuple(k.shape[1] for k, _ in sources)
    in_specs = [pl.BlockSpec((1, t, q_block), lambda bi, g: (bi, 0, g))]
    args = [q]
    for k, v in sources:
        s = k.shape[1]
        in_specs.append(pl.BlockSpec((1, k_block, s), lambda bi, g: (bi, g, 0)))
        in_specs.append(pl.BlockSpec((1, s, 2 * LANES), lambda bi, g: (bi, 0, g)))
        args += [jnp.swapaxes(k, 1, 2), v]
    total = sum(src_len)
    return pl.pallas_call(
        functools.partial(_attn_kernel, n_src=len(sources), heads=heads, q_tile=q_tile,
                          src_len=src_len, k_stride=k_stride, q_half_mask=q_half_mask,
                          pair_out=pair_out),
        out_shape=jax.ShapeDtypeStruct((b, t, groups * out_block), BF16),
        grid=(b, groups),
        in_specs=in_specs,
        out_specs=pl.BlockSpec((1, t, out_block), lambda bi, g: (bi, 0, g)),
        scratch_shapes=[pltpu.VMEM((q_tile, total), F32)],
        compiler_params=_params(58 << 20, 2),
        name=name,
    )(*args)


def _ctx_attn_kernel(q_ref, k_ref, v_ref, o_ref, *, heads, k_stride, q_half_mask, pair_out):
    nb, t, _ = q_ref.shape
    lane = lax.broadcasted_iota(jnp.int32, (t, LANES), 1)
    for b in range(nb):
        outs = []
        for j in range(heads):
            if q_half_mask:
                qb = q_ref[b]
                q = jnp.where((lane >= 64) == (j == 1), qb, jnp.zeros_like(qb))
            else:
                q = q_ref[b, :, j * LANES:(j + 1) * LANES]
            s = _dot_nt(q, k_ref[b, :, j * k_stride:j * k_stride + LANES])
            p = jnp.exp2(s - jnp.max(s, axis=-1, keepdims=True))
            acc = _dot(p.astype(BF16), v_ref[b, :, :LANES])
            outs.append(acc / jnp.sum(p, axis=-1, keepdims=True))
        if pair_out:
            o_ref[b] = jnp.where(lane < 64, outs[0], outs[1]).astype(o_ref.dtype)
        else:
            for j in range(heads):
                o_ref[b, :, j * LANES:(j + 1) * LANES] = outs[j].astype(o_ref.dtype)


def _ctx_attention(q, k, v, *, groups, heads, k_stride, q_half_mask, pair_out, name, v_block=LANES):
    b, t, _ = q.shape
    nb = 4
    q_block = LANES if q_half_mask else heads * LANES
    k_block = LANES if k_stride == 0 else heads * LANES
    out_block = LANES if pair_out else heads * LANES
    spec = lambda w: pl.BlockSpec((nb, t, w), lambda bi, g: (bi, 0, g))
    return pl.pallas_call(
        functools.partial(_ctx_attn_kernel, heads=heads, k_stride=k_stride, q_half_mask=q_half_mask,
                          pair_out=pair_out),
        out_shape=jax.ShapeDtypeStruct((b, t, groups * out_block), BF16),
        grid=(b // nb, groups),
        in_specs=[spec(q_block), spec(k_block), spec(v_block)],
        out_specs=spec(out_block),
        compiler_params=_params(32 << 20, 2),
        name=name,
    )(q, k, v)


def _na_bias_tables(rpb):
    n_rows = GRID_W
    h, n_dr, n_dc = rpb.shape
    edge = n_dc - 1 - (NA_WIN_W - 1)
    w = jnp.concatenate([rpb[..., NA_WIN_W - 1:],
                         jnp.broadcast_to(rpb[..., n_dc - 1:], (h, n_dr, GRID_W - 1 - edge)),
                         jnp.broadcast_to(rpb[..., :1], (h, n_dr, GRID_W - (NA_WIN_W - 1) + 1)),
                         rpb[..., 1:NA_WIN_W - 1]], axis=-1)
    toe = jnp.tile(w, (1, 1, GRID_W))[..., :GRID_W * (2 * GRID_W - 1)]
    toe = toe.reshape(h, n_dr, GRID_W, 2 * GRID_W - 1)[..., :GRID_W]
    cols = np.arange(GRID_W)
    cs = np.clip(cols - NA_WIN_W // 2, 0, GRID_W - NA_WIN_W)
    col_ok = (cols[None, :] >= cs[:, None]) & (cols[None, :] < cs[:, None] + NA_WIN_W)
    toe = jnp.where(col_ok, toe * LOG2E, NEG_INF)
    toe = toe.transpose(0, 2, 1, 3).reshape(h, GRID_W, n_dr * GRID_W)
    pieces = []
    for blk in (0, 1, n_rows // NA_Q_ROWS - 1):
        b0 = int(np.clip(NA_Q_ROWS * blk - NA_WIN_H // 2, 0, n_rows - NA_BAND_ROWS))
        for qr in range(NA_Q_ROWS):
            r = NA_Q_ROWS * blk + qr
            rs = int(np.clip(r - NA_WIN_H // 2, 0, n_rows - NA_WIN_H))
            dr0 = rs - r + NA_WIN_H - 1
            seen = toe[:, None, :, dr0 * GRID_W:(dr0 + NA_WIN_H) * GRID_W]
            before = (rs - b0) * GRID_W
            after = (NA_BAND_ROWS - NA_WIN_H) * GRID_W - before
            pieces.append(jnp.pad(seen, ((0, 0), (0, 0), (0, 0), (before, after)), constant_values=NEG_INF))
    return jnp.concatenate(pieces, axis=1).reshape(h, 3, NA_Q_ROWS * GRID_W, NA_BAND_ROWS * GRID_W)


def _na_kernel(q_ref, k_ref, v_ref, kc_ref, vc_ref, bias_ref, o_ref, s_ref):
    nq = NA_Q_ROWS * GRID_W
    nb = NA_BAND_ROWS * GRID_W
    per = NA_BLOCKS_PER_STEP
    t = k_ref.shape[1]
    n_blocks = t // nq
    n_groups = n_blocks // per
    lane = lax.broadcasted_iota(jnp.int32, (nq, LANES), 1)

    def block_rows(i):
        if isinstance(i, int):
            return slice(i * nq, (i + 1) * nq)
        return pl.ds(pl.multiple_of(i * nq, nq), nq)

    def band_rows(i):
        first = nq * i - (NA_WIN_H // 2) * GRID_W
        if isinstance(i, int):
            first = min(max(first, 0), t - nb)
            return slice(first, first + nb)
        return pl.ds(pl.multiple_of(jnp.clip(first, 0, t - nb), nq), nb)

    def bias_class(i):
        if isinstance(i, int):
            return min(i, 1) + max(i - (n_blocks - 2), 0)
        return jnp.minimum(i, 1) + jnp.maximum(i - (n_blocks - 2), 0)

    def slot(j, g_next, g_prev, ms_prev):
        new_ms = []
        for blk in range(per):
            srow = slice(blk * nq, (blk + 1) * nq)
            if g_next is not None:
                i = g_next * per + blk
                qb = q_ref[0, block_rows(i), :]
                q = jnp.where((lane >= 64) == (j == 1), qb, jnp.zeros_like(qb))
                s_band = _dot_nt(q, k_ref[0, band_rows(i), :]) + bias_ref[j, bias_class(i)]
                s_ctx = _dot_nt(q, kc_ref[0])
            if g_prev is not None:
                old_band = s_ref[srow, :nb]
                old_ctx = s_ref[srow, nb:]
            if g_next is not None:
                s_ref[srow, :nb] = s_band
                s_ref[srow, nb:] = s_ctx
                new_ms.append(jnp.maximum(jnp.max(s_band, axis=-1, keepdims=True),
                                          jnp.max(s_ctx, axis=-1, keepdims=True)))
            if g_prev is not None:
                i = g_prev * per + blk
                m = ms_prev[blk]
                p_band = jnp.exp2(old_band - m)
                p_ctx = jnp.exp2(old_ctx - m)
                acc = (_dot(p_band.astype(BF16), v_ref[0, band_rows(i), :])
                       + _dot(p_ctx.astype(BF16), vc_ref[0]))
                o = (acc[:, :LANES] / acc[:, LANES:]).astype(o_ref.dtype)
                if j == 0:
                    o_ref[0, block_rows(i), :] = o
                else:
                    o_ref[0, block_rows(i), :] = jnp.where(lane < 64, o_ref[0, block_rows(i), :], o)
        return tuple(new_ms)

    for j in range(2):
        ms = slot(j, 0, None, None)
        if n_groups > 1:
            ms = lax.fori_loop(0, n_groups - 1, lambda g, ms, j=j: slot(j, g + 1, g, ms), ms)
        slot(j, None, n_groups - 1, ms)


def _neighbourhood_attention(q, k, v, kc, vc, bias):
    b, t, w = q.shape
    pairs = w // LANES
    nq = NA_Q_ROWS * GRID_W
    nb = NA_BAND_ROWS * GRID_W
    c = kc.shape[1]
    seq = lambda n, lanes=LANES: pl.BlockSpec((1, n, lanes), lambda bi, g: (bi, 0, g))
    return pl.pallas_call(
        _na_kernel,
        out_shape=jax.ShapeDtypeStruct((b, t, w), BF16),
        grid=(b, pairs),
        in_specs=[seq(t), seq(t), seq(t, 2 * LANES), seq(c), seq(c, 2 * LANES),
                  pl.BlockSpec((2, 3, nq, nb), lambda bi, g: (g, 0, 0, 0))],
        out_specs=seq(t),
        scratch_shapes=[pltpu.VMEM((NA_BLOCKS_PER_STEP * nq, nb + c), F32)],
        compiler_params=_params(40 << 20, 2),
        name="neighbourhood_attention",
    )(q, k, v, kc, vc, bias)


def _out_ffn_kernel(*refs, n_attn, ff_chunk, final_norm):
    x_ref, mod_ref = refs[0], refs[1]
    a_refs = refs[2:2 + n_attn]
    w_refs = refs[2 + n_attn:2 + 2 * n_attn]
    gffn_ref, w_in_ref, w_out_ref, gfin_ref, o_ref = refs[2 + 2 * n_attn:]
    mod = mod_ref[0]
    mix = _dot(a_refs[0][...], w_refs[0][...])
    for a_ref, w_ref in zip(a_refs[1:], w_refs[1:]):
        mix = mix + _dot(a_ref[...], w_ref[...])
    x1 = x_ref[...] + mod[2:3] * mix
    h = (_rms(x1, gffn_ref[...]) * (1.0 + mod[4:5]) + mod[3:4]).astype(BF16)
    d_ff = w_out_ref.shape[1]
    acc = None
    for c in range(d_ff // ff_chunk):
        lo = c * ff_chunk
        gate = _dot(h, w_in_ref[0, :, lo:lo + ff_chunk])
        up = _dot(h, w_in_ref[0, :, d_ff + lo:d_ff + lo + ff_chunk])
        act = (gate * jax.nn.sigmoid(gate) * up).astype(BF16)
        part = _dot(act, w_out_ref[0, lo:lo + ff_chunk, :])
        acc = part if acc is None else acc + part
    x2 = x1 + mod[5:6] * acc
    if final_norm:
        x2 = _rms(x2, gfin_ref[...])
    o_ref[...] = x2


def _out_ffn(x, mod, tokens_per_group, attn, w_outs, gffn, layer, w_ffn_in, w_ffn_out, gfin, final_norm):
    n = x.shape[0]
    tm = TOKEN_TILE
    tiles_per_group = tokens_per_group // tm
    row = lambda i: (i, 0)
    in_specs = [pl.BlockSpec((tm, D_MODEL), row),
                pl.BlockSpec((1, 6, D_MODEL), lambda i: (i // tiles_per_group, 0, 0))]
    in_specs += [pl.BlockSpec((tm, a.shape[1]), row) for a in attn]
    in_specs += [_const_spec(w.shape) for w in w_outs]
    layer_spec = lambda w: pl.BlockSpec((1,) + w.shape[1:], lambda i: (layer, 0, 0),
                                        pipeline_mode=pl.Buffered(1))
    in_specs += [_const_spec(gffn.shape), layer_spec(w_ffn_in), layer_spec(w_ffn_out),
                 _const_spec(gfin.shape)]
    return pl.pallas_call(
        functools.partial(_out_ffn_kernel, n_attn=len(attn), ff_chunk=256, final_norm=final_norm),
        out_shape=jax.ShapeDtypeStruct((n, D_MODEL), F32),
        grid=(n // tm,),
        in_specs=in_specs,
        out_specs=pl.BlockSpec((tm, D_MODEL), row),
        compiler_params=_params(56 << 20, 1),
        name="out_ffn",
    )(x, mod, *attn, *w_outs, gffn, w_ffn_in, w_ffn_out, gfin)


def _rope_tables(n_tokens, rot_dim):
    t = np.arange(n_tokens)
    row = (t // GRID_W).astype(np.float32)
    col = (t % GRID_W).astype(np.float32)
    axis_dim = rot_dim // 2
    inv_freq = np.float32(ROPE_THETA) ** (-np.arange(0, axis_dim, 2, dtype=np.float32) / axis_dim)
    ang = np.concatenate([row[:, None] * inv_freq, col[:, None] * inv_freq], axis=-1).astype(np.float32)
    return np.cos(ang), np.sin(ang)


def _mla_rope_lanes(n_tokens):
    cos, sin = _rope_tables(n_tokens, MLA_ROPE_DIM)
    one = np.ones((n_tokens, MLA_NOPE_DIM), np.float32)
    zero = np.zeros((n_tokens, MLA_NOPE_DIM), np.float32)
    pad1 = np.ones((n_tokens, LANES - MLA_QK_DIM), np.float32)
    pad0 = np.zeros((n_tokens, LANES - MLA_QK_DIM), np.float32)
    return (jnp.asarray(np.concatenate([one, cos, cos, pad1], axis=-1)),
            jnp.asarray(np.concatenate([zero, -sin, sin, pad0], axis=-1)))


def _gqa_rope_lanes(n_tokens):
    cos, sin = _rope_tables(n_tokens, GQA_HEAD_DIM)
    return (jnp.asarray(np.concatenate([cos, cos], axis=-1)),
            jnp.asarray(np.concatenate([-sin, sin], axis=-1)))


def _swap_halves(w):
    half = w.shape[-1] // 2
    return jnp.concatenate([w[..., half:], w[..., :half]], axis=-1)


def _even_weights(w_in, w_uq, w_ukv):
    d = w_in.shape[0]
    i0 = MLA_Q_LORA
    i1 = i0 + MLA_KV_LORA
    i2 = i1 + MLA_ROPE_DIM
    w_kr = w_in[:, i1:i2]
    zl = jnp.zeros((d, MLA_NOPE_DIM), F32)
    zr = jnp.zeros((d, LANES - MLA_QK_DIM), F32)
    w_in_k = jnp.concatenate([w_in[:, :i1], w_in[:, i2:], zl, w_kr, zr], axis=-1).astype(BF16)
    r = w_uq.shape[0]
    uq = w_uq.reshape(r, MLA_HEADS, MLA_QK_DIM)
    zpad = jnp.zeros((r, MLA_HEADS, LANES - MLA_QK_DIM), F32)
    w_uq_k = jnp.concatenate([uq, zpad], axis=-1).reshape(r, MLA_HEADS * LANES).astype(BF16)
    r = w_ukv.shape[0]
    ukv = w_ukv.reshape(r, MLA_HEADS, MLA_NOPE_DIM + MLA_V_DIM)
    k_pad = jnp.concatenate([ukv[..., :MLA_NOPE_DIM], jnp.zeros((r, MLA_HEADS, LANES - MLA_NOPE_DIM), F32)],
                            axis=-1).reshape(r, MLA_HEADS * LANES)
    v_cat = ukv[..., MLA_NOPE_DIM:].reshape(r, MLA_HEADS * MLA_V_DIM)
    w_ukv_k = jnp.concatenate([k_pad, v_cat], axis=-1).astype(BF16)
    return w_in_k, w_uq_k, w_ukv_k


def kernel(x_prompt, x_sample, cache_mla_ckv, cache_mla_krope, cache_na_k, cache_na_v, cache_gqa_k, cache_gqa_v, c, c_ctx, w_mod, b_mod, norm_mix, norm_ffn, norm_final, w_in_a, mla_q_norm, mla_w_uq, mla_kv_norm, mla_w_ukv, na_rpb, w_out_a, w_in_c, gqa_q_norm, gqa_k_norm, w_out_c, w_ffn_in, w_ffn_out):
    batch, seq, d = x_prompt.shape
    dec_batch, dec_seq, _ = x_sample.shape
    depth = w_mod.shape[0]
    past = cache_mla_ckv.shape[2]
    n_ctx = batch * seq
    n_lat = dec_batch * dec_seq

    cond = jnp.concatenate([c_ctx[None], c, jnp.zeros((8 - 1 - dec_batch, d), F32)], axis=0)
    mod = _modulation(cond, w_mod, b_mod).reshape(depth, 8, 6, d)

    xp = x_prompt.reshape(n_ctx, d)
    xs = x_sample.reshape(n_lat, d)
    cos_m, sin_m = _mla_rope_lanes(dec_seq)
    cos_g, sin_g = _gqa_rope_lanes(dec_seq)
    ident_cos = jnp.ones((TOKEN_TILE, LANES), F32)
    ident_sin = jnp.zeros((TOKEN_TILE, LANES), F32)
    gfin = norm_final.reshape(1, d)
    states = {k: [] for k in ("ckv", "krope", "nk", "nv", "gk", "gv")}
    wfi = w_ffn_in.astype(BF16)
    wfo = w_ffn_out.astype(BF16)

    for l in range(depth):
        mod_p = mod[l, 0:1]
        mod_s = mod[l, 1:1 + dec_batch]
        gmix = norm_mix[l].reshape(1, d)
        gffn = norm_ffn[l].reshape(1, d)
        if l % 2 == 0:
            e = l // 2
            w_in_k, w_uq_k, w_ukv_k = _even_weights(w_in_a[e], mla_w_uq[e], mla_w_ukv[e])
            qn = (mla_q_norm[e] * MLA_SCALE).reshape(1, -1)
            kvn = mla_kv_norm[e].reshape(1, -1)
            (qp, kp, vp, nqp, nkp, nvp, s_ckv, s_kr, s_nk, s_nv) = _even_in(
                xp, mod_p, n_ctx, gmix, w_in_k, qn, kvn, w_uq_k, w_ukv_k, ident_cos, ident_sin, True)
            states["ckv"].append(s_ckv.reshape(batch, seq, MLA_KV_LORA))
            states["krope"].append(s_kr[:, MLA_NOPE_DIM:MLA_QK_DIM].reshape(batch, seq, MLA_ROPE_DIM))
            states["nk"].append(s_nk.reshape(batch, seq, NA_HEADS, NA_HEAD_DIM))
            states["nv"].append(s_nv.reshape(batch, seq, NA_HEADS, NA_HEAD_DIM))
            qs, ks, vs, nqs, nks, nvs = _even_in(
                xs, mod_s, dec_seq, gmix, w_in_k, qn, kvn, w_uq_k, w_ukv_k, cos_m, sin_m, False)
            kr_cache = jnp.pad(cache_mla_krope[:, e],
                               ((0, 0), (0, 0), (MLA_NOPE_DIM, LANES - MLA_QK_DIM)))
            kc, vc = _cache_expand(cache_mla_ckv[:, e], kr_cache, w_ukv_k)

            r3 = lambda a, b_: a.reshape(b_, a.shape[0] // b_, a.shape[1])
            mla_kw = dict(groups=MLA_HEADS // 2, heads=2, k_stride=LANES, q_half_mask=False, pair_out=True)
            na_kw = dict(groups=NA_HEADS // 2, heads=2, k_stride=0, q_half_mask=True, pair_out=True)
            a_mla_p = _ctx_attention(r3(qp, batch), r3(kp, batch), r3(vp, batch), name="mla_ctx",
                                     v_block=2 * LANES, **mla_kw)
            a_na_p = _ctx_attention(r3(nqp, batch), r3(nkp, batch), r3(nvp, batch), name="na_ctx",
                                    v_block=2 * LANES, **na_kw)
            a_mla_s = _attention(r3(qs, dec_batch), [(r3(ks, dec_batch), r3(vs, dec_batch)), (kc, vc)],
                                 q_tile=MLA_Q_TILE, name="mla_lat", **mla_kw)
            bias = _na_bias_tables(na_rpb[e])
            nv_cache = cache_na_v[:, e].astype(BF16).reshape(dec_batch, past, NA_HEADS // 2, LANES)
            nv_cache = jnp.concatenate([nv_cache, jnp.ones_like(nv_cache)], axis=-1)
            a_na_s = _neighbourhood_attention(
                r3(nqs, dec_batch), r3(nks, dec_batch), r3(nvs, dec_batch),
                cache_na_k[:, e].reshape(dec_batch, past, NA_W).astype(BF16),
                nv_cache.reshape(dec_batch, past, -1), bias)
            attn_p = [a_mla_p.reshape(n_ctx, -1), a_na_p.reshape(n_ctx, -1)]
            attn_s = [a_mla_s.reshape(n_lat, -1), a_na_s.reshape(n_lat, -1)]
            wo = w_out_a[e].astype(BF16)
            half = MLA_HEADS * MLA_V_DIM
            w_outs = [wo[:half], wo[half:]]
        else:
            o = l // 2
            w_in_k = w_in_c[o].astype(BF16)
            qn = gqa_q_norm[o] * GQA_SCALE
            qn = jnp.stack([qn, _swap_halves(qn)])
            kn = jnp.stack([gqa_k_norm[o], _swap_halves(gqa_k_norm[o])])
            qp, kp, vp, s_gk, s_gv = _odd_in(xp, mod_p, n_ctx, gmix, w_in_k, qn, kn, ident_cos, ident_sin, True)
            states["gk"].append(s_gk.reshape(batch, seq, GQA_KV_HEADS, GQA_HEAD_DIM))
            states["gv"].append(s_gv.reshape(batch, seq, GQA_KV_HEADS, GQA_HEAD_DIM))
            qs, ks, vs = _odd_in(xs, mod_s, dec_seq, gmix, w_in_k, qn, kn, cos_g, sin_g, False)
            r3 = lambda a, b_: a.reshape(b_, a.shape[0] // b_, a.shape[1])
            gqa_kw = dict(groups=GQA_KV_HEADS, heads=GQA_GROUP, k_stride=0, q_half_mask=False, pair_out=False)
            a_p = _ctx_attention(r3(qp, batch), r3(kp, batch), r3(vp, batch), name="gqa_ctx",
                                 v_block=2 * LANES, **gqa_kw)
            kcache = cache_gqa_k[:, o].reshape(dec_batch, past, -1).astype(BF16)
            vcache = cache_gqa_v[:, o].astype(BF16)
            vcache = jnp.concatenate([vcache, jnp.ones_like(vcache)], axis=-1).reshape(dec_batch, past, -1)
            a_s = _attention(r3(qs, dec_batch), [(r3(ks, dec_batch), r3(vs, dec_batch)), (kcache, vcache)],
                             q_tile=GQA_Q_TILE, name="gqa_lat", **gqa_kw)
            attn_p = [a_p.reshape(n_ctx, -1)]
            attn_s = [a_s.reshape(n_lat, -1)]
            w_outs = [w_out_c[o].astype(BF16)]
        last = l == depth - 1
        xp = _out_ffn(xp, mod_p, n_ctx, attn_p, w_outs, gffn, l, wfi, wfo, gfin, last)
        xs = _out_ffn(xs, mod_s, dec_seq, attn_s, w_outs, gffn, l, wfi, wfo, gfin, last)

    y_prompt = xp.reshape(batch, seq, d)
    y_sample = xs.reshape(dec_batch, dec_seq, d)
    return (y_prompt, y_sample,
            jnp.stack(states["ckv"], axis=1), jnp.stack(states["krope"], axis=1),
            jnp.stack(states["nk"], axis=1), jnp.stack(states["nv"], axis=1),
            jnp.stack(states["gk"], axis=1), jnp.stack(states["gv"], axis=1))
```

```python
import functools
import math

import numpy as np
import jax
import jax.numpy as jnp
from jax import lax
from jax.experimental import pallas as pl
from jax.experimental.pallas import tpu as pltpu

LANES = 128
V7X_VMEM_BYTES = 64 * 1024 * 1024

D_MODEL = 1024
GRID_W = 64
ROPE_THETA = 10000.0
RMS_EPS = 1e-6
NEG_INF = -1e30
MLA_HEADS = 8
MLA_Q_LORA = 256
MLA_KV_LORA = 256
MLA_NOPE_DIM = 64
MLA_ROPE_DIM = 32
MLA_V_DIM = 64
MLA_QK_DIM = MLA_NOPE_DIM + MLA_ROPE_DIM
LOG2E = math.log2(math.e)
MLA_SCALE = MLA_QK_DIM ** -0.5 * LOG2E
NA_HEADS = 8
NA_HEAD_DIM = 64
NA_WIN_H = 8
NA_WIN_W = 16
NA_SCALE = NA_HEAD_DIM ** -0.5 * LOG2E
NA_W = NA_HEADS * NA_HEAD_DIM
GQA_HEADS = 8
GQA_KV_HEADS = 2
GQA_HEAD_DIM = 128
GQA_SCALE = GQA_HEAD_DIM ** -0.5 * LOG2E
GQA_GROUP = GQA_HEADS // GQA_KV_HEADS

TOKEN_TILE = 512
MLA_Q_TILE = 1024
GQA_Q_TILE = 1024
ATTN_K_CHUNK = 256
NA_Q_ROWS = 4
NA_BAND_ROWS = 12
NA_BLOCKS_PER_STEP = 4

BF16 = jnp.bfloat16
F32 = jnp.float32


def _vmem_limit(nbytes):
    return int(min(V7X_VMEM_BYTES - (4 << 20), max(nbytes, 16 << 20)))


def _params(nbytes, ndims):
    return pltpu.CompilerParams(dimension_semantics=("arbitrary",) * ndims,
                                vmem_limit_bytes=_vmem_limit(nbytes))


def _rms(x, gain):
    return x * lax.rsqrt(jnp.mean(x * x, axis=-1, keepdims=True) + RMS_EPS) * gain


def _dot(a, b):
    return jnp.dot(a, b, preferred_element_type=F32)


def _dot_nt(a, b):
    return lax.dot_general(a, b, (((1,), (1,)), ((), ())), preferred_element_type=F32)


def _store_pairs_with_ones(v_ref, index, v):
    ones = jnp.ones((v.shape[0], LANES), BF16)
    for i in range(v.shape[1] // LANES):
        v_ref[index + (slice(None), slice(2 * i * LANES, (2 * i + 1) * LANES))] = (
            v[:, i * LANES:(i + 1) * LANES].astype(BF16))
        v_ref[index + (slice(None), slice((2 * i + 1) * LANES, (2 * i + 2) * LANES))] = ones


def _const_spec(shape):
    nd = len(shape)
    return pl.BlockSpec(shape, lambda *_: (0,) * nd, pipeline_mode=pl.Buffered(1))


def _mod_kernel(cond_ref, w_ref, b_ref, o_ref):
    c = cond_ref[...]
    s = (c * jax.nn.sigmoid(c)).astype(BF16)
    o_ref[0] = _dot(s, w_ref[0].astype(BF16)) + b_ref[0]


def _modulation(cond, w_mod, b_mod):
    depth, d, n = w_mod.shape
    rows = cond.shape[0]
    bn = 1024
    return pl.pallas_call(
        _mod_kernel,
        out_shape=jax.ShapeDtypeStruct((depth, rows, n), F32),
        grid=(depth, n // bn),
        in_specs=[pl.BlockSpec((rows, d), lambda l, j: (0, 0)),
                  pl.BlockSpec((1, d, bn), lambda l, j: (l, 0, j)),
                  pl.BlockSpec((1, 1, bn), lambda l, j: (l, 0, j))],
        out_specs=pl.BlockSpec((1, rows, bn), lambda l, j: (l, 0, j)),
        compiler_params=_params(3 * d * bn * 4, 2),
        name="ada_modulation",
    )(cond, w_mod, b_mod.reshape(depth, 1, n))


def _even_in_kernel(x_ref, mod_ref, gmix_ref, w_in_ref, qn_ref, kvn_ref, w_uq_ref, w_ukv_ref,
                    cos_ref, sin_ref, *out_refs, with_state):
    q_ref, k_ref, v_ref, nq_ref, nk_ref, nv_ref = out_refs[:6]
    x = x_ref[...]
    mod = mod_ref[0]
    h = _rms(x, gmix_ref[...]) * (1.0 + mod[1:2]) + mod[0:1]
    p = _dot(h.astype(BF16), w_in_ref[...])
    cq = p[:, 0:256]
    ckv = _rms(p[:, 256:512], kvn_ref[...])
    nq = p[:, 512:1024]
    nk = p[:, 1024:1536]
    nv = p[:, 1536:2048]
    kr = p[:, 2048:2176]
    cos = cos_ref[...]
    sin = sin_ref[...]
    def rope(t):
        if with_state:
            return t * cos
        return t * cos + pltpu.roll(t, LANES - MLA_ROPE_DIM // 2, 1) * sin

    qq = _dot(_rms(cq, qn_ref[...]).astype(BF16), w_uq_ref[...])
    kv = _dot(ckv.astype(BF16), w_ukv_ref[...])
    kr_rot = rope(kr)
    for hd in range(MLA_HEADS):
        lo = hd * LANES
        q_ref[:, lo:lo + LANES] = rope(qq[:, lo:lo + LANES]).astype(BF16)
        k_ref[:, lo:lo + LANES] = (kv[:, lo:lo + LANES] + kr_rot).astype(BF16)
    _store_pairs_with_ones(v_ref, (), kv[:, 1024:1536])
    nq_ref[...] = (nq * NA_SCALE).astype(BF16)
    nk_ref[...] = nk.astype(BF16)
    _store_pairs_with_ones(nv_ref, (), nv)
    if with_state:
        s_ckv_ref, s_kr_ref, s_nk_ref, s_nv_ref = out_refs[6:]
        s_ckv_ref[...] = ckv
        s_kr_ref[...] = kr
        s_nk_ref[...] = nk
        s_nv_ref[...] = nv


def _even_in(x, mod, tokens_per_group, gmix, w_in, qn, kvn, w_uq, w_ukv, cos, sin, with_state):
    n = x.shape[0]
    tm = TOKEN_TILE
    tiles_per_group = tokens_per_group // tm
    rope_tiles = cos.shape[0] // tm
    row = lambda i: (i, 0)
    outs = [jax.ShapeDtypeStruct((n, 1024), BF16), jax.ShapeDtypeStruct((n, 1024), BF16),
            jax.ShapeDtypeStruct((n, 1024), BF16), jax.ShapeDtypeStruct((n, 512), BF16),
            jax.ShapeDtypeStruct((n, 512), BF16), jax.ShapeDtypeStruct((n, 1024), BF16)]
    if with_state:
        outs += [jax.ShapeDtypeStruct((n, 256), F32), jax.ShapeDtypeStruct((n, 128), F32),
                 jax.ShapeDtypeStruct((n, 512), F32), jax.ShapeDtypeStruct((n, 512), F32)]
    return pl.pallas_call(
        functools.partial(_even_in_kernel, with_state=with_state),
        out_shape=outs,
        grid=(n // tm,),
        in_specs=[pl.BlockSpec((tm, D_MODEL), row),
                  pl.BlockSpec((1, 6, D_MODEL), lambda i: (i // tiles_per_group, 0, 0)),
                  _const_spec(gmix.shape), _const_spec(w_in.shape), _const_spec(qn.shape),
                  _const_spec(kvn.shape), _const_spec(w_uq.shape), _const_spec(w_ukv.shape),
                  pl.BlockSpec((tm, LANES), lambda i: (i % rope_tiles, 0)),
                  pl.BlockSpec((tm, LANES), lambda i: (i % rope_tiles, 0))],
        out_specs=[pl.BlockSpec((tm, o.shape[1]), row) for o in outs],
        compiler_params=_params(40 << 20, 1),
        name="even_in",
    )(x, mod, gmix, w_in, qn, kvn, w_uq, w_ukv, cos, sin)


def _cache_expand_kernel(ckv_ref, kr_ref, w_ukv_ref, k_ref, v_ref):
    kv = _dot(ckv_ref[0].astype(BF16), w_ukv_ref[...])
    kr = kr_ref[0]
    for hd in range(MLA_HEADS):
        lo = hd * LANES
        k_ref[0, :, lo:lo + LANES] = (kv[:, lo:lo + LANES] + kr).astype(BF16)
    _store_pairs_with_ones(v_ref, (0,), kv[:, 1024:1536])


def _cache_expand(ckv, kr128, w_ukv):
    b, s, _ = ckv.shape
    return pl.pallas_call(
        _cache_expand_kernel,
        out_shape=[jax.ShapeDtypeStruct((b, s, 1024), BF16), jax.ShapeDtypeStruct((b, s, 1024), BF16)],
        grid=(b,),
        in_specs=[pl.BlockSpec((1, s, MLA_KV_LORA), lambda i: (i, 0, 0)),
                  pl.BlockSpec((1, s, LANES), lambda i: (i, 0, 0)),
                  _const_spec(w_ukv.shape)],
        out_specs=[pl.BlockSpec((1, s, 1024), lambda i: (i, 0, 0)),
                   pl.BlockSpec((1, s, 1024), lambda i: (i, 0, 0))],
        compiler_params=_params(16 << 20, 1),
        name="mla_cache_expand",
    )(ckv, kr128, w_ukv)


def _odd_in_kernel(x_ref, mod_ref, gmix_ref, w_in_ref, qn_ref, kn_ref, avg_ref, cos_ref, sin_ref, *out_refs,
                   with_state):
    q_ref, k_ref, v_ref = out_refs[:3]
    x = x_ref[...]
    mod = mod_ref[0]
    h = (_rms(x, gmix_ref[...]) * (1.0 + mod[1:2]) + mod[0:1]).astype(BF16)
    cos = cos_ref[...]
    sin = sin_ref[...]
    half = GQA_HEAD_DIM // 2
    q_cos, q_sin = qn_ref[0:1] * cos, qn_ref[1:2] * sin
    k_cos, k_sin = kn_ref[0:1] * cos, kn_ref[1:2] * sin

    pair = 2 * LANES

    def project(col):
        return _dot(h, w_in_ref[:, col:col + pair])

    def inv_rms(p):
        sq = p * p
        hi = sq.astype(BF16)
        lo = (sq - hi.astype(F32)).astype(BF16)
        return lax.rsqrt(_dot(hi, avg_ref[...]) + _dot(lo, avg_ref[...]) + RMS_EPS)

    k_off = GQA_HEADS * LANES
    v_off = k_off + GQA_KV_HEADS * LANES
    cols = [g * pair for g in range(GQA_HEADS // 2)] + [k_off, v_off]
    p_next = project(cols[0])
    for idx, col in enumerate(cols[:-1]):
        p, p_next = p_next, project(cols[idx + 1])
        r = inv_rms(p)
        for i in range(2):
            ph = p[:, i * LANES:(i + 1) * LANES]
            rh = r[:, i * LANES:(i + 1) * LANES]
            lo = i * LANES
            if col < k_off:
                qh = (ph * q_cos + pltpu.roll(ph, half, 1) * q_sin) * rh
                q_ref[:, col + lo:col + lo + LANES] = qh.astype(BF16)
            else:
                kh = (ph * k_cos + pltpu.roll(ph, half, 1) * k_sin) * rh
                k_ref[:, lo:lo + LANES] = kh.astype(BF16)
                if with_state:
                    out_refs[3][:, lo:lo + LANES] = ph * rh * kn_ref[0:1]
    ones = jnp.ones((x.shape[0], LANES), BF16)
    for hd in range(GQA_KV_HEADS):
        v_ref[:, 2 * hd * LANES:(2 * hd + 1) * LANES] = p_next[:, hd * LANES:(hd + 1) * LANES].astype(BF16)
        v_ref[:, (2 * hd + 1) * LANES:(2 * hd + 2) * LANES] = ones
    if with_state:
        out_refs[4][...] = p_next


def _odd_in(x, mod, tokens_per_group, gmix, w_in, qn, kn, cos, sin, with_state):
    n = x.shape[0]
    tm = TOKEN_TILE
    tiles_per_group = tokens_per_group // tm
    rope_tiles = cos.shape[0] // tm
    row = lambda i: (i, 0)
    outs = [jax.ShapeDtypeStruct((n, 1024), BF16), jax.ShapeDtypeStruct((n, 256), BF16),
            jax.ShapeDtypeStruct((n, 512), BF16)]
    if with_state:
        outs += [jax.ShapeDtypeStruct((n, 256), F32), jax.ShapeDtypeStruct((n, 256), F32)]
    avg = np.kron(np.eye(2), np.full((GQA_HEAD_DIM, GQA_HEAD_DIM), 1.0 / GQA_HEAD_DIM))
    avg = jnp.asarray(avg, BF16)
    return pl.pallas_call(
        functools.partial(_odd_in_kernel, with_state=with_state),
        out_shape=outs,
        grid=(n // tm,),
        in_specs=[pl.BlockSpec((tm, D_MODEL), row),
                  pl.BlockSpec((1, 6, D_MODEL), lambda i: (i // tiles_per_group, 0, 0)),
                  _const_spec(gmix.shape), _const_spec(w_in.shape), _const_spec(qn.shape),
                  _const_spec(kn.shape), _const_spec(avg.shape),
                  pl.BlockSpec((tm, LANES), lambda i: (i % rope_tiles, 0)),
                  pl.BlockSpec((tm, LANES), lambda i: (i % rope_tiles, 0))],
        out_specs=[pl.BlockSpec((tm, o.shape[1]), row) for o in outs],
        compiler_params=_params(32 << 20, 1),
        name="odd_in",
    )(x, mod, gmix, w_in, qn, kn, avg, cos, sin)


def _attn_kernel(*refs, n_src, heads, q_tile, src_len, k_stride, q_half_mask, pair_out):
    q_ref = refs[0]
    kv_refs = refs[1:1 + 2 * n_src]
    o_ref = refs[1 + 2 * n_src]
    s_ref = refs[2 + 2 * n_src]
    n_tiles = q_ref.shape[1] // q_tile
    chunks = []
    for src in range(n_src):
        ck = min(ATTN_K_CHUNK, src_len[src])
        for c in range(src_len[src] // ck):
            chunks.append((src, c * ck, ck))
    lane = lax.broadcasted_iota(jnp.int32, (q_tile, LANES), 1)

    def rows(t):
        if isinstance(t, int):
            return slice(t * q_tile, (t + 1) * q_tile)
        return pl.ds(pl.multiple_of(t * q_tile, q_tile), q_tile)

    def load_q(t, j):
        if q_half_mask:
            qb = q_ref[0, rows(t), :]
            return jnp.where((lane >= 64) == (j == 1), qb, jnp.zeros_like(qb))
        return q_ref[0, rows(t), j * LANES:(j + 1) * LANES]

    def slot(j, q_next, m_prev):
        if q_next is not None:
            m_part = jnp.full((q_tile, LANES), -jnp.inf, F32)
        if m_prev is not None:
            acc = jnp.zeros((q_tile, 2 * LANES), F32)
        off = 0
        for src, k0, ck in chunks:
            if q_next is not None:
                kc = kv_refs[2 * src][0, j * k_stride:j * k_stride + LANES, k0:k0 + ck]
                s_new = _dot(q_next, kc)
            if m_prev is not None:
                s_old = s_ref[:, off:off + ck]
            if q_next is not None:
                s_ref[:, off:off + ck] = s_new
                for i in range(ck // LANES):
                    m_part = jnp.maximum(m_part, s_new[:, i * LANES:(i + 1) * LANES])
            if m_prev is not None:
                p = jnp.exp2(s_old - m_prev)
                acc = acc + _dot(p.astype(BF16), kv_refs[2 * src + 1][0, k0:k0 + ck, :])
            off += ck
        m_next = None if q_next is None else jnp.max(m_part, axis=-1, keepdims=True)
        o_prev = None if m_prev is None else acc[:, :LANES] / acc[:, LANES:]
        return m_next, o_prev

    def write_out(t, j, o):
        o = o.astype(o_ref.dtype)
        if not pair_out:
            o_ref[0, rows(t), j * LANES:(j + 1) * LANES] = o
        elif j == 0:
            o_ref[0, rows(t), :] = o
        else:
            o_ref[0, rows(t), :] = jnp.where(lane < 64, o_ref[0, rows(t), :], o)

    def tile(j, t, m, last):
        m_next, o = slot(j, None if last else load_q(t + 1, j), m)
        write_out(t, j, o)
        return m_next

    for j in range(heads):
        m, _ = slot(j, load_q(0, j), None)
        if n_tiles > 1:
            m = lax.fori_loop(0, n_tiles - 1, lambda t, m, j=j: tile(j, t, m, False), m)
        tile(j, n_tiles - 1, m, True)


def _attention(q, sources, *, groups, heads, k_stride, q_half_mask, pair_out, q_tile, name):
    b, t, _ = q.shape
    q_block = LANES if q_half_mask else heads * LANES
    k_block = LANES if k_stride == 0 else heads * LANES
    out_block = LANES if pair_out else heads * LANES
    src_len = tuple(k.shape[1] for k, _ in sources)
    in_specs = [pl.BlockSpec((1, t, q_block), lambda bi, g: (bi, 0, g))]
    args = [q]
    for k, v in sources:
        s = k.shape[1]
        in_specs.append(pl.BlockSpec((1, k_block, s), lambda bi, g: (bi, g, 0)))
        in_specs.append(pl.BlockSpec((1, s, 2 * LANES), lambda bi, g: (bi, 0, g)))
        args += [jnp.swapaxes(k, 1, 2), v]
    total = sum(src_len)
    return pl.pallas_call(
        functools.partial(_attn_kernel, n_src=len(sources), heads=heads, q_tile=q_tile,
                          src_len=src_len, k_stride=k_stride, q_half_mask=q_half_mask,
                          pair_out=pair_out),
        out_shape=jax.ShapeDtypeStruct((b, t, groups * out_block), BF16),
        grid=(b, groups),
        in_specs=in_specs,
        out_specs=pl.BlockSpec((1, t, out_block), lambda bi, g: (bi, 0, g)),
        scratch_shapes=[pltpu.VMEM((q_tile, total), F32)],
        compiler_params=_params(58 << 20, 2),
        name=name,
    )(*args)


def _ctx_attn_kernel(q_ref, k_ref, v_ref, o_ref, *, heads, k_stride, q_half_mask, pair_out):
    nb, t, _ = q_ref.shape
    lane = lax.broadcasted_iota(jnp.int32, (t, LANES), 1)
    for b in range(nb):
        outs = []
        for j in range(heads):
            if q_half_mask:
                qb = q_ref[b]
                q = jnp.where((lane >= 64) == (j == 1), qb, jnp.zeros_like(qb))
            else:
                q = q_ref[b, :, j * LANES:(j + 1) * LANES]
            s = _dot_nt(q, k_ref[b, :, j * k_stride:j * k_stride + LANES])
            p = jnp.exp2(s - jnp.max(s, axis=-1, keepdims=True))
            acc = _dot(p.astype(BF16), v_ref[b, :, :LANES])
            outs.append(acc / jnp.sum(p, axis=-1, keepdims=True))
        if pair_out:
            o_ref[b] = jnp.where(lane < 64, outs[0], outs[1]).astype(o_ref.dtype)
        else:
            for j in range(heads):
                o_ref[b, :, j * LANES:(j + 1) * LANES] = outs[j].astype(o_ref.dtype)


def _ctx_attention(q, k, v, *, groups, heads, k_stride, q_half_mask, pair_out, name, v_block=LANES):
    b, t, _ = q.shape
    nb = 4
    q_block = LANES if q_half_mask else heads * LANES
    k_block = LANES if k_stride == 0 else heads * LANES
    out_block = LANES if pair_out else heads * LANES
    spec = lambda w: pl.BlockSpec((nb, t, w), lambda bi, g: (bi, 0, g))
    return pl.pallas_call(
        functools.partial(_ctx_attn_kernel, heads=heads, k_stride=k_stride, q_half_mask=q_half_mask,
                          pair_out=pair_out),
        out_shape=jax.ShapeDtypeStruct((b, t, groups * out_block), BF16),
        grid=(b // nb, groups),
        in_specs=[spec(q_block), spec(k_block), spec(v_block)],
        out_specs=spec(out_block),
        compiler_params=_params(32 << 20, 2),
        name=name,
    )(q, k, v)


def _na_bias_tables(rpb):
    n_rows = GRID_W
    h, n_dr, n_dc = rpb.shape
    edge = n_dc - 1 - (NA_WIN_W - 1)
    w = jnp.concatenate([rpb[..., NA_WIN_W - 1:],
                         jnp.broadcast_to(rpb[..., n_dc - 1:], (h, n_dr, GRID_W - 1 - edge)),
                         jnp.broadcast_to(rpb[..., :1], (h, n_dr, GRID_W - (NA_WIN_W - 1) + 1)),
                         rpb[..., 1:NA_WIN_W - 1]], axis=-1)
    toe = jnp.tile(w, (1, 1, GRID_W))[..., :GRID_W * (2 * GRID_W - 1)]
    toe = toe.reshape(h, n_dr, GRID_W, 2 * GRID_W - 1)[..., :GRID_W]
    cols = np.arange(GRID_W)
    cs = np.clip(cols - NA_WIN_W // 2, 0, GRID_W - NA_WIN_W)
    col_ok = (cols[None, :] >= cs[:, None]) & (cols[None, :] < cs[:, None] + NA_WIN_W)
    toe = jnp.where(col_ok, toe * LOG2E, NEG_INF)
    toe = toe.transpose(0, 2, 1, 3).reshape(h, GRID_W, n_dr * GRID_W)
    pieces = []
    for blk in (0, 1, n_rows // NA_Q_ROWS - 1):
        b0 = int(np.clip(NA_Q_ROWS * blk - NA_WIN_H // 2, 0, n_rows - NA_BAND_ROWS))
        for qr in range(NA_Q_ROWS):
            r = NA_Q_ROWS * blk + qr
            rs = int(np.clip(r - NA_WIN_H // 2, 0, n_rows - NA_WIN_H))
            dr0 = rs - r + NA_WIN_H - 1
            seen = toe[:, None, :, dr0 * GRID_W:(dr0 + NA_WIN_H) * GRID_W]
            before = (rs - b0) * GRID_W
            after = (NA_BAND_ROWS - NA_WIN_H) * GRID_W - before
            pieces.append(jnp.pad(seen, ((0, 0), (0, 0), (0, 0), (before, after)), constant_values=NEG_INF))
    return jnp.concatenate(pieces, axis=1).reshape(h, 3, NA_Q_ROWS * GRID_W, NA_BAND_ROWS * GRID_W)


def _na_kernel(q_ref, k_ref, v_ref, kc_ref, vc_ref, bias_ref, o_ref, s_ref):
    nq = NA_Q_ROWS * GRID_W
    nb = NA_BAND_ROWS * GRID_W
    per = NA_BLOCKS_PER_STEP
    t = k_ref.shape[1]
    n_blocks = t // nq
    n_groups = n_blocks // per
    lane = lax.broadcasted_iota(jnp.int32, (nq, LANES), 1)

    def block_rows(i):
        if isinstance(i, int):
            return slice(i * nq, (i + 1) * nq)
        return pl.ds(pl.multiple_of(i * nq, nq), nq)

    def band_rows(i):
        first = nq * i - (NA_WIN_H // 2) * GRID_W
        if isinstance(i, int):
            first = min(max(first, 0), t - nb)
            return slice(first, first + nb)
        return pl.ds(pl.multiple_of(jnp.clip(first, 0, t - nb), nq), nb)

    def bias_class(i):
        if isinstance(i, int):
            return min(i, 1) + max(i - (n_blocks - 2), 0)
        return jnp.minimum(i, 1) + jnp.maximum(i - (n_blocks - 2), 0)

    def slot(j, g_next, g_prev, ms_prev):
        new_ms = []
        for blk in range(per):
            srow = slice(blk * nq, (blk + 1) * nq)
            if g_next is not None:
                i = g_next * per + blk
                qb = q_ref[0, block_rows(i), :]
                q = jnp.where((lane >= 64) == (j == 1), qb, jnp.zeros_like(qb))
                s_band = _dot_nt(q, k_ref[0, band_rows(i), :]) + bias_ref[j, bias_class(i)]
                s_ctx = _dot_nt(q, kc_ref[0])
            if g_prev is not None:
                old_band = s_ref[srow, :nb]
                old_ctx = s_ref[srow, nb:]
            if g_next is not None:
                s_ref[srow, :nb] = s_band
                s_ref[srow, nb:] = s_ctx
                new_ms.append(jnp.maximum(jnp.max(s_band, axis=-1, keepdims=True),
                                          jnp.max(s_ctx, axis=-1, keepdims=True)))
            if g_prev is not None:
                i = g_prev * per + blk
                m = ms_prev[blk]
                p_band = jnp.exp2(old_band - m)
                p_ctx = jnp.exp2(old_ctx - m)
                acc = (_dot(p_band.astype(BF16), v_ref[0, band_rows(i), :])
                       + _dot(p_ctx.astype(BF16), vc_ref[0]))
                o = (acc[:, :LANES] / acc[:, LANES:]).astype(o_ref.dtype)
                if j == 0:
                    o_ref[0, block_rows(i), :] = o
                else:
                    o_ref[0, block_rows(i), :] = jnp.where(lane < 64, o_ref[0, block_rows(i), :], o)
        return tuple(new_ms)

    for j in range(2):
        ms = slot(j, 0, None, None)
        if n_groups > 1:
            ms = lax.fori_loop(0, n_groups - 1, lambda g, ms, j=j: slot(j, g + 1, g, ms), ms)
        slot(j, None, n_groups - 1, ms)


def _neighbourhood_attention(q, k, v, kc, vc, bias):
    b, t, w = q.shape
    pairs = w // LANES
    nq = NA_Q_ROWS * GRID_W
    nb = NA_BAND_ROWS * GRID_W
    c = kc.shape[1]
    seq = lambda n, lanes=LANES: pl.BlockSpec((1, n, lanes), lambda bi, g: (bi, 0, g))
    return pl.pallas_call(
        _na_kernel,
        out_shape=jax.ShapeDtypeStruct((b, t, w), BF16),
        grid=(b, pairs),
        in_specs=[seq(t), seq(t), seq(t, 2 * LANES), seq(c), seq(c, 2 * LANES),
                  pl.BlockSpec((2, 3, nq, nb), lambda bi, g: (g, 0, 0, 0))],
        out_specs=seq(t),
        scratch_shapes=[pltpu.VMEM((NA_BLOCKS_PER_STEP * nq, nb + c), F32)],
        compiler_params=_params(40 << 20, 2),
        name="neighbourhood_attention",
    )(q, k, v, kc, vc, bias)


def _out_ffn_kernel(*refs, n_attn, ff_chunk, final_norm):
    x_ref, mod_ref = refs[0], refs[1]
    a_refs = refs[2:2 + n_attn]
    w_refs = refs[2 + n_attn:2 + 2 * n_attn]
    gffn_ref, w_in_ref, w_out_ref, gfin_ref, o_ref = refs[2 + 2 * n_attn:]
    mod = mod_ref[0]
    mix = _dot(a_refs[0][...], w_refs[0][...])
    for a_ref, w_ref in zip(a_refs[1:], w_refs[1:]):
        mix = mix + _dot(a_ref[...], w_ref[...])
    x1 = x_ref[...] + mod[2:3] * mix
    h = (_rms(x1, gffn_ref[...]) * (1.0 + mod[4:5]) + mod[3:4]).astype(BF16)
    d_ff = w_out_ref.shape[1]
    acc = None
    for c in range(d_ff // ff_chunk):
        lo = c * ff_chunk
        gate = _dot(h, w_in_ref[0, :, lo:lo + ff_chunk])
        up = _dot(h, w_in_ref[0, :, d_ff + lo:d_ff + lo + ff_chunk])
        act = (gate * jax.nn.sigmoid(gate) * up).astype(BF16)
        part = _dot(act, w_out_ref[0, lo:lo + ff_chunk, :])
        acc = part if acc is None else acc + part
    x2 = x1 + mod[5:6] * acc
    if final_norm:
        x2 = _rms(x2, gfin_ref[...])
    o_ref[...] = x2


def _out_ffn(x, mod, tokens_per_group, attn, w_outs, gffn, layer, w_ffn_in, w_ffn_out, gfin, final_norm):
    n = x.shape[0]
    tm = TOKEN_TILE
    tiles_per_group = tokens_per_group // tm
    row = lambda i: (i, 0)
    in_specs = [pl.BlockSpec((tm, D_MODEL), row),
                pl.BlockSpec((1, 6, D_MODEL), lambda i: (i // tiles_per_group, 0, 0))]
    in_specs += [pl.BlockSpec((tm, a.shape[1]), row) for a in attn]
    in_specs += [_const_spec(w.shape) for w in w_outs]
    layer_spec = lambda w: pl.BlockSpec((1,) + w.shape[1:], lambda i: (layer, 0, 0),
                                        pipeline_mode=pl.Buffered(1))
    in_specs += [_const_spec(gffn.shape), layer_spec(w_ffn_in), layer_spec(w_ffn_out),
                 _const_spec(gfin.shape)]
    return pl.pallas_call(
        functools.partial(_out_ffn_kernel, n_attn=len(attn), ff_chunk=256, final_norm=final_norm),
        out_shape=jax.ShapeDtypeStruct((n, D_MODEL), F32),
        grid=(n // tm,),
        in_specs=in_specs,
        out_specs=pl.BlockSpec((tm, D_MODEL), row),
        compiler_params=_params(56 << 20, 1),
        name="out_ffn",
    )(x, mod, *attn, *w_outs, gffn, w_ffn_in, w_ffn_out, gfin)


def _rope_tables(n_tokens, rot_dim):
    t = np.arange(n_tokens)
    row = (t // GRID_W).astype(np.float32)
    col = (t % GRID_W).astype(np.float32)
    axis_dim = rot_dim // 2
    inv_freq = np.float32(ROPE_THETA) ** (-np.arange(0, axis_dim, 2, dtype=np.float32) / axis_dim)
    ang = np.concatenate([row[:, None] * inv_freq, col[:, None] * inv_freq], axis=-1).astype(np.float32)
    return np.cos(ang), np.sin(ang)


def _mla_rope_lanes(n_tokens):
    cos, sin = _rope_tables(n_tokens, MLA_ROPE_DIM)
    one = np.ones((n_tokens, MLA_NOPE_DIM), np.float32)
    zero = np.zeros((n_tokens, MLA_NOPE_DIM), np.float32)
    pad0 = np.zeros((n_tokens, LANES - MLA_QK_DIM), np.float32)
    return (jnp.asarray(np.concatenate([one, cos, cos, pad0], axis=-1)),
            jnp.asarray(np.concatenate([zero, -sin, sin, pad0], axis=-1)))


def _gqa_rope_lanes(n_tokens):
    cos, sin = _rope_tables(n_tokens, GQA_HEAD_DIM)
    return (jnp.asarray(np.concatenate([cos, cos], axis=-1)),
            jnp.asarray(np.concatenate([-sin, sin], axis=-1)))


def _swap_halves(w):
    half = w.shape[-1] // 2
    return jnp.concatenate([w[..., half:], w[..., :half]], axis=-1)


def _even_weights(w_in, w_uq, w_ukv):
    d = w_in.shape[0]
    i0 = MLA_Q_LORA
    i1 = i0 + MLA_KV_LORA
    i2 = i1 + MLA_ROPE_DIM
    w_kr = w_in[:, i1:i2]
    zl = jnp.zeros((d, MLA_NOPE_DIM), F32)
    w_in_k = jnp.concatenate([w_in[:, :i1], w_in[:, i2:], zl, w_kr, w_kr], axis=-1).astype(BF16)
    r = w_uq.shape[0]
    uq = w_uq.reshape(r, MLA_HEADS, MLA_QK_DIM)
    w_uq_k = jnp.concatenate([uq, uq[..., MLA_NOPE_DIM:]], axis=-1).reshape(r, MLA_HEADS * LANES).astype(BF16)
    r = w_ukv.shape[0]
    ukv = w_ukv.reshape(r, MLA_HEADS, MLA_NOPE_DIM + MLA_V_DIM)
    k_pad = jnp.concatenate([ukv[..., :MLA_NOPE_DIM], jnp.zeros((r, MLA_HEADS, LANES - MLA_NOPE_DIM), F32)],
                            axis=-1).reshape(r, MLA_HEADS * LANES)
    v_cat = ukv[..., MLA_NOPE_DIM:].reshape(r, MLA_HEADS * MLA_V_DIM)
    w_ukv_k = jnp.concatenate([k_pad, v_cat], axis=-1).astype(BF16)
    return w_in_k, w_uq_k, w_ukv_k


def kernel(x_prompt, x_sample, cache_mla_ckv, cache_mla_krope, cache_na_k, cache_na_v, cache_gqa_k, cache_gqa_v, c, c_ctx, w_mod, b_mod, norm_mix, norm_ffn, norm_final, w_in_a, mla_q_norm, mla_w_uq, mla_kv_norm, mla_w_ukv, na_rpb, w_out_a, w_in_c, gqa_q_norm, gqa_k_norm, w_out_c, w_ffn_in, w_ffn_out):
    batch, seq, d = x_prompt.shape
    dec_batch, dec_seq, _ = x_sample.shape
    depth = w_mod.shape[0]
    past = cache_mla_ckv.shape[2]
    n_ctx = batch * seq
    n_lat = dec_batch * dec_seq

    cond = jnp.concatenate([c_ctx[None], c, jnp.zeros((8 - 1 - dec_batch, d), F32)], axis=0)
    mod = _modulation(cond, w_mod, b_mod).reshape(depth, 8, 6, d)

    xp = x_prompt.reshape(n_ctx, d)
    xs = x_sample.reshape(n_lat, d)
    cos_m, sin_m = _mla_rope_lanes(dec_seq)
    cos_g, sin_g = _gqa_rope_lanes(dec_seq)
    ident_cos = jnp.ones((TOKEN_TILE, LANES), F32)
    ident_sin = jnp.zeros((TOKEN_TILE, LANES), F32)
    keep_qk = jnp.asarray(np.broadcast_to(np.arange(LANES) < MLA_QK_DIM, (TOKEN_TILE, LANES)), F32)
    gfin = norm_final.reshape(1, d)
    states = {k: [] for k in ("ckv", "krope", "nk", "nv", "gk", "gv")}
    wfi = w_ffn_in.astype(BF16)
    wfo = w_ffn_out.astype(BF16)

    for l in range(depth):
        mod_p = mod[l, 0:1]
        mod_s = mod[l, 1:1 + dec_batch]
        gmix = norm_mix[l].reshape(1, d)
        gffn = norm_ffn[l].reshape(1, d)
        if l % 2 == 0:
            e = l // 2
            w_in_k, w_uq_k, w_ukv_k = _even_weights(w_in_a[e], mla_w_uq[e], mla_w_ukv[e])
            qn = (mla_q_norm[e] * MLA_SCALE).reshape(1, -1)
            kvn = mla_kv_norm[e].reshape(1, -1)
            (qp, kp, vp, nqp, nkp, nvp, s_ckv, s_kr, s_nk, s_nv) = _even_in(
                xp, mod_p, n_ctx, gmix, w_in_k, qn, kvn, w_uq_k, w_ukv_k, keep_qk, ident_sin, True)
            states["ckv"].append(s_ckv.reshape(batch, seq, MLA_KV_LORA))
            states["krope"].append(s_kr[:, MLA_NOPE_DIM:MLA_QK_DIM].reshape(batch, seq, MLA_ROPE_DIM))
            states["nk"].append(s_nk.reshape(batch, seq, NA_HEADS, NA_HEAD_DIM))
            states["nv"].append(s_nv.reshape(batch, seq, NA_HEADS, NA_HEAD_DIM))
            qs, ks, vs, nqs, nks, nvs = _even_in(
                xs, mod_s, dec_seq, gmix, w_in_k, qn, kvn, w_uq_k, w_ukv_k, cos_m, sin_m, False)
            kr_cache = jnp.pad(cache_mla_krope[:, e],
                               ((0, 0), (0, 0), (MLA_NOPE_DIM, LANES - MLA_QK_DIM)))
            kc, vc = _cache_expand(cache_mla_ckv[:, e], kr_cache, w_ukv_k)

            r3 = lambda a, b_: a.reshape(b_, a.shape[0] // b_, a.shape[1])
            mla_kw = dict(groups=MLA_HEADS // 2, heads=2, k_stride=LANES, q_half_mask=False, pair_out=True)
            na_kw = dict(groups=NA_HEADS // 2, heads=2, k_stride=0, q_half_mask=True, pair_out=True)
            a_mla_p = _ctx_attention(r3(qp, batch), r3(kp, batch), r3(vp, batch), name="mla_ctx",
                                     v_block=2 * LANES, **mla_kw)
            a_na_p = _ctx_attention(r3(nqp, batch), r3(nkp, batch), r3(nvp, batch), name="na_ctx",
                                    v_block=2 * LANES, **na_kw)
            a_mla_s = _attention(r3(qs, dec_batch), [(r3(ks, dec_batch), r3(vs, dec_batch)), (kc, vc)],
                                 q_tile=MLA_Q_TILE, name="mla_lat", **mla_kw)
            bias = _na_bias_tables(na_rpb[e])
            nv_cache = cache_na_v[:, e].astype(BF16).reshape(dec_batch, past, NA_HEADS // 2, LANES)
            nv_cache = jnp.concatenate([nv_cache, jnp.ones_like(nv_cache)], axis=-1)
            a_na_s = _neighbourhood_attention(
                r3(nqs, dec_batch), r3(nks, dec_batch), r3(nvs, dec_batch),
                cache_na_k[:, e].reshape(dec_batch, past, NA_W).astype(BF16),
                nv_cache.reshape(dec_batch, past, -1), bias)
            attn_p = [a_mla_p.reshape(n_ctx, -1), a_na_p.reshape(n_ctx, -1)]
            attn_s = [a_mla_s.reshape(n_lat, -1), a_na_s.reshape(n_lat, -1)]
            wo = w_out_a[e].astype(BF16)
            half = MLA_HEADS * MLA_V_DIM
            w_outs = [wo[:half], wo[half:]]
        else:
            o = l // 2
            w_in_k = w_in_c[o].astype(BF16)
            qn = gqa_q_norm[o] * GQA_SCALE
            qn = jnp.stack([qn, _swap_halves(qn)])
            kn = jnp.stack([gqa_k_norm[o], _swap_halves(gqa_k_norm[o])])
            qp, kp, vp, s_gk, s_gv = _odd_in(xp, mod_p, n_ctx, gmix, w_in_k, qn, kn, ident_cos, ident_sin, True)
            states["gk"].append(s_gk.reshape(batch, seq, GQA_KV_HEADS, GQA_HEAD_DIM))
            states["gv"].append(s_gv.reshape(batch, seq, GQA_KV_HEADS, GQA_HEAD_DIM))
            qs, ks, vs = _odd_in(xs, mod_s, dec_seq, gmix, w_in_k, qn, kn, cos_g, sin_g, False)
            r3 = lambda a, b_: a.reshape(b_, a.shape[0] // b_, a.shape[1])
            gqa_kw = dict(groups=GQA_KV_HEADS, heads=GQA_GROUP, k_stride=0, q_half_mask=False, pair_out=False)
            a_p = _ctx_attention(r3(qp, batch), r3(kp, batch), r3(vp, batch), name="gqa_ctx",
                                 v_block=2 * LANES, **gqa_kw)
            kcache = cache_gqa_k[:, o].reshape(dec_batch, past, -1).astype(BF16)
            vcache = cache_gqa_v[:, o].astype(BF16)
            vcache = jnp.concatenate([vcache, jnp.ones_like(vcache)], axis=-1).reshape(dec_batch, past, -1)
            a_s = _attention(r3(qs, dec_batch), [(r3(ks, dec_batch), r3(vs, dec_batch)), (kcache, vcache)],
                             q_tile=GQA_Q_TILE, name="gqa_lat", **gqa_kw)
            attn_p = [a_p.reshape(n_ctx, -1)]
            attn_s = [a_s.reshape(n_lat, -1)]
            w_outs = [w_out_c[o].astype(BF16)]
        last = l == depth - 1
        xp = _out_ffn(xp, mod_p, n_ctx, attn_p, w_outs, gffn, l, wfi, wfo, gfin, last)
        xs = _out_ffn(xs, mod_s, dec_seq, attn_s, w_outs, gffn, l, wfi, wfo, gfin, last)

    y_prompt = xp.reshape(batch, seq, d)
    y_sample = xs.reshape(dec_batch, dec_seq, d)
    return (y_prompt, y_sample,
            jnp.stack(states["ckv"], axis=1), jnp.stack(states["krope"], axis=1),
            jnp.stack(states["nk"], axis=1), jnp.stack(states["nv"], axis=1),
            jnp.stack(states["gk"], axis=1), jnp.stack(states["gv"], axis=1))
```

```python
import functools
import math

import numpy as np
import jax
import jax.numpy as jnp
from jax import lax
from jax.experimental import pallas as pl
from jax.experimental.pallas import tpu as pltpu

LANES = 128
V7X_VMEM_BYTES = 64 * 1024 * 1024

D_MODEL = 1024
GRID_W = 64
ROPE_THETA = 10000.0
RMS_EPS = 1e-6
NEG_INF = -1e30
MLA_HEADS = 8
MLA_Q_LORA = 256
MLA_KV_LORA = 256
MLA_NOPE_DIM = 64
MLA_ROPE_DIM = 32
MLA_V_DIM = 64
MLA_QK_DIM = MLA_NOPE_DIM + MLA_ROPE_DIM
LOG2E = math.log2(math.e)
MLA_SCALE = MLA_QK_DIM ** -0.5 * LOG2E
NA_HEADS = 8
NA_HEAD_DIM = 64
NA_WIN_H = 8
NA_WIN_W = 16
NA_SCALE = NA_HEAD_DIM ** -0.5 * LOG2E
NA_W = NA_HEADS * NA_HEAD_DIM
GQA_HEADS = 8
GQA_KV_HEADS = 2
GQA_HEAD_DIM = 128
GQA_SCALE = GQA_HEAD_DIM ** -0.5 * LOG2E
GQA_GROUP = GQA_HEADS // GQA_KV_HEADS

TOKEN_TILE = 512
MLA_Q_TILE = 1024
GQA_Q_TILE = 512
ATTN_K_CHUNK = 256
NA_Q_ROWS = 4
NA_BAND_ROWS = 12
NA_BLOCKS_PER_STEP = 4

BF16 = jnp.bfloat16
F32 = jnp.float32


def _vmem_limit(nbytes):
    return int(min(V7X_VMEM_BYTES - (4 << 20), max(nbytes, 16 << 20)))


def _params(nbytes, ndims):
    return pltpu.CompilerParams(dimension_semantics=("arbitrary",) * ndims,
                                vmem_limit_bytes=_vmem_limit(nbytes))


def _rms(x, gain):
    return x * lax.rsqrt(jnp.mean(x * x, axis=-1, keepdims=True) + RMS_EPS) * gain


def _dot(a, b):
    return jnp.dot(a, b, preferred_element_type=F32)


def _dot_nt(a, b):
    return lax.dot_general(a, b, (((1,), (1,)), ((), ())), preferred_element_type=F32)


def _store_pairs_with_ones(v_ref, index, v):
    ones = jnp.ones((v.shape[0], LANES), BF16)
    for i in range(v.shape[1] // LANES):
        v_ref[index + (slice(None), slice(2 * i * LANES, (2 * i + 1) * LANES))] = (
            v[:, i * LANES:(i + 1) * LANES].astype(BF16))
        v_ref[index + (slice(None), slice((2 * i + 1) * LANES, (2 * i + 2) * LANES))] = ones


def _const_spec(shape):
    nd = len(shape)
    return pl.BlockSpec(shape, lambda *_: (0,) * nd, pipeline_mode=pl.Buffered(1))


def _mod_kernel(cond_ref, w_ref, b_ref, o_ref):
    c = cond_ref[...]
    s = (c * jax.nn.sigmoid(c)).astype(BF16)
    o_ref[0] = _dot(s, w_ref[0].astype(BF16)) + b_ref[0]


def _modulation(cond, w_mod, b_mod):
    depth, d, n = w_mod.shape
    rows = cond.shape[0]
    bn = 1024
    return pl.pallas_call(
        _mod_kernel,
        out_shape=jax.ShapeDtypeStruct((depth, rows, n), F32),
        grid=(depth, n // bn),
        in_specs=[pl.BlockSpec((rows, d), lambda l, j: (0, 0)),
                  pl.BlockSpec((1, d, bn), lambda l, j: (l, 0, j)),
                  pl.BlockSpec((1, 1, bn), lambda l, j: (l, 0, j))],
        out_specs=pl.BlockSpec((1, rows, bn), lambda l, j: (l, 0, j)),
        compiler_params=_params(3 * d * bn * 4, 2),
        name="ada_modulation",
    )(cond, w_mod, b_mod.reshape(depth, 1, n))


def _even_in_kernel(x_ref, mod_ref, gmix_ref, w_in_ref, qn_ref, kvn_ref, w_uq_ref, w_ukv_ref,
                    cos_ref, sin_ref, *out_refs, with_state):
    q_ref, k_ref, v_ref, nq_ref, nk_ref, nv_ref = out_refs[:6]
    x = x_ref[...]
    mod = mod_ref[0]
    h = _rms(x, gmix_ref[...]) * (1.0 + mod[1:2]) + mod[0:1]
    p = _dot(h.astype(BF16), w_in_ref[...])
    cq = p[:, 0:256]
    ckv = _rms(p[:, 256:512], kvn_ref[...])
    nq = p[:, 512:1024]
    nk = p[:, 1024:1536]
    nv = p[:, 1536:2048]
    kr = p[:, 2048:2176]
    cos = cos_ref[...]
    sin = sin_ref[...]
    def rope(t):
        if with_state:
            return t * cos
        return t * cos + pltpu.roll(t, LANES - MLA_ROPE_DIM // 2, 1) * sin

    qq = _dot(_rms(cq, qn_ref[...]).astype(BF16), w_uq_ref[...])
    kv = _dot(ckv.astype(BF16), w_ukv_ref[...])
    kr_rot = rope(kr)
    for hd in range(MLA_HEADS):
        lo = hd * LANES
        q_ref[:, lo:lo + LANES] = rope(qq[:, lo:lo + LANES]).astype(BF16)
        k_ref[:, lo:lo + LANES] = (kv[:, lo:lo + LANES] + kr_rot).astype(BF16)
    _store_pairs_with_ones(v_ref, (), kv[:, 1024:1536])
    nq_ref[...] = (nq * NA_SCALE).astype(BF16)
    nk_ref[...] = nk.astype(BF16)
    _store_pairs_with_ones(nv_ref, (), nv)
    if with_state:
        s_ckv_ref, s_kr_ref, s_nk_ref, s_nv_ref = out_refs[6:]
        s_ckv_ref[...] = ckv
        s_kr_ref[...] = kr
        s_nk_ref[...] = nk
        s_nv_ref[...] = nv


def _even_in(x, mod, tokens_per_group, gmix, w_in, qn, kvn, w_uq, w_ukv, cos, sin, with_state):
    n = x.shape[0]
    tm = TOKEN_TILE
    tiles_per_group = tokens_per_group // tm
    rope_tiles = cos.shape[0] // tm
    row = lambda i: (i, 0)
    outs = [jax.ShapeDtypeStruct((n, 1024), BF16), jax.ShapeDtypeStruct((n, 1024), BF16),
            jax.ShapeDtypeStruct((n, 1024), BF16), jax.ShapeDtypeStruct((n, 512), BF16),
            jax.ShapeDtypeStruct((n, 512), BF16), jax.ShapeDtypeStruct((n, 1024), BF16)]
    if with_state:
        outs += [jax.ShapeDtypeStruct((n, 256), F32), jax.ShapeDtypeStruct((n, 128), F32),
                 jax.ShapeDtypeStruct((n, 512), F32), jax.ShapeDtypeStruct((n, 512), F32)]
    return pl.pallas_call(
        functools.partial(_even_in_kernel, with_state=with_state),
        out_shape=outs,
        grid=(n // tm,),
        in_specs=[pl.BlockSpec((tm, D_MODEL), row),
                  pl.BlockSpec((1, 6, D_MODEL), lambda i: (i // tiles_per_group, 0, 0)),
                  _const_spec(gmix.shape), _const_spec(w_in.shape), _const_spec(qn.shape),
                  _const_spec(kvn.shape), _const_spec(w_uq.shape), _const_spec(w_ukv.shape),
                  pl.BlockSpec((tm, LANES), lambda i: (i % rope_tiles, 0)),
                  pl.BlockSpec((tm, LANES), lambda i: (i % rope_tiles, 0))],
        out_specs=[pl.BlockSpec((tm, o.shape[1]), row) for o in outs],
        compiler_params=_params(40 << 20, 1),
        name="even_in",
    )(x, mod, gmix, w_in, qn, kvn, w_uq, w_ukv, cos, sin)


def _cache_expand_kernel(ckv_ref, kr_ref, w_ukv_ref, k_ref, v_ref):
    kv = _dot(ckv_ref[0].astype(BF16), w_ukv_ref[...])
    kr = kr_ref[0]
    for hd in range(MLA_HEADS):
        lo = hd * LANES
        k_ref[0, :, lo:lo + LANES] = (kv[:, lo:lo + LANES] + kr).astype(BF16)
    _store_pairs_with_ones(v_ref, (0,), kv[:, 1024:1536])


def _cache_expand(ckv, kr128, w_ukv):
    b, s, _ = ckv.shape
    return pl.pallas_call(
        _cache_expand_kernel,
        out_shape=[jax.ShapeDtypeStruct((b, s, 1024), BF16), jax.ShapeDtypeStruct((b, s, 1024), BF16)],
        grid=(b,),
        in_specs=[pl.BlockSpec((1, s, MLA_KV_LORA), lambda i: (i, 0, 0)),
                  pl.BlockSpec((1, s, LANES), lambda i: (i, 0, 0)),
                  _const_spec(w_ukv.shape)],
        out_specs=[pl.BlockSpec((1, s, 1024), lambda i: (i, 0, 0)),
                   pl.BlockSpec((1, s, 1024), lambda i: (i, 0, 0))],
        compiler_params=_params(16 << 20, 1),
        name="mla_cache_expand",
    )(ckv, kr128, w_ukv)


def _odd_in_kernel(x_ref, mod_ref, gmix_ref, w_in_ref, qn_ref, kn_ref, avg_ref, cos_ref, sin_ref, *out_refs,
                   with_state):
    q_ref, k_ref, v_ref = out_refs[:3]
    x = x_ref[...]
    mod = mod_ref[0]
    h = (_rms(x, gmix_ref[...]) * (1.0 + mod[1:2]) + mod[0:1]).astype(BF16)
    cos = cos_ref[...]
    sin = sin_ref[...]
    half = GQA_HEAD_DIM // 2
    q_cos, q_sin = qn_ref[0:1] * cos, qn_ref[1:2] * sin
    k_cos, k_sin = kn_ref[0:1] * cos, kn_ref[1:2] * sin

    pair = 2 * LANES

    def project(col):
        return _dot(h, w_in_ref[:, col:col + pair])

    def inv_rms(p):
        sq = p * p
        hi = sq.astype(BF16)
        lo = (sq - hi.astype(F32)).astype(BF16)
        return lax.rsqrt(_dot(hi, avg_ref[...]) + _dot(lo, avg_ref[...]) + RMS_EPS)

    k_off = GQA_HEADS * LANES
    v_off = k_off + GQA_KV_HEADS * LANES
    cols = [g * pair for g in range(GQA_HEADS // 2)] + [k_off, v_off]
    p_next = project(cols[0])
    for idx, col in enumerate(cols[:-1]):
        p, p_next = p_next, project(cols[idx + 1])
        r = inv_rms(p)
        for i in range(2):
            ph = p[:, i * LANES:(i + 1) * LANES]
            rh = r[:, i * LANES:(i + 1) * LANES]
            lo = i * LANES
            if col < k_off:
                qh = (ph * q_cos + pltpu.roll(ph, half, 1) * q_sin) * rh
                q_ref[:, col + lo:col + lo + LANES] = qh.astype(BF16)
            else:
                kh = (ph * k_cos + pltpu.roll(ph, half, 1) * k_sin) * rh
                k_ref[:, lo:lo + LANES] = kh.astype(BF16)
                if with_state:
                    out_refs[3][:, lo:lo + LANES] = ph * rh * kn_ref[0:1]
    ones = jnp.ones((x.shape[0], LANES), BF16)
    for hd in range(GQA_KV_HEADS):
        v_ref[:, 2 * hd * LANES:(2 * hd + 1) * LANES] = p_next[:, hd * LANES:(hd + 1) * LANES].astype(BF16)
        v_ref[:, (2 * hd + 1) * LANES:(2 * hd + 2) * LANES] = ones
    if with_state:
        out_refs[4][...] = p_next


def _odd_in(x, mod, tokens_per_group, gmix, w_in, qn, kn, cos, sin, with_state):
    n = x.shape[0]
    tm = TOKEN_TILE
    tiles_per_group = tokens_per_group // tm
    rope_tiles = cos.shape[0] // tm
    row = lambda i: (i, 0)
    outs = [jax.ShapeDtypeStruct((n, 1024), BF16), jax.ShapeDtypeStruct((n, 256), BF16),
            jax.ShapeDtypeStruct((n, 512), BF16)]
    if with_state:
        outs += [jax.ShapeDtypeStruct((n, 256), F32), jax.ShapeDtypeStruct((n, 256), F32)]
    avg = np.kron(np.eye(2), np.full((GQA_HEAD_DIM, GQA_HEAD_DIM), 1.0 / GQA_HEAD_DIM))
    avg = jnp.asarray(avg, BF16)
    return pl.pallas_call(
        functools.partial(_odd_in_kernel, with_state=with_state),
        out_shape=outs,
        grid=(n // tm,),
        in_specs=[pl.BlockSpec((tm, D_MODEL), row),
                  pl.BlockSpec((1, 6, D_MODEL), lambda i: (i // tiles_per_group, 0, 0)),
                  _const_spec(gmix.shape), _const_spec(w_in.shape), _const_spec(qn.shape),
                  _const_spec(kn.shape), _const_spec(avg.shape),
                  pl.BlockSpec((tm, LANES), lambda i: (i % rope_tiles, 0)),
                  pl.BlockSpec((tm, LANES), lambda i: (i % rope_tiles, 0))],
        out_specs=[pl.BlockSpec((tm, o.shape[1]), row) for o in outs],
        compiler_params=_params(32 << 20, 1),
        name="odd_in",
    )(x, mod, gmix, w_in, qn, kn, avg, cos, sin)


def _attn_kernel(*refs, n_src, heads, q_tile, src_len, k_stride, q_half_mask, pair_out):
    q_ref = refs[0]
    kv_refs = refs[1:1 + 2 * n_src]
    o_ref = refs[1 + 2 * n_src]
    s_ref = refs[2 + 2 * n_src]
    n_tiles = q_ref.shape[1] // q_tile
    chunks = []
    for src in range(n_src):
        ck = min(ATTN_K_CHUNK, src_len[src])
        for c in range(src_len[src] // ck):
            chunks.append((src, c * ck, ck))
    lane = lax.broadcasted_iota(jnp.int32, (q_tile, LANES), 1)

    def rows(t):
        if isinstance(t, int):
            return slice(t * q_tile, (t + 1) * q_tile)
        return pl.ds(pl.multiple_of(t * q_tile, q_tile), q_tile)

    def load_q(t, j):
        if q_half_mask:
            qb = q_ref[0, rows(t), :]
            return jnp.where((lane >= 64) == (j == 1), qb, jnp.zeros_like(qb))
        return q_ref[0, rows(t), j * LANES:(j + 1) * LANES]

    def slot(j, q_next, m_prev):
        if q_next is not None:
            m_part = jnp.full((q_tile, LANES), -jnp.inf, F32)
        if m_prev is not None:
            acc = jnp.zeros((q_tile, 2 * LANES), F32)
        off = 0
        for src, k0, ck in chunks:
            if q_next is not None:
                kc = kv_refs[2 * src][0, j * k_stride:j * k_stride + LANES, k0:k0 + ck]
                s_new = _dot(q_next, kc)
            if m_prev is not None:
                s_old = s_ref[:, off:off + ck]
            if q_next is not None:
                s_ref[:, off:off + ck] = s_new
                for i in range(ck // LANES):
                    m_part = jnp.maximum(m_part, s_new[:, i * LANES:(i + 1) * LANES])
            if m_prev is not None:
                p = jnp.exp2(s_old - m_prev)
                acc = acc + _dot(p.astype(BF16), kv_refs[2 * src + 1][0, k0:k0 + ck, :])
            off += ck
        m_next = None if q_next is None else jnp.max(m_part, axis=-1, keepdims=True)
        o_prev = None if m_prev is None else acc[:, :LANES] / acc[:, LANES:]
        return m_next, o_prev

    def write_out(t, j, o):
        o = o.astype(o_ref.dtype)
        if not pair_out:
            o_ref[0, rows(t), j * LANES:(j + 1) * LANES] = o
        elif j == 0:
            o_ref[0, rows(t), :] = o
        else:
            o_ref[0, rows(t), :] = jnp.where(lane < 64, o_ref[0, rows(t), :], o)

    def tile(j, t, m, last):
        m_next, o = slot(j, None if last else load_q(t + 1, j), m)
        write_out(t, j, o)
        return m_next

    for j in range(heads):
        m, _ = slot(j, load_q(0, j), None)
        if n_tiles > 1:
            m = lax.fori_loop(0, n_tiles - 1, lambda t, m, j=j: tile(j, t, m, False), m)
        tile(j, n_tiles - 1, m, True)


def _attention(q, sources, *, groups, heads, k_stride, q_half_mask, pair_out, q_tile, name):
    b, t, _ = q.shape
    q_block = LANES if q_half_mask else heads * LANES
    k_block = LANES if k_stride == 0 else heads * LANES
    out_block = LANES if pair_out else heads * LANES
    src_len = tuple(k.shape[1] for k, _ in sources)
    in_specs = [pl.BlockSpec((1, t, q_block), lambda bi, g: (bi, 0, g))]
    args = [q]
    for k, v in sources:
        s = k.shape[1]
        in_specs.append(pl.BlockSpec((1, k_block, s), lambda bi, g: (bi, g, 0)))
        in_specs.append(pl.BlockSpec((1, s, 2 * LANES), lambda bi, g: (bi, 0, g)))
        args += [jnp.swapaxes(k, 1, 2), v]
    total = sum(src_len)
    return pl.pallas_call(
        functools.partial(_attn_kernel, n_src=len(sources), heads=heads, q_tile=q_tile,
                          src_len=src_len, k_stride=k_stride, q_half_mask=q_half_mask,
                          pair_out=pair_out),
        out_shape=jax.ShapeDtypeStruct((b, t, groups * out_block), BF16),
        grid=(b, groups),
        in_specs=in_specs,
        out_specs=pl.BlockSpec((1, t, out_block), lambda bi, g: (bi, 0, g)),
        scratch_shapes=[pltpu.VMEM((q_tile, total), F32)],
        compiler_params=_params(58 << 20, 2),
        name=name,
    )(*args)


def _ctx_attn_kernel(q_ref, k_ref, v_ref, o_ref, *, heads, k_stride, q_half_mask, pair_out):
    nb, t, _ = q_ref.shape
    lane = lax.broadcasted_iota(jnp.int32, (t, LANES), 1)
    for b in range(nb):
        outs = []
        for j in range(heads):
            if q_half_mask:
                qb = q_ref[b]
                q = jnp.where((lane >= 64) == (j == 1), qb, jnp.zeros_like(qb))
            else:
                q = q_ref[b, :, j * LANES:(j + 1) * LANES]
            s = _dot_nt(q, k_ref[b, :, j * k_stride:j * k_stride + LANES])
            p = jnp.exp2(s - jnp.max(s, axis=-1, keepdims=True))
            acc = _dot(p.astype(BF16), v_ref[b, :, :LANES])
            outs.append(acc / jnp.sum(p, axis=-1, keepdims=True))
        if pair_out:
            o_ref[b] = jnp.where(lane < 64, outs[0], outs[1]).astype(o_ref.dtype)
        else:
            for j in range(heads):
                o_ref[b, :, j * LANES:(j + 1) * LANES] = outs[j].astype(o_ref.dtype)


def _ctx_attention(q, k, v, *, groups, heads, k_stride, q_half_mask, pair_out, name, v_block=LANES):
    b, t, _ = q.shape
    nb = 4
    q_block = LANES if q_half_mask else heads * LANES
    k_block = LANES if k_stride == 0 else heads * LANES
    out_block = LANES if pair_out else heads * LANES
    spec = lambda w: pl.BlockSpec((nb, t, w), lambda bi, g: (bi, 0, g))
    return pl.pallas_call(
        functools.partial(_ctx_attn_kernel, heads=heads, k_stride=k_stride, q_half_mask=q_half_mask,
                          pair_out=pair_out),
        out_shape=jax.ShapeDtypeStruct((b, t, groups * out_block), BF16),
        grid=(b // nb, groups),
        in_specs=[spec(q_block), spec(k_block), spec(v_block)],
        out_specs=spec(out_block),
        compiler_params=_params(32 << 20, 2),
        name=name,
    )(q, k, v)


def _na_bias_tables(rpb):
    n_rows = GRID_W
    h, n_dr, n_dc = rpb.shape
    edge = n_dc - 1 - (NA_WIN_W - 1)
    w = jnp.concatenate([rpb[..., NA_WIN_W - 1:],
                         jnp.broadcast_to(rpb[..., n_dc - 1:], (h, n_dr, GRID_W - 1 - edge)),
                         jnp.broadcast_to(rpb[..., :1], (h, n_dr, GRID_W - (NA_WIN_W - 1) + 1)),
                         rpb[..., 1:NA_WIN_W - 1]], axis=-1)
    toe = jnp.tile(w, (1, 1, GRID_W))[..., :GRID_W * (2 * GRID_W - 1)]
    toe = toe.reshape(h, n_dr, GRID_W, 2 * GRID_W - 1)[..., :GRID_W]
    cols = np.arange(GRID_W)
    cs = np.clip(cols - NA_WIN_W // 2, 0, GRID_W - NA_WIN_W)
    col_ok = (cols[None, :] >= cs[:, None]) & (cols[None, :] < cs[:, None] + NA_WIN_W)
    toe = jnp.where(col_ok, toe * LOG2E, NEG_INF)
    toe = toe.transpose(0, 2, 1, 3).reshape(h, GRID_W, n_dr * GRID_W)
    pieces = []
    for blk in (0, 1, n_rows // NA_Q_ROWS - 1):
        b0 = int(np.clip(NA_Q_ROWS * blk - NA_WIN_H // 2, 0, n_rows - NA_BAND_ROWS))
        for qr in range(NA_Q_ROWS):
            r = NA_Q_ROWS * blk + qr
            rs = int(np.clip(r - NA_WIN_H // 2, 0, n_rows - NA_WIN_H))
            dr0 = rs - r + NA_WIN_H - 1
            seen = toe[:, None, :, dr0 * GRID_W:(dr0 + NA_WIN_H) * GRID_W]
            before = (rs - b0) * GRID_W
            after = (NA_BAND_ROWS - NA_WIN_H) * GRID_W - before
            pieces.append(jnp.pad(seen, ((0, 0), (0, 0), (0, 0), (before, after)), constant_values=NEG_INF))
    return jnp.concatenate(pieces, axis=1).reshape(h, 3, NA_Q_ROWS * GRID_W, NA_BAND_ROWS * GRID_W)


def _na_kernel(q_ref, k_ref, v_ref, kc_ref, vc_ref, bias_ref, o_ref, s_ref):
    nq = NA_Q_ROWS * GRID_W
    nb = NA_BAND_ROWS * GRID_W
    per = NA_BLOCKS_PER_STEP
    t = k_ref.shape[1]
    n_blocks = t // nq
    n_groups = n_blocks // per
    lane = lax.broadcasted_iota(jnp.int32, (nq, LANES), 1)

    def block_rows(i):
        if isinstance(i, int):
            return slice(i * nq, (i + 1) * nq)
        return pl.ds(pl.multiple_of(i * nq, nq), nq)

    def band_rows(i):
        first = nq * i - (NA_WIN_H // 2) * GRID_W
        if isinstance(i, int):
            first = min(max(first, 0), t - nb)
            return slice(first, first + nb)
        return pl.ds(pl.multiple_of(jnp.clip(first, 0, t - nb), nq), nb)

    def bias_class(i):
        if isinstance(i, int):
            return min(i, 1) + max(i - (n_blocks - 2), 0)
        return jnp.minimum(i, 1) + jnp.maximum(i - (n_blocks - 2), 0)

    def slot(j, g_next, g_prev, ms_prev):
        new_ms = []
        for blk in range(per):
            srow = slice(blk * nq, (blk + 1) * nq)
            if g_next is not None:
                i = g_next * per + blk
                qb = q_ref[0, block_rows(i), :]
                q = jnp.where((lane >= 64) == (j == 1), qb, jnp.zeros_like(qb))
                s_band = _dot_nt(q, k_ref[0, band_rows(i), :]) + bias_ref[j, bias_class(i)]
                s_ctx = _dot_nt(q, kc_ref[0])
            if g_prev is not None:
                old_band = s_ref[srow, :nb]
                old_ctx = s_ref[srow, nb:]
            if g_next is not None:
                s_ref[srow, :nb] = s_band
                s_ref[srow, nb:] = s_ctx
                new_ms.append(jnp.maximum(jnp.max(s_band, axis=-1, keepdims=True),
                                          jnp.max(s_ctx, axis=-1, keepdims=True)))
            if g_prev is not None:
                i = g_prev * per + blk
                m = ms_prev[blk]
                p_band = jnp.exp2(old_band - m)
                p_ctx = jnp.exp2(old_ctx - m)
                acc = (_dot(p_band.astype(BF16), v_ref[0, band_rows(i), :])
                       + _dot(p_ctx.astype(BF16), vc_ref[0]))
                o = (acc[:, :LANES] / acc[:, LANES:]).astype(o_ref.dtype)
                if j == 0:
                    o_ref[0, block_rows(i), :] = o
                else:
                    o_ref[0, block_rows(i), :] = jnp.where(lane < 64, o_ref[0, block_rows(i), :], o)
        return tuple(new_ms)

    for j in range(2):
        ms = slot(j, 0, None, None)
        if n_groups > 1:
            ms = lax.fori_loop(0, n_groups - 1, lambda g, ms, j=j: slot(j, g + 1, g, ms), ms)
        slot(j, None, n_groups - 1, ms)


def _neighbourhood_attention(q, k, v, kc, vc, bias):
    b, t, w = q.shape
    pairs = w // LANES
    nq = NA_Q_ROWS * GRID_W
    nb = NA_BAND_ROWS * GRID_W
    c = kc.shape[1]
    seq = lambda n, lanes=LANES: pl.BlockSpec((1, n, lanes), lambda bi, g: (bi, 0, g))
    return pl.pallas_call(
        _na_kernel,
        out_shape=jax.ShapeDtypeStruct((b, t, w), BF16),
        grid=(b, pairs),
        in_specs=[seq(t), seq(t), seq(t, 2 * LANES), seq(c), seq(c, 2 * LANES),
                  pl.BlockSpec((2, 3, nq, nb), lambda bi, g: (g, 0, 0, 0))],
        out_specs=seq(t),
        scratch_shapes=[pltpu.VMEM((NA_BLOCKS_PER_STEP * nq, nb + c), F32)],
        compiler_params=_params(40 << 20, 2),
        name="neighbourhood_attention",
    )(q, k, v, kc, vc, bias)


def _out_ffn_kernel(*refs, n_attn, ff_chunk, final_norm):
    x_ref, mod_ref = refs[0], refs[1]
    a_refs = refs[2:2 + n_attn]
    w_refs = refs[2 + n_attn:2 + 2 * n_attn]
    gffn_ref, w_in_ref, w_out_ref, gfin_ref, o_ref = refs[2 + 2 * n_attn:]
    mod = mod_ref[0]
    mix = _dot(a_refs[0][...], w_refs[0][...])
    for a_ref, w_ref in zip(a_refs[1:], w_refs[1:]):
        mix = mix + _dot(a_ref[...], w_ref[...])
    x1 = x_ref[...] + mod[2:3] * mix
    h = (_rms(x1, gffn_ref[...]) * (1.0 + mod[4:5]) + mod[3:4]).astype(BF16)
    d_ff = w_out_ref.shape[1]
    acc = None
    for c in range(d_ff // ff_chunk):
        lo = c * ff_chunk
        gate = _dot(h, w_in_ref[0, :, lo:lo + ff_chunk])
        up = _dot(h, w_in_ref[0, :, d_ff + lo:d_ff + lo + ff_chunk])
        act = (gate * jax.nn.sigmoid(gate) * up).astype(BF16)
        part = _dot(act, w_out_ref[0, lo:lo + ff_chunk, :])
        acc = part if acc is None else acc + part
    x2 = x1 + mod[5:6] * acc
    if final_norm:
        x2 = _rms(x2, gfin_ref[...])
    o_ref[...] = x2


def _out_ffn(x, mod, tokens_per_group, attn, w_outs, gffn, layer, w_ffn_in, w_ffn_out, gfin, final_norm):
    n = x.shape[0]
    tm = TOKEN_TILE
    tiles_per_group = tokens_per_group // tm
    row = lambda i: (i, 0)
    in_specs = [pl.BlockSpec((tm, D_MODEL), row),
                pl.BlockSpec((1, 6, D_MODEL), lambda i: (i // tiles_per_group, 0, 0))]
    in_specs += [pl.BlockSpec((tm, a.shape[1]), row) for a in attn]
    in_specs += [_const_spec(w.shape) for w in w_outs]
    layer_spec = lambda w: pl.BlockSpec((1,) + w.shape[1:], lambda i: (layer, 0, 0),
                                        pipeline_mode=pl.Buffered(1))
    in_specs += [_const_spec(gffn.shape), layer_spec(w_ffn_in), layer_spec(w_ffn_out),
                 _const_spec(gfin.shape)]
    return pl.pallas_call(
        functools.partial(_out_ffn_kernel, n_attn=len(attn), ff_chunk=256, final_norm=final_norm),
        out_shape=jax.ShapeDtypeStruct((n, D_MODEL), F32),
        grid=(n // tm,),
        in_specs=in_specs,
        out_specs=pl.BlockSpec((tm, D_MODEL), row),
        compiler_params=_params(56 << 20, 1),
        name="out_ffn",
    )(x, mod, *attn, *w_outs, gffn, w_ffn_in, w_ffn_out, gfin)


def _rope_tables(n_tokens, rot_dim):
    t = np.arange(n_tokens)
    row = (t // GRID_W).astype(np.float32)
    col = (t % GRID_W).astype(np.float32)
    axis_dim = rot_dim // 2
    inv_freq = np.float32(ROPE_THETA) ** (-np.arange(0, axis_dim, 2, dtype=np.float32) / axis_dim)
    ang = np.concatenate([row[:, None] * inv_freq, col[:, None] * inv_freq], axis=-1).astype(np.float32)
    return np.cos(ang), np.sin(ang)


def _mla_rope_lanes(n_tokens):
    cos, sin = _rope_tables(n_tokens, MLA_ROPE_DIM)
    one = np.ones((n_tokens, MLA_NOPE_DIM), np.float32)
    zero = np.zeros((n_tokens, MLA_NOPE_DIM), np.float32)
    pad0 = np.zeros((n_tokens, LANES - MLA_QK_DIM), np.float32)
    return (jnp.asarray(np.concatenate([one, cos, cos, pad0], axis=-1)),
            jnp.asarray(np.concatenate([zero, -sin, sin, pad0], axis=-1)))


def _gqa_rope_lanes(n_tokens):
    cos, sin = _rope_tables(n_tokens, GQA_HEAD_DIM)
    return (jnp.asarray(np.concatenate([cos, cos], axis=-1)),
            jnp.asarray(np.concatenate([-sin, sin], axis=-1)))


def _swap_halves(w):
    half = w.shape[-1] // 2
    return jnp.concatenate([w[..., half:], w[..., :half]], axis=-1)


def _even_weights(w_in, w_uq, w_ukv):
    d = w_in.shape[0]
    i0 = MLA_Q_LORA
    i1 = i0 + MLA_KV_LORA
    i2 = i1 + MLA_ROPE_DIM
    w_kr = w_in[:, i1:i2]
    zl = jnp.zeros((d, MLA_NOPE_DIM), F32)
    w_in_k = jnp.concatenate([w_in[:, :i1], w_in[:, i2:], zl, w_kr, w_kr], axis=-1).astype(BF16)
    r = w_uq.shape[0]
    uq = w_uq.reshape(r, MLA_HEADS, MLA_QK_DIM)
    w_uq_k = jnp.concatenate([uq, uq[..., MLA_NOPE_DIM:]], axis=-1).reshape(r, MLA_HEADS * LANES).astype(BF16)
    r = w_ukv.shape[0]
    ukv = w_ukv.reshape(r, MLA_HEADS, MLA_NOPE_DIM + MLA_V_DIM)
    k_pad = jnp.concatenate([ukv[..., :MLA_NOPE_DIM], jnp.zeros((r, MLA_HEADS, LANES - MLA_NOPE_DIM), F32)],
                            axis=-1).reshape(r, MLA_HEADS * LANES)
    v_cat = ukv[..., MLA_NOPE_DIM:].reshape(r, MLA_HEADS * MLA_V_DIM)
    w_ukv_k = jnp.concatenate([k_pad, v_cat], axis=-1).astype(BF16)
    return w_in_k, w_uq_k, w_ukv_k


def kernel(x_prompt, x_sample, cache_mla_ckv, cache_mla_krope, cache_na_k, cache_na_v, cache_gqa_k, cache_gqa_v, c, c_ctx, w_mod, b_mod, norm_mix, norm_ffn, norm_final, w_in_a, mla_q_norm, mla_w_uq, mla_kv_norm, mla_w_ukv, na_rpb, w_out_a, w_in_c, gqa_q_norm, gqa_k_norm, w_out_c, w_ffn_in, w_ffn_out):
    batch, seq, d = x_prompt.shape
    dec_batch, dec_seq, _ = x_sample.shape
    depth = w_mod.shape[0]
    past = cache_mla_ckv.shape[2]
    n_ctx = batch * seq
    n_lat = dec_batch * dec_seq

    cond = jnp.concatenate([c_ctx[None], c, jnp.zeros((8 - 1 - dec_batch, d), F32)], axis=0)
    mod = _modulation(cond, w_mod, b_mod).reshape(depth, 8, 6, d)

    xp = x_prompt.reshape(n_ctx, d)
    xs = x_sample.reshape(n_lat, d)
    cos_m, sin_m = _mla_rope_lanes(dec_seq)
    cos_g, sin_g = _gqa_rope_lanes(dec_seq)
    ident_cos = jnp.ones((TOKEN_TILE, LANES), F32)
    ident_sin = jnp.zeros((TOKEN_TILE, LANES), F32)
    keep_qk = jnp.asarray(np.broadcast_to(np.arange(LANES) < MLA_QK_DIM, (TOKEN_TILE, LANES)), F32)
    gfin = norm_final.reshape(1, d)
    states = {k: [] for k in ("ckv", "krope", "nk", "nv", "gk", "gv")}
    wfi = w_ffn_in.astype(BF16)
    wfo = w_ffn_out.astype(BF16)

    for l in range(depth):
        mod_p = mod[l, 0:1]
        mod_s = mod[l, 1:1 + dec_batch]
        gmix = norm_mix[l].reshape(1, d)
        gffn = norm_ffn[l].reshape(1, d)
        if l % 2 == 0:
            e = l // 2
            w_in_k, w_uq_k, w_ukv_k = _even_weights(w_in_a[e], mla_w_uq[e], mla_w_ukv[e])
            qn = (mla_q_norm[e] * MLA_SCALE).reshape(1, -1)
            kvn = mla_kv_norm[e].reshape(1, -1)
            (qp, kp, vp, nqp, nkp, nvp, s_ckv, s_kr, s_nk, s_nv) = _even_in(
                xp, mod_p, n_ctx, gmix, w_in_k, qn, kvn, w_uq_k, w_ukv_k, keep_qk, ident_sin, True)
            states["ckv"].append(s_ckv.reshape(batch, seq, MLA_KV_LORA))
            states["krope"].append(s_kr[:, MLA_NOPE_DIM:MLA_QK_DIM].reshape(batch, seq, MLA_ROPE_DIM))
            states["nk"].append(s_nk.reshape(batch, seq, NA_HEADS, NA_HEAD_DIM))
            states["nv"].append(s_nv.reshape(batch, seq, NA_HEADS, NA_HEAD_DIM))
            qs, ks, vs, nqs, nks, nvs = _even_in(
                xs, mod_s, dec_seq, gmix, w_in_k, qn, kvn, w_uq_k, w_ukv_k, cos_m, sin_m, False)
            kr_cache = jnp.pad(cache_mla_krope[:, e],
                               ((0, 0), (0, 0), (MLA_NOPE_DIM, LANES - MLA_QK_DIM)))
            kc, vc = _cache_expand(cache_mla_ckv[:, e], kr_cache, w_ukv_k)

            r3 = lambda a, b_: a.reshape(b_, a.shape[0] // b_, a.shape[1])
            mla_kw = dict(groups=MLA_HEADS // 2, heads=2, k_stride=LANES, q_half_mask=False, pair_out=True)
            na_kw = dict(groups=NA_HEADS // 2, heads=2, k_stride=0, q_half_mask=True, pair_out=True)
            a_mla_p = _ctx_attention(r3(qp, batch), r3(kp, batch), r3(vp, batch), name="mla_ctx",
                                     v_block=2 * LANES, **mla_kw)
            a_na_p = _ctx_attention(r3(nqp, batch), r3(nkp, batch), r3(nvp, batch), name="na_ctx",
                                    v_block=2 * LANES, **na_kw)
            a_mla_s = _attention(r3(qs, dec_batch), [(r3(ks, dec_batch), r3(vs, dec_batch)), (kc, vc)],
                                 q_tile=MLA_Q_TILE, name="mla_lat", **mla_kw)
            bias = _na_bias_tables(na_rpb[e])
            nv_cache = cache_na_v[:, e].astype(BF16).reshape(dec_batch, past, NA_HEADS // 2, LANES)
            nv_cache = jnp.concatenate([nv_cache, jnp.ones_like(nv_cache)], axis=-1)
            a_na_s = _neighbourhood_attention(
                r3(nqs, dec_batch), r3(nks, dec_batch), r3(nvs, dec_batch),
                cache_na_k[:, e].reshape(dec_batch, past, NA_W).astype(BF16),
                nv_cache.reshape(dec_batch, past, -1), bias)
            attn_p = [a_mla_p.reshape(n_ctx, -1), a_na_p.reshape(n_ctx, -1)]
            attn_s = [a_mla_s.reshape(n_lat, -1), a_na_s.reshape(n_lat, -1)]
            wo = w_out_a[e].astype(BF16)
            half = MLA_HEADS * MLA_V_DIM
            w_outs = [wo[:half], wo[half:]]
        else:
            o = l // 2
            w_in_k = w_in_c[o].astype(BF16)
            qn = gqa_q_norm[o] * GQA_SCALE
            qn = jnp.stack([qn, _swap_halves(qn)])
            kn = jnp.stack([gqa_k_norm[o], _swap_halves(gqa_k_norm[o])])
            qp, kp, vp, s_gk, s_gv = _odd_in(xp, mod_p, n_ctx, gmix, w_in_k, qn, kn, ident_cos, ident_sin, True)
            states["gk"].append(s_gk.reshape(batch, seq, GQA_KV_HEADS, GQA_HEAD_DIM))
            states["gv"].append(s_gv.reshape(batch, seq, GQA_KV_HEADS, GQA_HEAD_DIM))
            qs, ks, vs = _odd_in(xs, mod_s, dec_seq, gmix, w_in_k, qn, kn, cos_g, sin_g, False)
            r3 = lambda a, b_: a.reshape(b_, a.shape[0] // b_, a.shape[1])
            gqa_kw = dict(groups=GQA_KV_HEADS, heads=GQA_GROUP, k_stride=0, q_half_mask=False, pair_out=False)
            a_p = _ctx_attention(r3(qp, batch), r3(kp, batch), r3(vp, batch), name="gqa_ctx",
                                 v_block=2 * LANES, **gqa_kw)
            kcache = cache_gqa_k[:, o].reshape(dec_batch, past, -1).astype(BF16)
            vcache = cache_gqa_v[:, o].astype(BF16)
            vcache = jnp.concatenate([vcache, jnp.ones_like(vcache)], axis=-1).reshape(dec_batch, past, -1)
            a_s = _attention(r3(qs, dec_batch), [(r3(ks, dec_batch), r3(vs, dec_batch)), (kcache, vcache)],
                             q_tile=GQA_Q_TILE, name="gqa_lat", **gqa_kw)
            attn_p = [a_p.reshape(n_ctx, -1)]
            attn_s = [a_s.reshape(n_lat, -1)]
            w_outs = [w_out_c[o].astype(BF16)]
        last = l == depth - 1
        xp = _out_ffn(xp, mod_p, n_ctx, attn_p, w_outs, gffn, l, wfi, wfo, gfin, last)
        xs = _out_ffn(xs, mod_s, dec_seq, attn_s, w_outs, gffn, l, wfi, wfo, gfin, last)

    y_prompt = xp.reshape(batch, seq, d)
    y_sample = xs.reshape(dec_batch, dec_seq, d)
    return (y_prompt, y_sample,
            jnp.stack(states["ckv"], axis=1), jnp.stack(states["krope"], axis=1),
            jnp.stack(states["nk"], axis=1), jnp.stack(states["nv"], axis=1),
            jnp.stack(states["gk"], axis=1), jnp.stack(states["gv"], axis=1))
```

```python
import functools
import math

import numpy as np
import jax
import jax.numpy as jnp
from jax import lax
from jax.experimental import pallas as pl
from jax.experimental.pallas import tpu as pltpu

LANES = 128
V7X_VMEM_BYTES = 64 * 1024 * 1024

D_MODEL = 1024
GRID_W = 64
ROPE_THETA = 10000.0
RMS_EPS = 1e-6
NEG_INF = -1e30
MLA_HEADS = 8
MLA_Q_LORA = 256
MLA_KV_LORA = 256
MLA_NOPE_DIM = 64
MLA_ROPE_DIM = 32
MLA_V_DIM = 64
MLA_QK_DIM = MLA_NOPE_DIM + MLA_ROPE_DIM
LOG2E = math.log2(math.e)
MLA_SCALE = MLA_QK_DIM ** -0.5 * LOG2E
NA_HEADS = 8
NA_HEAD_DIM = 64
NA_WIN_H = 8
NA_WIN_W = 16
NA_SCALE = NA_HEAD_DIM ** -0.5 * LOG2E
NA_W = NA_HEADS * NA_HEAD_DIM
GQA_HEADS = 8
GQA_KV_HEADS = 2
GQA_HEAD_DIM = 128
GQA_SCALE = GQA_HEAD_DIM ** -0.5 * LOG2E
GQA_GROUP = GQA_HEADS // GQA_KV_HEADS

TOKEN_TILE = 512
MLA_Q_TILE = 1024
GQA_Q_TILE = 512
ATTN_K_CHUNK = 512
NA_Q_ROWS = 4
NA_BAND_ROWS = 12
NA_BLOCKS_PER_STEP = 4

BF16 = jnp.bfloat16
F32 = jnp.float32


def _vmem_limit(nbytes):
    return int(min(V7X_VMEM_BYTES - (4 << 20), max(nbytes, 16 << 20)))


def _params(nbytes, ndims):
    return pltpu.CompilerParams(dimension_semantics=("arbitrary",) * ndims,
                                vmem_limit_bytes=_vmem_limit(nbytes))


def _rms(x, gain):
    return x * lax.rsqrt(jnp.mean(x * x, axis=-1, keepdims=True) + RMS_EPS) * gain


def _dot(a, b):
    return jnp.dot(a, b, preferred_element_type=F32)


def _dot_nt(a, b):
    return lax.dot_general(a, b, (((1,), (1,)), ((), ())), preferred_element_type=F32)


def _store_pairs_with_ones(v_ref, index, v):
    ones = jnp.ones((v.shape[0], LANES), BF16)
    for i in range(v.shape[1] // LANES):
        v_ref[index + (slice(None), slice(2 * i * LANES, (2 * i + 1) * LANES))] = (
            v[:, i * LANES:(i + 1) * LANES].astype(BF16))
        v_ref[index + (slice(None), slice((2 * i + 1) * LANES, (2 * i + 2) * LANES))] = ones


def _const_spec(shape):
    nd = len(shape)
    return pl.BlockSpec(shape, lambda *_: (0,) * nd, pipeline_mode=pl.Buffered(1))


def _mod_kernel(cond_ref, w_ref, b_ref, o_ref):
    c = cond_ref[...]
    s = (c * jax.nn.sigmoid(c)).astype(BF16)
    o_ref[0] = _dot(s, w_ref[0].astype(BF16)) + b_ref[0]


def _modulation(cond, w_mod, b_mod):
    depth, d, n = w_mod.shape
    rows = cond.shape[0]
    bn = 1024
    return pl.pallas_call(
        _mod_kernel,
        out_shape=jax.ShapeDtypeStruct((depth, rows, n), F32),
        grid=(depth, n // bn),
        in_specs=[pl.BlockSpec((rows, d), lambda l, j: (0, 0)),
                  pl.BlockSpec((1, d, bn), lambda l, j: (l, 0, j)),
                  pl.BlockSpec((1, 1, bn), lambda l, j: (l, 0, j))],
        out_specs=pl.BlockSpec((1, rows, bn), lambda l, j: (l, 0, j)),
        compiler_params=_params(3 * d * bn * 4, 2),
        name="ada_modulation",
    )(cond, w_mod, b_mod.reshape(depth, 1, n))


def _even_in_kernel(x_ref, mod_ref, gmix_ref, w_in_ref, qn_ref, kvn_ref, w_uq_ref, w_ukv_ref,
                    cos_ref, sin_ref, *out_refs, with_state):
    q_ref, k_ref, v_ref, nq_ref, nk_ref, nv_ref = out_refs[:6]
    x = x_ref[...]
    mod = mod_ref[0]
    h = _rms(x, gmix_ref[...]) * (1.0 + mod[1:2]) + mod[0:1]
    p = _dot(h.astype(BF16), w_in_ref[...])
    cq = p[:, 0:256]
    ckv = _rms(p[:, 256:512], kvn_ref[...])
    nq = p[:, 512:1024]
    nk = p[:, 1024:1536]
    nv = p[:, 1536:2048]
    kr = p[:, 2048:2176]
    cos = cos_ref[...]
    sin = sin_ref[...]
    def rope(t):
        if with_state:
            return t * cos
        return t * cos + pltpu.roll(t, LANES - MLA_ROPE_DIM // 2, 1) * sin

    qq = _dot(_rms(cq, qn_ref[...]).astype(BF16), w_uq_ref[...])
    kv = _dot(ckv.astype(BF16), w_ukv_ref[...])
    kr_rot = rope(kr)
    for hd in range(MLA_HEADS):
        lo = hd * LANES
        q_ref[:, lo:lo + LANES] = rope(qq[:, lo:lo + LANES]).astype(BF16)
        k_ref[:, lo:lo + LANES] = (kv[:, lo:lo + LANES] + kr_rot).astype(BF16)
    _store_pairs_with_ones(v_ref, (), kv[:, 1024:1536])
    nq_ref[...] = (nq * NA_SCALE).astype(BF16)
    nk_ref[...] = nk.astype(BF16)
    _store_pairs_with_ones(nv_ref, (), nv)
    if with_state:
        s_ckv_ref, s_kr_ref, s_nk_ref, s_nv_ref = out_refs[6:]
        s_ckv_ref[...] = ckv
        s_kr_ref[...] = kr
        s_nk_ref[...] = nk
        s_nv_ref[...] = nv


def _even_in(x, mod, tokens_per_group, gmix, w_in, qn, kvn, w_uq, w_ukv, cos, sin, with_state):
    n = x.shape[0]
    tm = TOKEN_TILE
    tiles_per_group = tokens_per_group // tm
    rope_tiles = cos.shape[0] // tm
    row = lambda i: (i, 0)
    outs = [jax.ShapeDtypeStruct((n, 1024), BF16), jax.ShapeDtypeStruct((n, 1024), BF16),
            jax.ShapeDtypeStruct((n, 1024), BF16), jax.ShapeDtypeStruct((n, 512), BF16),
            jax.ShapeDtypeStruct((n, 512), BF16), jax.ShapeDtypeStruct((n, 1024), BF16)]
    if with_state:
        outs += [jax.ShapeDtypeStruct((n, 256), F32), jax.ShapeDtypeStruct((n, 128), F32),
                 jax.ShapeDtypeStruct((n, 512), F32), jax.ShapeDtypeStruct((n, 512), F32)]
    return pl.pallas_call(
        functools.partial(_even_in_kernel, with_state=with_state),
        out_shape=outs,
        grid=(n // tm,),
        in_specs=[pl.BlockSpec((tm, D_MODEL), row),
                  pl.BlockSpec((1, 6, D_MODEL), lambda i: (i // tiles_per_group, 0, 0)),
                  _const_spec(gmix.shape), _const_spec(w_in.shape), _const_spec(qn.shape),
                  _const_spec(kvn.shape), _const_spec(w_uq.shape), _const_spec(w_ukv.shape),
                  pl.BlockSpec((tm, LANES), lambda i: (i % rope_tiles, 0)),
                  pl.BlockSpec((tm, LANES), lambda i: (i % rope_tiles, 0))],
        out_specs=[pl.BlockSpec((tm, o.shape[1]), row) for o in outs],
        compiler_params=_params(40 << 20, 1),
        name="even_in",
    )(x, mod, gmix, w_in, qn, kvn, w_uq, w_ukv, cos, sin)


def _cache_expand_kernel(ckv_ref, kr_ref, w_ukv_ref, k_ref, v_ref):
    kv = _dot(ckv_ref[0].astype(BF16), w_ukv_ref[...])
    kr = kr_ref[0]
    for hd in range(MLA_HEADS):
        lo = hd * LANES
        k_ref[0, :, lo:lo + LANES] = (kv[:, lo:lo + LANES] + kr).astype(BF16)
    _store_pairs_with_ones(v_ref, (0,), kv[:, 1024:1536])


def _cache_expand(ckv, kr128, w_ukv):
    b, s, _ = ckv.shape
    return pl.pallas_call(
        _cache_expand_kernel,
        out_shape=[jax.ShapeDtypeStruct((b, s, 1024), BF16), jax.ShapeDtypeStruct((b, s, 1024), BF16)],
        grid=(b,),
        in_specs=[pl.BlockSpec((1, s, MLA_KV_LORA), lambda i: (i, 0, 0)),
                  pl.BlockSpec((1, s, LANES), lambda i: (i, 0, 0)),
                  _const_spec(w_ukv.shape)],
        out_specs=[pl.BlockSpec((1, s, 1024), lambda i: (i, 0, 0)),
                   pl.BlockSpec((1, s, 1024), lambda i: (i, 0, 0))],
        compiler_params=_params(16 << 20, 1),
        name="mla_cache_expand",
    )(ckv, kr128, w_ukv)


def _odd_in_kernel(x_ref, mod_ref, gmix_ref, w_in_ref, qn_ref, kn_ref, avg_ref, cos_ref, sin_ref, *out_refs,
                   with_state):
    q_ref, k_ref, v_ref = out_refs[:3]
    x = x_ref[...]
    mod = mod_ref[0]
    h = (_rms(x, gmix_ref[...]) * (1.0 + mod[1:2]) + mod[0:1]).astype(BF16)
    cos = cos_ref[...]
    sin = sin_ref[...]
    half = GQA_HEAD_DIM // 2
    q_cos, q_sin = qn_ref[0:1] * cos, qn_ref[1:2] * sin
    k_cos, k_sin = kn_ref[0:1] * cos, kn_ref[1:2] * sin

    pair = 2 * LANES

    def project(col):
        return _dot(h, w_in_ref[:, col:col + pair])

    def inv_rms(p):
        sq = p * p
        hi = sq.astype(BF16)
        lo = (sq - hi.astype(F32)).astype(BF16)
        return lax.rsqrt(_dot(hi, avg_ref[...]) + _dot(lo, avg_ref[...]) + RMS_EPS)

    k_off = GQA_HEADS * LANES
    v_off = k_off + GQA_KV_HEADS * LANES
    cols = [g * pair for g in range(GQA_HEADS // 2)] + [k_off, v_off]
    p_next = project(cols[0])
    for idx, col in enumerate(cols[:-1]):
        p, p_next = p_next, project(cols[idx + 1])
        r = inv_rms(p)
        for i in range(2):
            ph = p[:, i * LANES:(i + 1) * LANES]
            rh = r[:, i * LANES:(i + 1) * LANES]
            lo = i * LANES
            if col < k_off:
                qh = (ph * q_cos + pltpu.roll(ph, half, 1) * q_sin) * rh
                q_ref[:, col + lo:col + lo + LANES] = qh.astype(BF16)
            else:
                kh = (ph * k_cos + pltpu.roll(ph, half, 1) * k_sin) * rh
                k_ref[:, lo:lo + LANES] = kh.astype(BF16)
                if with_state:
                    out_refs[3][:, lo:lo + LANES] = ph * rh * kn_ref[0:1]
    ones = jnp.ones((x.shape[0], LANES), BF16)
    for hd in range(GQA_KV_HEADS):
        v_ref[:, 2 * hd * LANES:(2 * hd + 1) * LANES] = p_next[:, hd * LANES:(hd + 1) * LANES].astype(BF16)
        v_ref[:, (2 * hd + 1) * LANES:(2 * hd + 2) * LANES] = ones
    if with_state:
        out_refs[4][...] = p_next


def _odd_in(x, mod, tokens_per_group, gmix, w_in, qn, kn, cos, sin, with_state):
    n = x.shape[0]
    tm = TOKEN_TILE
    tiles_per_group = tokens_per_group // tm
    rope_tiles = cos.shape[0] // tm
    row = lambda i: (i, 0)
    outs = [jax.ShapeDtypeStruct((n, 1024), BF16), jax.ShapeDtypeStruct((n, 256), BF16),
            jax.ShapeDtypeStruct((n, 512), BF16)]
    if with_state:
        outs += [jax.ShapeDtypeStruct((n, 256), F32), jax.ShapeDtypeStruct((n, 256), F32)]
    avg = np.kron(np.eye(2), np.full((GQA_HEAD_DIM, GQA_HEAD_DIM), 1.0 / GQA_HEAD_DIM))
    avg = jnp.asarray(avg, BF16)
    return pl.pallas_call(
        functools.partial(_odd_in_kernel, with_state=with_state),
        out_shape=outs,
        grid=(n // tm,),
        in_specs=[pl.BlockSpec((tm, D_MODEL), row),
                  pl.BlockSpec((1, 6, D_MODEL), lambda i: (i // tiles_per_group, 0, 0)),
                  _const_spec(gmix.shape), _const_spec(w_in.shape), _const_spec(qn.shape),
                  _const_spec(kn.shape), _const_spec(avg.shape),
                  pl.BlockSpec((tm, LANES), lambda i: (i % rope_tiles, 0)),
                  pl.BlockSpec((tm, LANES), lambda i: (i % rope_tiles, 0))],
        out_specs=[pl.BlockSpec((tm, o.shape[1]), row) for o in outs],
        compiler_params=_params(32 << 20, 1),
        name="odd_in",
    )(x, mod, gmix, w_in, qn, kn, avg, cos, sin)


def _attn_kernel(*refs, n_src, heads, q_tile, src_len, k_stride, q_half_mask, pair_out):
    q_ref = refs[0]
    kv_refs = refs[1:1 + 2 * n_src]
    o_ref = refs[1 + 2 * n_src]
    s_ref = refs[2 + 2 * n_src]
    n_tiles = q_ref.shape[1] // q_tile
    chunks = []
    for src in range(n_src):
        ck = min(ATTN_K_CHUNK, src_len[src])
        for c in range(src_len[src] // ck):
            chunks.append((src, c * ck, ck))
    lane = lax.broadcasted_iota(jnp.int32, (q_tile, LANES), 1)

    def rows(t):
        if isinstance(t, int):
            return slice(t * q_tile, (t + 1) * q_tile)
        return pl.ds(pl.multiple_of(t * q_tile, q_tile), q_tile)

    def load_q(t, j):
        if q_half_mask:
            qb = q_ref[0, rows(t), :]
            return jnp.where((lane >= 64) == (j == 1), qb, jnp.zeros_like(qb))
        return q_ref[0, rows(t), j * LANES:(j + 1) * LANES]

    def slot(j, q_next, m_prev):
        if q_next is not None:
            m_part = jnp.full((q_tile, LANES), -jnp.inf, F32)
        if m_prev is not None:
            acc = jnp.zeros((q_tile, 2 * LANES), F32)
        off = 0
        for src, k0, ck in chunks:
            if q_next is not None:
                kc = kv_refs[2 * src][0, j * k_stride:j * k_stride + LANES, k0:k0 + ck]
                s_new = _dot(q_next, kc)
            if m_prev is not None:
                s_old = s_ref[:, off:off + ck]
            if q_next is not None:
                s_ref[:, off:off + ck] = s_new
                for i in range(ck // LANES):
                    m_part = jnp.maximum(m_part, s_new[:, i * LANES:(i + 1) * LANES])
            if m_prev is not None:
                p = jnp.exp2(s_old - m_prev)
                acc = acc + _dot(p.astype(BF16), kv_refs[2 * src + 1][0, k0:k0 + ck, :])
            off += ck
        m_next = None if q_next is None else jnp.max(m_part, axis=-1, keepdims=True)
        o_prev = None if m_prev is None else acc[:, :LANES] / acc[:, LANES:]
        return m_next, o_prev

    def write_out(t, j, o):
        o = o.astype(o_ref.dtype)
        if not pair_out:
            o_ref[0, rows(t), j * LANES:(j + 1) * LANES] = o
        elif j == 0:
            o_ref[0, rows(t), :] = o
        else:
            o_ref[0, rows(t), :] = jnp.where(lane < 64, o_ref[0, rows(t), :], o)

    def tile(j, t, m, last):
        m_next, o = slot(j, None if last else load_q(t + 1, j), m)
        write_out(t, j, o)
        return m_next

    for j in range(heads):
        m, _ = slot(j, load_q(0, j), None)
        if n_tiles > 1:
            m = lax.fori_loop(0, n_tiles - 1, lambda t, m, j=j: tile(j, t, m, False), m)
        tile(j, n_tiles - 1, m, True)


def _attention(q, sources, *, groups, heads, k_stride, q_half_mask, pair_out, q_tile, name):
    b, t, _ = q.shape
    q_block = LANES if q_half_mask else heads * LANES
    k_block = LANES if k_stride == 0 else heads * LANES
    out_block = LANES if pair_out else heads * LANES
    src_len = tuple(k.shape[1] for k, _ in sources)
    in_specs = [pl.BlockSpec((1, t, q_block), lambda bi, g: (bi, 0, g))]
    args = [q]
    for k, v in sources:
        s = k.shape[1]
        in_specs.append(pl.BlockSpec((1, k_block, s), lambda bi, g: (bi, g, 0)))
        in_specs.append(pl.BlockSpec((1, s, 2 * LANES), lambda bi, g: (bi, 0, g)))
        args += [jnp.swapaxes(k, 1, 2), v]
    total = sum(src_len)
    return pl.pallas_call(
        functools.partial(_attn_kernel, n_src=len(sources), heads=heads, q_tile=q_tile,
                          src_len=src_len, k_stride=k_stride, q_half_mask=q_half_mask,
                          pair_out=pair_out),
        out_shape=jax.ShapeDtypeStruct((b, t, groups * out_block), BF16),
        grid=(b, groups),
        in_specs=in_specs,
        out_specs=pl.BlockSpec((1, t, out_block), lambda bi, g: (bi, 0, g)),
        scratch_shapes=[pltpu.VMEM((q_tile, total), F32)],
        compiler_params=_params(58 << 20, 2),
        name=name,
    )(*args)


def _ctx_attn_kernel(q_ref, k_ref, v_ref, o_ref, *, heads, k_stride, q_half_mask, pair_out):
    nb, t, _ = q_ref.shape
    lane = lax.broadcasted_iota(jnp.int32, (t, LANES), 1)
    for b in range(nb):
        outs = []
        for j in range(heads):
            if q_half_mask:
                qb = q_ref[b]
                q = jnp.where((lane >= 64) == (j == 1), qb, jnp.zeros_like(qb))
            else:
                q = q_ref[b, :, j * LANES:(j + 1) * LANES]
            s = _dot_nt(q, k_ref[b, :, j * k_stride:j * k_stride + LANES])
            p = jnp.exp2(s - jnp.max(s, axis=-1, keepdims=True))
            acc = _dot(p.astype(BF16), v_ref[b, :, :LANES])
            outs.append(acc / jnp.sum(p, axis=-1, keepdims=True))
        if pair_out:
            o_ref[b] = jnp.where(lane < 64, outs[0], outs[1]).astype(o_ref.dtype)
        else:
            for j in range(heads):
                o_ref[b, :, j * LANES:(j + 1) * LANES] = outs[j].astype(o_ref.dtype)


def _ctx_attention(q, k, v, *, groups, heads, k_stride, q_half_mask, pair_out, name, v_block=LANES):
    b, t, _ = q.shape
    nb = 4
    q_block = LANES if q_half_mask else heads * LANES
    k_block = LANES if k_stride == 0 else heads * LANES
    out_block = LANES if pair_out else heads * LANES
    spec = lambda w: pl.BlockSpec((nb, t, w), lambda bi, g: (bi, 0, g))
    return pl.pallas_call(
        functools.partial(_ctx_attn_kernel, heads=heads, k_stride=k_stride, q_half_mask=q_half_mask,
                          pair_out=pair_out),
        out_shape=jax.ShapeDtypeStruct((b, t, groups * out_block), BF16),
        grid=(b // nb, groups),
        in_specs=[spec(q_block), spec(k_block), spec(v_block)],
        out_specs=spec(out_block),
        compiler_params=_params(32 << 20, 2),
        name=name,
    )(q, k, v)


def _na_bias_tables(rpb):
    n_rows = GRID_W
    h, n_dr, n_dc = rpb.shape
    edge = n_dc - 1 - (NA_WIN_W - 1)
    w = jnp.concatenate([rpb[..., NA_WIN_W - 1:],
                         jnp.broadcast_to(rpb[..., n_dc - 1:], (h, n_dr, GRID_W - 1 - edge)),
                         jnp.broadcast_to(rpb[..., :1], (h, n_dr, GRID_W - (NA_WIN_W - 1) + 1)),
                         rpb[..., 1:NA_WIN_W - 1]], axis=-1)
    toe = jnp.tile(w, (1, 1, GRID_W))[..., :GRID_W * (2 * GRID_W - 1)]
    toe = toe.reshape(h, n_dr, GRID_W, 2 * GRID_W - 1)[..., :GRID_W]
    cols = np.arange(GRID_W)
    cs = np.clip(cols - NA_WIN_W // 2, 0, GRID_W - NA_WIN_W)
    col_ok = (cols[None, :] >= cs[:, None]) & (cols[None, :] < cs[:, None] + NA_WIN_W)
    toe = jnp.where(col_ok, toe * LOG2E, NEG_INF)
    toe = toe.transpose(0, 2, 1, 3).reshape(h, GRID_W, n_dr * GRID_W)
    pieces = []
    for blk in (0, 1, n_rows // NA_Q_ROWS - 1):
        b0 = int(np.clip(NA_Q_ROWS * blk - NA_WIN_H // 2, 0, n_rows - NA_BAND_ROWS))
        for qr in range(NA_Q_ROWS):
            r = NA_Q_ROWS * blk + qr
            rs = int(np.clip(r - NA_WIN_H // 2, 0, n_rows - NA_WIN_H))
            dr0 = rs - r + NA_WIN_H - 1
            seen = toe[:, None, :, dr0 * GRID_W:(dr0 + NA_WIN_H) * GRID_W]
            before = (rs - b0) * GRID_W
            after = (NA_BAND_ROWS - NA_WIN_H) * GRID_W - before
            pieces.append(jnp.pad(seen, ((0, 0), (0, 0), (0, 0), (before, after)), constant_values=NEG_INF))
    return jnp.concatenate(pieces, axis=1).reshape(h, 3, NA_Q_ROWS * GRID_W, NA_BAND_ROWS * GRID_W)


def _na_kernel(q_ref, k_ref, v_ref, kc_ref, vc_ref, bias_ref, o_ref, s_ref):
    nq = NA_Q_ROWS * GRID_W
    nb = NA_BAND_ROWS * GRID_W
    per = NA_BLOCKS_PER_STEP
    t = k_ref.shape[1]
    n_blocks = t // nq
    n_groups = n_blocks // per
    lane = lax.broadcasted_iota(jnp.int32, (nq, LANES), 1)

    def block_rows(i):
        if isinstance(i, int):
            return slice(i * nq, (i + 1) * nq)
        return pl.ds(pl.multiple_of(i * nq, nq), nq)

    def band_rows(i):
        first = nq * i - (NA_WIN_H // 2) * GRID_W
        if isinstance(i, int):
            first = min(max(first, 0), t - nb)
            return slice(first, first + nb)
        return pl.ds(pl.multiple_of(jnp.clip(first, 0, t - nb), nq), nb)

    def bias_class(i):
        if isinstance(i, int):
            return min(i, 1) + max(i - (n_blocks - 2), 0)
        return jnp.minimum(i, 1) + jnp.maximum(i - (n_blocks - 2), 0)

    def slot(j, g_next, g_prev, ms_prev):
        new_ms = []
        for blk in range(per):
            srow = slice(blk * nq, (blk + 1) * nq)
            if g_next is not None:
                i = g_next * per + blk
                qb = q_ref[0, block_rows(i), :]
                q = jnp.where((lane >= 64) == (j == 1), qb, jnp.zeros_like(qb))
                s_band = _dot_nt(q, k_ref[0, band_rows(i), :]) + bias_ref[j, bias_class(i)]
                s_ctx = _dot_nt(q, kc_ref[0])
            if g_prev is not None:
                old_band = s_ref[srow, :nb]
                old_ctx = s_ref[srow, nb:]
            if g_next is not None:
                s_ref[srow, :nb] = s_band
                s_ref[srow, nb:] = s_ctx
                new_ms.append(jnp.maximum(jnp.max(s_band, axis=-1, keepdims=True),
                                          jnp.max(s_ctx, axis=-1, keepdims=True)))
            if g_prev is not None:
                i = g_prev * per + blk
                m = ms_prev[blk]
                p_band = jnp.exp2(old_band - m)
                p_ctx = jnp.exp2(old_ctx - m)
                acc = (_dot(p_band.astype(BF16), v_ref[0, band_rows(i), :])
                       + _dot(p_ctx.astype(BF16), vc_ref[0]))
                o = (acc[:, :LANES] / acc[:, LANES:]).astype(o_ref.dtype)
                if j == 0:
                    o_ref[0, block_rows(i), :] = o
                else:
                    o_ref[0, block_rows(i), :] = jnp.where(lane < 64, o_ref[0, block_rows(i), :], o)
        return tuple(new_ms)

    for j in range(2):
        ms = slot(j, 0, None, None)
        if n_groups > 1:
            ms = lax.fori_loop(0, n_groups - 1, lambda g, ms, j=j: slot(j, g + 1, g, ms), ms)
        slot(j, None, n_groups - 1, ms)


def _neighbourhood_attention(q, k, v, kc, vc, bias):
    b, t, w = q.shape
    pairs = w // LANES
    nq = NA_Q_ROWS * GRID_W
    nb = NA_BAND_ROWS * GRID_W
    c = kc.shape[1]
    seq = lambda n, lanes=LANES: pl.BlockSpec((1, n, lanes), lambda bi, g: (bi, 0, g))
    return pl.pallas_call(
        _na_kernel,
        out_shape=jax.ShapeDtypeStruct((b, t, w), BF16),
        grid=(b, pairs),
        in_specs=[seq(t), seq(t), seq(t, 2 * LANES), seq(c), seq(c, 2 * LANES),
                  pl.BlockSpec((2, 3, nq, nb), lambda bi, g: (g, 0, 0, 0))],
        out_specs=seq(t),
        scratch_shapes=[pltpu.VMEM((NA_BLOCKS_PER_STEP * nq, nb + c), F32)],
        compiler_params=_params(40 << 20, 2),
        name="neighbourhood_attention",
    )(q, k, v, kc, vc, bias)


def _out_ffn_kernel(*refs, n_attn, ff_chunk, final_norm):
    x_ref, mod_ref = refs[0], refs[1]
    a_refs = refs[2:2 + n_attn]
    w_refs = refs[2 + n_attn:2 + 2 * n_attn]
    gffn_ref, w_in_ref, w_out_ref, gfin_ref, o_ref = refs[2 + 2 * n_attn:]
    mod = mod_ref[0]
    mix = _dot(a_refs[0][...], w_refs[0][...])
    for a_ref, w_ref in zip(a_refs[1:], w_refs[1:]):
        mix = mix + _dot(a_ref[...], w_ref[...])
    x1 = x_ref[...] + mod[2:3] * mix
    h = (_rms(x1, gffn_ref[...]) * (1.0 + mod[4:5]) + mod[3:4]).astype(BF16)
    d_ff = w_out_ref.shape[1]
    acc = None
    for c in range(d_ff // ff_chunk):
        lo = c * ff_chunk
        gate = _dot(h, w_in_ref[0, :, lo:lo + ff_chunk])
        up = _dot(h, w_in_ref[0, :, d_ff + lo:d_ff + lo + ff_chunk])
        act = (gate * jax.nn.sigmoid(gate) * up).astype(BF16)
        part = _dot(act, w_out_ref[0, lo:lo + ff_chunk, :])
        acc = part if acc is None else acc + part
    x2 = x1 + mod[5:6] * acc
    if final_norm:
        x2 = _rms(x2, gfin_ref[...])
    o_ref[...] = x2


def _out_ffn(x, mod, tokens_per_group, attn, w_outs, gffn, layer, w_ffn_in, w_ffn_out, gfin, final_norm):
    n = x.shape[0]
    tm = TOKEN_TILE
    tiles_per_group = tokens_per_group // tm
    row = lambda i: (i, 0)
    in_specs = [pl.BlockSpec((tm, D_MODEL), row),
                pl.BlockSpec((1, 6, D_MODEL), lambda i: (i // tiles_per_group, 0, 0))]
    in_specs += [pl.BlockSpec((tm, a.shape[1]), row) for a in attn]
    in_specs += [_const_spec(w.shape) for w in w_outs]
    layer_spec = lambda w: pl.BlockSpec((1,) + w.shape[1:], lambda i: (layer, 0, 0),
                                        pipeline_mode=pl.Buffered(1))
    in_specs += [_const_spec(gffn.shape), layer_spec(w_ffn_in), layer_spec(w_ffn_out),
                 _const_spec(gfin.shape)]
    return pl.pallas_call(
        functools.partial(_out_ffn_kernel, n_attn=len(attn), ff_chunk=256, final_norm=final_norm),
        out_shape=jax.ShapeDtypeStruct((n, D_MODEL), F32),
        grid=(n // tm,),
        in_specs=in_specs,
        out_specs=pl.BlockSpec((tm, D_MODEL), row),
        compiler_params=_params(56 << 20, 1),
        name="out_ffn",
    )(x, mod, *attn, *w_outs, gffn, w_ffn_in, w_ffn_out, gfin)


def _rope_tables(n_tokens, rot_dim):
    t = np.arange(n_tokens)
    row = (t // GRID_W).astype(np.float32)
    col = (t % GRID_W).astype(np.float32)
    axis_dim = rot_dim // 2
    inv_freq = np.float32(ROPE_THETA) ** (-np.arange(0, axis_dim, 2, dtype=np.float32) / axis_dim)
    ang = np.concatenate([row[:, None] * inv_freq, col[:, None] * inv_freq], axis=-1).astype(np.float32)
    return np.cos(ang), np.sin(ang)


def _mla_rope_lanes(n_tokens):
    cos, sin = _rope_tables(n_tokens, MLA_ROPE_DIM)
    one = np.ones((n_tokens, MLA_NOPE_DIM), np.float32)
    zero = np.zeros((n_tokens, MLA_NOPE_DIM), np.float32)
    pad0 = np.zeros((n_tokens, LANES - MLA_QK_DIM), np.float32)
    return (jnp.asarray(np.concatenate([one, cos, cos, pad0], axis=-1)),
            jnp.asarray(np.concatenate([zero, -sin, sin, pad0], axis=-1)))


def _gqa_rope_lanes(n_tokens):
    cos, sin = _rope_tables(n_tokens, GQA_HEAD_DIM)
    return (jnp.asarray(np.concatenate([cos, cos], axis=-1)),
            jnp.asarray(np.concatenate([-sin, sin], axis=-1)))


def _swap_halves(w):
    half = w.shape[-1] // 2
    return jnp.concatenate([w[..., half:], w[..., :half]], axis=-1)


def _even_weights(w_in, w_uq, w_ukv):
    d = w_in.shape[0]
    i0 = MLA_Q_LORA
    i1 = i0 + MLA_KV_LORA
    i2 = i1 + MLA_ROPE_DIM
    w_kr = w_in[:, i1:i2]
    zl = jnp.zeros((d, MLA_NOPE_DIM), F32)
    w_in_k = jnp.concatenate([w_in[:, :i1], w_in[:, i2:], zl, w_kr, w_kr], axis=-1).astype(BF16)
    r = w_uq.shape[0]
    uq = w_uq.reshape(r, MLA_HEADS, MLA_QK_DIM)
    w_uq_k = jnp.concatenate([uq, uq[..., MLA_NOPE_DIM:]], axis=-1).reshape(r, MLA_HEADS * LANES).astype(BF16)
    r = w_ukv.shape[0]
    ukv = w_ukv.reshape(r, MLA_HEADS, MLA_NOPE_DIM + MLA_V_DIM)
    k_pad = jnp.concatenate([ukv[..., :MLA_NOPE_DIM], jnp.zeros((r, MLA_HEADS, LANES - MLA_NOPE_DIM), F32)],
                            axis=-1).reshape(r, MLA_HEADS * LANES)
    v_cat = ukv[..., MLA_NOPE_DIM:].reshape(r, MLA_HEADS * MLA_V_DIM)
    w_ukv_k = jnp.concatenate([k_pad, v_cat], axis=-1).astype(BF16)
    return w_in_k, w_uq_k, w_ukv_k


def kernel(x_prompt, x_sample, cache_mla_ckv, cache_mla_krope, cache_na_k, cache_na_v, cache_gqa_k, cache_gqa_v, c, c_ctx, w_mod, b_mod, norm_mix, norm_ffn, norm_final, w_in_a, mla_q_norm, mla_w_uq, mla_kv_norm, mla_w_ukv, na_rpb, w_out_a, w_in_c, gqa_q_norm, gqa_k_norm, w_out_c, w_ffn_in, w_ffn_out):
    batch, seq, d = x_prompt.shape
    dec_batch, dec_seq, _ = x_sample.shape
    depth = w_mod.shape[0]
    past = cache_mla_ckv.shape[2]
    n_ctx = batch * seq
    n_lat = dec_batch * dec_seq

    cond = jnp.concatenate([c_ctx[None], c, jnp.zeros((8 - 1 - dec_batch, d), F32)], axis=0)
    mod = _modulation(cond, w_mod, b_mod).reshape(depth, 8, 6, d)

    xp = x_prompt.reshape(n_ctx, d)
    xs = x_sample.reshape(n_lat, d)
    cos_m, sin_m = _mla_rope_lanes(dec_seq)
    cos_g, sin_g = _gqa_rope_lanes(dec_seq)
    ident_cos = jnp.ones((TOKEN_TILE, LANES), F32)
    ident_sin = jnp.zeros((TOKEN_TILE, LANES), F32)
    keep_qk = jnp.asarray(np.broadcast_to(np.arange(LANES) < MLA_QK_DIM, (TOKEN_TILE, LANES)), F32)
    gfin = norm_final.reshape(1, d)
    states = {k: [] for k in ("ckv", "krope", "nk", "nv", "gk", "gv")}
    wfi = w_ffn_in.astype(BF16)
    wfo = w_ffn_out.astype(BF16)

    for l in range(depth):
        mod_p = mod[l, 0:1]
        mod_s = mod[l, 1:1 + dec_batch]
        gmix = norm_mix[l].reshape(1, d)
        gffn = norm_ffn[l].reshape(1, d)
        if l % 2 == 0:
            e = l // 2
            w_in_k, w_uq_k, w_ukv_k = _even_weights(w_in_a[e], mla_w_uq[e], mla_w_ukv[e])
            qn = (mla_q_norm[e] * MLA_SCALE).reshape(1, -1)
            kvn = mla_kv_norm[e].reshape(1, -1)
            (qp, kp, vp, nqp, nkp, nvp, s_ckv, s_kr, s_nk, s_nv) = _even_in(
                xp, mod_p, n_ctx, gmix, w_in_k, qn, kvn, w_uq_k, w_ukv_k, keep_qk, ident_sin, True)
            states["ckv"].append(s_ckv.reshape(batch, seq, MLA_KV_LORA))
            states["krope"].append(s_kr[:, MLA_NOPE_DIM:MLA_QK_DIM].reshape(batch, seq, MLA_ROPE_DIM))
            states["nk"].append(s_nk.reshape(batch, seq, NA_HEADS, NA_HEAD_DIM))
            states["nv"].append(s_nv.reshape(batch, seq, NA_HEADS, NA_HEAD_DIM))
            qs, ks, vs, nqs, nks, nvs = _even_in(
                xs, mod_s, dec_seq, gmix, w_in_k, qn, kvn, w_uq_k, w_ukv_k, cos_m, sin_m, False)
            kr_cache = jnp.pad(cache_mla_krope[:, e],
                               ((0, 0), (0, 0), (MLA_NOPE_DIM, LANES - MLA_QK_DIM)))
            kc, vc = _cache_expand(cache_mla_ckv[:, e], kr_cache, w_ukv_k)

            r3 = lambda a, b_: a.reshape(b_, a.shape[0] // b_, a.shape[1])
            mla_kw = dict(groups=MLA_HEADS // 2, heads=2, k_stride=LANES, q_half_mask=False, pair_out=True)
            na_kw = dict(groups=NA_HEADS // 2, heads=2, k_stride=0, q_half_mask=True, pair_out=True)
            a_mla_p = _ctx_attention(r3(qp, batch), r3(kp, batch), r3(vp, batch), name="mla_ctx",
                                     v_block=2 * LANES, **mla_kw)
            a_na_p = _ctx_attention(r3(nqp, batch), r3(nkp, batch), r3(nvp, batch), name="na_ctx",
                                    v_block=2 * LANES, **na_kw)
            a_mla_s = _attention(r3(qs, dec_batch), [(r3(ks, dec_batch), r3(vs, dec_batch)), (kc, vc)],
                                 q_tile=MLA_Q_TILE, name="mla_lat", **mla_kw)
            bias = _na_bias_tables(na_rpb[e])
            nv_cache = cache_na_v[:, e].astype(BF16).reshape(dec_batch, past, NA_HEADS // 2, LANES)
            nv_cache = jnp.concatenate([nv_cache, jnp.ones_like(nv_cache)], axis=-1)
            a_na_s = _neighbourhood_attention(
                r3(nqs, dec_batch), r3(nks, dec_batch), r3(nvs, dec_batch),
                cache_na_k[:, e].reshape(dec_batch, past, NA_W).astype(BF16),
                nv_cache.reshape(dec_batch, past, -1), bias)
            attn_p = [a_mla_p.reshape(n_ctx, -1), a_na_p.reshape(n_ctx, -1)]
            attn_s = [a_mla_s.reshape(n_lat, -1), a_na_s.reshape(n_lat, -1)]
            wo = w_out_a[e].astype(BF16)
            half = MLA_HEADS * MLA_V_DIM
            w_outs = [wo[:half], wo[half:]]
        else:
            o = l // 2
            w_in_k = w_in_c[o].astype(BF16)
            qn = gqa_q_norm[o] * GQA_SCALE
            qn = jnp.stack([qn, _swap_halves(qn)])
            kn = jnp.stack([gqa_k_norm[o], _swap_halves(gqa_k_norm[o])])
            qp, kp, vp, s_gk, s_gv = _odd_in(xp, mod_p, n_ctx, gmix, w_in_k, qn, kn, ident_cos, ident_sin, True)
            states["gk"].append(s_gk.reshape(batch, seq, GQA_KV_HEADS, GQA_HEAD_DIM))
            states["gv"].append(s_gv.reshape(batch, seq, GQA_KV_HEADS, GQA_HEAD_DIM))
            qs, ks, vs = _odd_in(xs, mod_s, dec_seq, gmix, w_in_k, qn, kn, cos_g, sin_g, False)
            r3 = lambda a, b_: a.reshape(b_, a.shape[0] // b_, a.shape[1])
            gqa_kw = dict(groups=GQA_KV_HEADS, heads=GQA_GROUP, k_stride=0, q_half_mask=False, pair_out=False)
            a_p = _ctx_attention(r3(qp, batch), r3(kp, batch), r3(vp, batch), name="gqa_ctx",
                                 v_block=2 * LANES, **gqa_kw)
            kcache = cache_gqa_k[:, o].reshape(dec_batch, past, -1).astype(BF16)
            vcache = cache_gqa_v[:, o].astype(BF16)
            vcache = jnp.concatenate([vcache, jnp.ones_like(vcache)], axis=-1).reshape(dec_batch, past, -1)
            a_s = _attention(r3(qs, dec_batch), [(r3(ks, dec_batch), r3(vs, dec_batch)), (kcache, vcache)],
                             q_tile=GQA_Q_TILE, name="gqa_lat", **gqa_kw)
            attn_p = [a_p.reshape(n_ctx, -1)]
            attn_s = [a_s.reshape(n_lat, -1)]
            w_outs = [w_out_c[o].astype(BF16)]
        last = l == depth - 1
        xp = _out_ffn(xp, mod_p, n_ctx, attn_p, w_outs, gffn, l, wfi, wfo, gfin, last)
        xs = _out_ffn(xs, mod_s, dec_seq, attn_s, w_outs, gffn, l, wfi, wfo, gfin, last)

    y_prompt = xp.reshape(batch, seq, d)
    y_sample = xs.reshape(dec_batch, dec_seq, d)
    return (y_prompt, y_sample,
            jnp.stack(states["ckv"], axis=1), jnp.stack(states["krope"], axis=1),
            jnp.stack(states["nk"], axis=1), jnp.stack(states["nv"], axis=1),
            jnp.stack(states["gk"], axis=1), jnp.stack(states["gv"], axis=1))
```
